```python
import jax, jax.numpy as jnp
from jax import lax
import numpy as np

D_MODEL = 1024
BATCH = 16
SEQ = 2048
DEPTH = 2

HEAD_DIM = 64
Q_BLOCK = 128
NEG_INF = -1e30
MLA_HEADS = 8
MLA_Q_RANK = 512
MLA_KV_RANK = 256
MLA_NOPE = 64
MLA_ROPE = 32
MLA_V = 64
ROPE_THETA = 10000.0
DIL_PATTERNS = ((128, 1), (512, 4), (2048, 16))
DIL_GROUPS = 3
DIL_HPG = 4
DIL_HEADS = DIL_GROUPS * DIL_HPG
WIN_HEADS = 8
WIN_KV_HEADS = 2
WIN_RADIUS = 128
WIN_BLOCK = 128
MLA_COLS = MLA_Q_RANK + MLA_KV_RANK + MLA_ROPE
DIL_COLS = 3 * DIL_HEADS * HEAD_DIM
WIN_COLS = (WIN_HEADS + 2 * WIN_KV_HEADS) * HEAD_DIM
GATE_COLS = 3 * D_MODEL
IN_COLS = MLA_COLS + DIL_COLS + WIN_COLS + GATE_COLS
MLA_OUT = MLA_HEADS * MLA_V
DIL_OUT = DIL_HPG * HEAD_DIM
WIN_OUT = WIN_HEADS * HEAD_DIM
N_EXPERT_GROUPS = 4
EXPERTS_PER_GROUP = 8
N_EXPERTS = N_EXPERT_GROUPS * EXPERTS_PER_GROUP
TOP_K = 2
D_EXPERT = 384
MOE_BLOCK = 128

kernel_name = 'hybrid_mla_dilated_window_hmoe_encoder'


def rms_norm(x, g, eps=1e-6):
    xf = x.astype(jnp.float32)
    y = xf * lax.rsqrt(jnp.mean(xf * xf, axis=-1, keepdims=True) + eps)
    return (y * g.astype(jnp.float32)).astype(x.dtype)


def alibi_slopes(n):
    return 2.0 ** (-8.0 * jnp.arange(1, n + 1, dtype=jnp.float32) / n)


def apply_rope(x, pos):
    half = x.shape[-1] // 2
    inv_freq = ROPE_THETA ** (-jnp.arange(half, dtype=jnp.float32) / half)
    ang = pos.astype(jnp.float32)[..., None] * inv_freq
    ang = ang.reshape(ang.shape[:2] + (1,) * (x.ndim - 3) + (half,))
    cos, sin = jnp.cos(ang), jnp.sin(ang)
    xf = x.astype(jnp.float32)
    x1, x2 = xf[..., :half], xf[..., half:]
    return jnp.concatenate([x1 * cos - x2 * sin, x1 * sin + x2 * cos], axis=-1).astype(x.dtype)


def to_blocks(t, block):
    B, S = t.shape[:2]
    return jnp.moveaxis(t.reshape(B, S // block, block, *t.shape[2:]), 1, 0)


def from_blocks(o):
    o = jnp.moveaxis(o, 0, 1)
    return o.reshape(o.shape[0], o.shape[1] * o.shape[2], -1)


def mla_mixer(c_q, c_kv, k_rope, pos, g_cq, w_uq, g_ckv, w_ukv, g_q, g_k):
    B, S, _ = c_q.shape
    q = (rms_norm(c_q, g_cq) @ w_uq).reshape(B, S, MLA_HEADS, MLA_NOPE + MLA_ROPE)
    kv = (rms_norm(c_kv, g_ckv) @ w_ukv).reshape(B, S, MLA_HEADS, MLA_NOPE + MLA_V)
    q_nope = rms_norm(q[..., :MLA_NOPE], g_q[:MLA_NOPE])
    q_rope = apply_rope(rms_norm(q[..., MLA_NOPE:], g_q[MLA_NOPE:]), pos)
    k_nope = rms_norm(kv[..., :MLA_NOPE], g_k[:MLA_NOPE])
    v = kv[..., MLA_NOPE:]
    k_rope = apply_rope(rms_norm(k_rope, g_k[MLA_NOPE:]), pos)
    scale = (MLA_NOPE + MLA_ROPE) ** -0.5

    def attend(blk):
        qn, qr = blk
        s = (jnp.einsum('bqhd,bkhd->bhqk', qn, k_nope)
             + jnp.einsum('bqhd,bkd->bhqk', qr, k_rope)).astype(jnp.float32) * scale
        p = jax.nn.softmax(s, axis=-1)
        return jnp.einsum('bhqk,bkhd->bqhd', p.astype(v.dtype), v)

    o = lax.map(attend, (to_blocks(q_nope, Q_BLOCK), to_blocks(q_rope, Q_BLOCK)))
    return from_blocks(o)


def dilated_group(q, k, v, pos, window, dil, slope):
    B, S = q.shape[:2]
    n_side = (window // 2) // dil
    offsets = jnp.arange(-n_side, n_side + 1, dtype=jnp.int32) * dil
    starts = jnp.arange(S // Q_BLOCK, dtype=jnp.int32) * Q_BLOCK
    scale = HEAD_DIM ** -0.5

    def attend(args):
        qb, start = args
        q_idx = start + jnp.arange(Q_BLOCK, dtype=jnp.int32)
        k_idx = q_idx[:, None] + offsets[None, :]
        valid = (k_idx >= 0) & (k_idx < S)
        k_idx = jnp.clip(k_idx, 0, S - 1)
        kb = jnp.take(k, k_idx, axis=1)
        vb = jnp.take(v, k_idx, axis=1)
        pos_q = lax.dynamic_slice_in_dim(pos, start, Q_BLOCK, axis=1)
        pos_k = jnp.take(pos, k_idx, axis=1)
        dist = jnp.abs(pos_q[:, :, None] - pos_k).astype(jnp.float32)
        s = jnp.einsum('bqhd,bqjhd->bhqj', qb, kb).astype(jnp.float32) * scale
        s = s - slope[None, :, None, None] * dist[:, None]
        s = jnp.where(valid[None, None], s, NEG_INF)
        lse = jax.nn.logsumexp(s, axis=-1)
        p = jnp.exp(s - lse[..., None])
        o = jnp.einsum('bhqj,bqjhd->bqhd', p.astype(vb.dtype), vb)
        return o, lse

    return lax.map(attend, (to_blocks(q, Q_BLOCK), starts))


def dilated_mixer(qkv, pos, g_q, g_k):
    B, S, _ = qkv.shape
    qkv = qkv.reshape(B, S, 3, DIL_GROUPS, DIL_HPG, HEAD_DIM)
    q = rms_norm(qkv[:, :, 0], g_q)
    k = rms_norm(qkv[:, :, 1], g_k)
    v = qkv[:, :, 2]
    slopes = alibi_slopes(DIL_HEADS).reshape(DIL_GROUPS, DIL_HPG)
    outs, lses = [], []
    for gi, (window, dil) in enumerate(DIL_PATTERNS):
        o, lse = dilated_group(q[:, :, gi], k[:, :, gi], v[:, :, gi], pos, window, dil, slopes[gi])
        outs.append(o)
        lses.append(lse)
    outs = jnp.stack(outs)
    w = jax.nn.softmax(jnp.stack(lses), axis=0)
    o = jnp.einsum('gnbhq,gnbqhd->nbqhd', w.astype(outs.dtype), outs)
    return from_blocks(o)


def window_mixer(qkv, pos, g_q, g_k, sink):
    B, S, _ = qkv.shape
    rep = WIN_HEADS // WIN_KV_HEADS
    nq, nk = WIN_HEADS * HEAD_DIM, WIN_KV_HEADS * HEAD_DIM
    q = rms_norm(qkv[..., :nq].reshape(B, S, WIN_KV_HEADS, rep, HEAD_DIM), g_q)
    k = rms_norm(qkv[..., nq:nq + nk].reshape(B, S, WIN_KV_HEADS, HEAD_DIM), g_k)
    v = qkv[..., nq + nk:].reshape(B, S, WIN_KV_HEADS, HEAD_DIM)
    pad = ((0, 0), (WIN_BLOCK, WIN_BLOCK), (0, 0), (0, 0))
    kp, vp = jnp.pad(k, pad), jnp.pad(v, pad)
    pp = jnp.pad(pos, ((0, 0), (WIN_BLOCK, WIN_BLOCK)))
    slopes = alibi_slopes(WIN_HEADS).reshape(WIN_KV_HEADS, rep)
    sink_l = sink.astype(jnp.float32).reshape(1, WIN_KV_HEADS, rep, 1, 1)
    starts = jnp.arange(S // WIN_BLOCK, dtype=jnp.int32) * WIN_BLOCK
    scale = HEAD_DIM ** -0.5
    band = 3 * WIN_BLOCK

    def attend(args):
        qb, start = args
        kb = lax.dynamic_slice_in_dim(kp, start, band, axis=1)
        vb = lax.dynamic_slice_in_dim(vp, start, band, axis=1)
        pos_k = lax.dynamic_slice_in_dim(pp, start, band, axis=1)
        pos_q = lax.dynamic_slice_in_dim(pos, start, WIN_BLOCK, axis=1)
        q_idx = start + jnp.arange(WIN_BLOCK, dtype=jnp.int32)
        k_idx = start - WIN_BLOCK + jnp.arange(band, dtype=jnp.int32)
        valid = ((jnp.abs(q_idx[:, None] - k_idx[None, :]) <= WIN_RADIUS)
                 & (k_idx >= 0)[None, :] & (k_idx < S)[None, :])
        dist = jnp.abs(pos_q[:, :, None] - pos_k[:, None, :]).astype(jnp.float32)
        s = jnp.einsum('bqgrd,bkgd->bgrqk', qb, kb).astype(jnp.float32) * scale
        s = s - slopes[None, :, :, None, None] * dist[:, None, None]
        s = jnp.where(valid, s, NEG_INF)
        m = jnp.maximum(jnp.max(s, axis=-1, keepdims=True), sink_l)
        e = jnp.exp(s - m)
        p = e / (jnp.sum(e, axis=-1, keepdims=True) + jnp.exp(sink_l - m))
        return jnp.einsum('bgrqk,bkgd->bqgrd', p.astype(vb.dtype), vb)

    o = lax.map(attend, (to_blocks(q, WIN_BLOCK), starts))
    return from_blocks(o)


def hybrid_mixer(h, pos, w_in, g_cq, w_uq, g_ckv, w_ukv, g_q_mla, g_k_mla, g_q_dil, g_k_dil,
                 g_q_win, g_k_win, sink_win, w_br_mla, w_br_dil, w_br_win, w_out):
    cols = h @ w_in
    sizes = [MLA_Q_RANK, MLA_KV_RANK, MLA_ROPE, DIL_COLS, WIN_COLS]
    bounds, acc = [], 0
    for sz in sizes:
        acc += sz
        bounds.append(acc)
    c_q, c_kv, k_rope, dil_qkv, win_qkv, gate_cols = jnp.split(cols, bounds, axis=-1)
    g_mla, g_dil, g_win = jnp.split(jax.nn.sigmoid(gate_cols), 3, axis=-1)
    y_mla = mla_mixer(c_q, c_kv, k_rope, pos, g_cq, w_uq, g_ckv, w_ukv, g_q_mla, g_k_mla) @ w_br_mla
    y_dil = dilated_mixer(dil_qkv, pos, g_q_dil, g_k_dil) @ w_br_dil
    y_win = window_mixer(win_qkv, pos, g_q_win, g_k_win, sink_win) @ w_br_win
    return (g_mla * y_mla + g_dil * y_dil + g_win * y_win) @ w_out


def hier_moe(h, w_gr, b_gr, w_er, b_er, w1, w3, w2):
    B, S, D = h.shape
    t = h.reshape(-1, D)
    T = t.shape[0]
    g_prob = jax.nn.softmax((t @ w_gr).astype(jnp.float32) + b_gr.astype(jnp.float32), axis=-1)
    g_w, g_idx = lax.top_k(g_prob, 1)
    e_logits = ((t @ w_er).astype(jnp.float32) + b_er.astype(jnp.float32)).reshape(
        T, N_EXPERT_GROUPS, EXPERTS_PER_GROUP)
    e_logits = jnp.take_along_axis(e_logits, g_idx[:, :, None], axis=1)[:, 0]
    e_w, e_idx = lax.top_k(jax.nn.softmax(e_logits, axis=-1), TOP_K)
    gate = g_w * e_w / jnp.sum(e_w, axis=-1, keepdims=True)
    expert = g_idx * EXPERTS_PER_GROUP + e_idx
    flat_e, flat_gate = expert.reshape(-1), gate.reshape(-1)
    n_assign = flat_e.shape[0]
    order = jnp.argsort(flat_e)
    sorted_e = flat_e[order]
    token = order // TOP_K
    counts = jnp.bincount(flat_e, length=N_EXPERTS)
    padded = (counts + MOE_BLOCK - 1) // MOE_BLOCK * MOE_BLOCK
    start_sorted = jnp.cumsum(counts) - counts
    start_padded = jnp.cumsum(padded) - padded
    dest = start_padded[sorted_e] + jnp.arange(n_assign) - start_sorted[sorted_e]
    n_blocks = -(-n_assign // MOE_BLOCK) + N_EXPERTS
    rows = jnp.zeros((n_blocks * MOE_BLOCK, D), t.dtype).at[dest].set(t[token])
    block_start = jnp.arange(n_blocks) * MOE_BLOCK
    block_expert = jnp.minimum(
        jnp.sum(block_start[:, None] >= (start_padded + padded)[None, :], axis=1), N_EXPERTS - 1)

    def expert_ffn(args):
        xb, e = args
        return (jax.nn.silu(xb @ w1[e]) * (xb @ w3[e])) @ w2[e]

    y = lax.map(expert_ffn, (rows.reshape(n_blocks, MOE_BLOCK, D), block_expert)).reshape(-1, D)
    y = y[dest] * flat_gate[order][:, None].astype(y.dtype)
    return jnp.zeros_like(t).at[token].add(y).reshape(B, S, D)


def setup_inputs(seed: int = 0) -> dict:
    key = jax.random.key(seed)
    ks = iter(jax.random.split(key, 40))
    L, D = DEPTH, D_MODEL

    def nrm(shape, fan_in, s=1.0):
        return (s * fan_in ** -0.5) * jax.random.normal(next(ks), shape, jnp.float32)

    def gain(shape):
        return 1.0 + 0.05 * jax.random.normal(next(ks), shape, jnp.float32)

    x = jax.random.normal(next(ks), (BATCH, SEQ, D), jnp.float32)
    c = jax.random.normal(next(ks), (BATCH, D), jnp.float32)
    pos = (jnp.arange(SEQ, dtype=jnp.int32)[None, :]
           + jax.random.randint(next(ks), (BATCH, 1), 0, SEQ, dtype=jnp.int32))
    return {
        'x': x, 'c': c, 'pos': pos,
        'w_ada': nrm((L, D, 6 * D), D, 0.5),
        'b_ada': 0.02 * jax.random.normal(next(ks), (L, 6 * D), jnp.float32),
        'g_norm1': gain((L, D)),
        'w_in': nrm((L, D, IN_COLS), D),
        'g_cq': gain((L, MLA_Q_RANK)),
        'w_uq': nrm((L, MLA_Q_RANK, MLA_HEADS * (MLA_NOPE + MLA_ROPE)), MLA_Q_RANK),
        'g_ckv': gain((L, MLA_KV_RANK)),
        'w_ukv': nrm((L, MLA_KV_RANK, MLA_HEADS * (MLA_NOPE + MLA_V)), MLA_KV_RANK),
        'g_q_mla': gain((L, MLA_NOPE + MLA_ROPE)),
        'g_k_mla': gain((L, MLA_NOPE + MLA_ROPE)),
        'g_q_dil': gain((L, HEAD_DIM)),
        'g_k_dil': gain((L, HEAD_DIM)),
        'g_q_win': gain((L, HEAD_DIM)),
        'g_k_win': gain((L, HEAD_DIM)),
        'sink_win': 0.5 * jax.random.normal(next(ks), (L, WIN_HEADS), jnp.float32),
        'w_br_mla': nrm((L, MLA_OUT, D), MLA_OUT),
        'w_br_dil': nrm((L, DIL_OUT, D), DIL_OUT),
        'w_br_win': nrm((L, WIN_OUT, D), WIN_OUT),
        'w_out': nrm((L, D, D), D),
        'g_norm2': gain((L, D)),
        'w_gr': nrm((L, D, N_EXPERT_GROUPS), D),
        'b_gr': 0.01 * jax.random.normal(next(ks), (L, N_EXPERT_GROUPS), jnp.float32),
        'w_er': nrm((L, D, N_EXPERTS), D),
        'b_er': 0.01 * jax.random.normal(next(ks), (L, N_EXPERTS), jnp.float32),
        'w1': nrm((L, N_EXPERTS, D, D_EXPERT), D),
        'w3': nrm((L, N_EXPERTS, D, D_EXPERT), D),
        'w2': nrm((L, N_EXPERTS, D_EXPERT, D), D_EXPERT),
    }


def reference(x, c, pos, w_ada, b_ada, g_norm1, w_in, g_cq, w_uq, g_ckv, w_ukv, g_q_mla, g_k_mla,
              g_q_dil, g_k_dil, g_q_win, g_k_win, sink_win, w_br_mla, w_br_dil, w_br_win, w_out,
              g_norm2, w_gr, b_gr, w_er, b_er, w1, w3, w2):
    cond = jax.nn.silu(c)
    for l in range(DEPTH):
        mod = cond @ w_ada[l] + b_ada[l]
        sh1, sc1, gt1, sh2, sc2, gt2 = jnp.split(mod[:, None, :], 6, axis=-1)
        h = rms_norm(x, g_norm1[l]) * (1 + sc1) + sh1
        x = x + gt1 * hybrid_mixer(h, pos, w_in[l], g_cq[l], w_uq[l], g_ckv[l], w_ukv[l],
                                   g_q_mla[l], g_k_mla[l], g_q_dil[l], g_k_dil[l],
                                   g_q_win[l], g_k_win[l], sink_win[l],
                                   w_br_mla[l], w_br_dil[l], w_br_win[l], w_out[l])
        h = rms_norm(x, g_norm2[l]) * (1 + sc2) + sh2
        x = x + gt2 * hier_moe(h, w_gr[l], b_gr[l], w_er[l], b_er[l], w1[l], w3[l], w2[l])
    return x
```

```python
import functools

import jax
import jax.numpy as jnp
from jax import lax
from jax.experimental import pallas as pl
from jax.experimental.pallas import tpu as pltpu

F32 = jnp.float32
BF16 = jnp.bfloat16
I32 = jnp.int32

D_MODEL = 1024
HEAD_DIM = 64
NEG_INF = -1e30
EPS = 1e-6
MLA_HEADS = 8
MLA_Q_RANK = 512
MLA_KV_RANK = 256
MLA_NOPE = 64
MLA_ROPE = 32
MLA_V = 64
ROPE_THETA = 10000.0
DIL_PATTERNS = ((128, 1), (512, 4), (2048, 16))
DIL_GROUPS = 3
DIL_HPG = 4
DIL_HEADS = DIL_GROUPS * DIL_HPG
WIN_HEADS = 8
WIN_KV_HEADS = 2
WIN_RADIUS = 128
N_EXPERT_GROUPS = 4
EXPERTS_PER_GROUP = 8
N_EXPERTS = N_EXPERT_GROUPS * EXPERTS_PER_GROUP
D_EXPERT = 384

LANES = 128
SLOT = 128
VMEM_LIMIT = 48 * 1024 * 1024

OFF_GATE = 0
OFF_CQ = 3072
OFF_DIL = 3584
OFF_CKV = 5888
OFF_WQ = 6144
OFF_WK = 6656
OFF_WV = 6784
OFF_KR = 6912
N_COLS = 7168

MOE_BM = 256


def _cparams(sem, vmem=None):
    return pltpu.CompilerParams(dimension_semantics=sem, vmem_limit_bytes=vmem)


def _tile(n, pref):
    t = min(n, pref)
    assert n % t == 0, (n, pref)
    return t


def _dot(a, b):
    return jnp.dot(a, b, preferred_element_type=F32)


def _dot_nt(a, b):
    return lax.dot_general(a, b, (((1,), (1,)), ((), ())), preferred_element_type=F32)


def _seg_mean_sq(x, mseg):
    x2 = x * x
    hi = x2.astype(BF16)
    lo = (x2 - hi.astype(F32)).astype(BF16)
    return _dot(hi, mseg) + _dot(lo, mseg)


def _rms(x, gain):
    ms = jnp.mean(x * x, axis=-1, keepdims=True)
    return x * lax.rsqrt(ms + EPS) * gain


def _mod_kernel(c_ref, w_ref, b_ref, o_ref):
    c = c_ref[...]
    cond = (c * jax.nn.sigmoid(c)).astype(BF16)
    o_ref[0] = _dot(cond, w_ref[0].astype(BF16)) + b_ref[0]


def _modulation(c, w_ada, b_ada):
    depth, d, n = w_ada.shape
    b = c.shape[0]
    tn = _tile(n, 1536)
    return pl.pallas_call(
        _mod_kernel,
        out_shape=jax.ShapeDtypeStruct((depth, b, n), F32),
        grid=(depth, n // tn),
        in_specs=[
            pl.BlockSpec((b, d), lambda l, j: (0, 0)),
            pl.BlockSpec((1, d, tn), lambda l, j: (l, 0, j)),
            pl.BlockSpec((1, 1, tn), lambda l, j: (l, 0, j)),
        ],
        out_specs=pl.BlockSpec((1, b, tn), lambda l, j: (l, 0, j)),
        compiler_params=_cparams(("parallel", "parallel")),
        name="adaln_mod",
    )(c, w_ada, b_ada.reshape(depth, 1, n))


def _rope_table_kernel(pos_ref, invf_ref, cos_ref, sin_ref):
    ang = pos_ref[...].astype(F32) * invf_ref[...]
    cos_ref[...] = jnp.cos(ang)
    sin_ref[...] = jnp.sin(ang)


def _rope_tables(pos_col, invf_slot):
    t = pos_col.shape[0]
    tm = _tile(t, 1024)
    return pl.pallas_call(
        _rope_table_kernel,
        out_shape=(jax.ShapeDtypeStruct((t, SLOT), F32), jax.ShapeDtypeStruct((t, SLOT), F32)),
        grid=(t // tm,),
        in_specs=[pl.BlockSpec((tm, 1), lambda i: (i, 0)), pl.BlockSpec((1, SLOT), lambda i: (0, 0))],
        out_specs=(pl.BlockSpec((tm, SLOT), lambda i: (i, 0)), pl.BlockSpec((tm, SLOT), lambda i: (i, 0))),
        compiler_params=_cparams(("parallel",)),
        name="rope_tables",
    )(pos_col, invf_slot)


def _inproj_kernel(x_ref, mod_ref, g_ref, w_ref, o_ref, h_scr):
    @pl.when(pl.program_id(1) == 0)
    def _():
        m = mod_ref[0]
        h = _rms(x_ref[...], g_ref[...]) * (1.0 + m[1:2]) + m[0:1]
        h_scr[...] = h.astype(BF16)

    o_ref[...] = _dot(h_scr[...], w_ref[...]).astype(o_ref.dtype)


def _in_projection(x2d, mod_l, g_norm, w_in_l, seq):
    t, d = x2d.shape
    nc = w_in_l.shape[1]
    tm = _tile(seq, 1024)
    tn = _tile(nc, 1024)
    per_b = seq // tm
    return pl.pallas_call(
        _inproj_kernel,
        out_shape=jax.ShapeDtypeStruct((t, nc), BF16),
        grid=(t // tm, nc // tn),
        in_specs=[
            pl.BlockSpec((tm, d), lambda i, j: (i, 0)),
            pl.BlockSpec((1, 6, d), lambda i, j: (i // per_b, 0, 0)),
            pl.BlockSpec((1, d), lambda i, j: (0, 0)),
            pl.BlockSpec((d, tn), lambda i, j: (0, j)),
        ],
        out_specs=pl.BlockSpec((tm, tn), lambda i, j: (i, j)),
        scratch_shapes=[pltpu.VMEM((tm, d), BF16)],
        compiler_params=_cparams(("parallel", "arbitrary"), VMEM_LIMIT),
        name="in_projection",
    )(x2d, mod_l, g_norm, w_in_l)


def _mla_prep_kernel(cq_ref, ckv_ref, kr_ref, cos_ref, sin_ref, gcq_ref, gckv_ref, wuq_ref, wukv_ref,
                     gq_ref, gkn_ref, gkr_ref, mseg_ref, q_out, k_out, v_out):
    cos = cos_ref[...]
    sin = sin_ref[...]
    lane = lax.broadcasted_iota(I32, (1, SLOT), 1)
    s_neg = jnp.where((lane >= 64) & (lane < 80), -sin, 0.0)
    s_pos = jnp.where((lane >= 80) & (lane < 96), sin, 0.0)
    mseg = mseg_ref[...]

    def rope(xn):
        return xn * cos + pltpu.roll(xn, SLOT - 16, 1) * s_neg + pltpu.roll(xn, 16, 1) * s_pos

    def seg_norm(xs, gain):
        return xs * lax.rsqrt(_seg_mean_sq(xs, mseg) + EPS) * gain

    cqn = _rms(cq_ref[...].astype(F32), gcq_ref[...]).astype(BF16)
    q = _dot(cqn, wuq_ref[...])
    for h in range(MLA_HEADS):
        sl = slice(h * SLOT, (h + 1) * SLOT)
        q_out[:, sl] = rope(seg_norm(q[:, sl], gq_ref[...])).astype(q_out.dtype)

    ckvn = _rms(ckv_ref[...].astype(F32), gckv_ref[...]).astype(BF16)
    kv = _dot(ckvn, wukv_ref[...])
    kr = rope(seg_norm(kr_ref[...].astype(F32), gkr_ref[...]))
    for h in range(MLA_HEADS):
        sl = slice(h * SLOT, (h + 1) * SLOT)
        k_out[:, sl] = (seg_norm(kv[:, sl], gkn_ref[...]) + kr).astype(k_out.dtype)
    v_out[...] = kv[:, MLA_HEADS * SLOT:].astype(v_out.dtype)


def _mla_prep(cols, cos_t, sin_t, gcq, gckv, wuq, wukv, gq_slot, gkn_slot, gkr_slot, mseg):
    t = cols.shape[0]
    tm = _tile(t, 512)
    hs = MLA_HEADS * SLOT
    full = lambda shape: pl.BlockSpec(shape, lambda i: (0,) * len(shape))
    return pl.pallas_call(
        _mla_prep_kernel,
        out_shape=(jax.ShapeDtypeStruct((t, hs), BF16), jax.ShapeDtypeStruct((t, hs), BF16),
                   jax.ShapeDtypeStruct((t, MLA_HEADS * MLA_V), BF16)),
        grid=(t // tm,),
        in_specs=[
            pl.BlockSpec((tm, MLA_Q_RANK), lambda i: (i, OFF_CQ // MLA_Q_RANK)),
            pl.BlockSpec((tm, MLA_KV_RANK), lambda i: (i, OFF_CKV // MLA_KV_RANK)),
            pl.BlockSpec((tm, SLOT), lambda i: (i, OFF_KR // SLOT)),
            pl.BlockSpec((tm, SLOT), lambda i: (i, 0)),
            pl.BlockSpec((tm, SLOT), lambda i: (i, 0)),
            full((1, MLA_Q_RANK)), full((1, MLA_KV_RANK)),
            full((MLA_Q_RANK, hs)), full((MLA_KV_RANK, hs + MLA_HEADS * MLA_V)),
            full((1, SLOT)), full((1, SLOT)), full((1, SLOT)), full((SLOT, SLOT)),
        ],
        out_specs=(pl.BlockSpec((tm, hs), lambda i: (i, 0)), pl.BlockSpec((tm, hs), lambda i: (i, 0)),
                   pl.BlockSpec((tm, MLA_HEADS * MLA_V), lambda i: (i, 0))),
        compiler_params=_cparams(("parallel",), VMEM_LIMIT),
        name="mla_prep",
    )(cols, cols, cols, cos_t, sin_t, gcq, gckv, wuq, wukv, gq_slot, gkn_slot, gkr_slot, mseg)


def _mla_attn_kernel(q_ref, k_ref, v_ref, o_ref):
    outs = []
    for a in range(2):
        sl = slice(a * SLOT, (a + 1) * SLOT)
        s = _dot_nt(q_ref[0, :, sl], k_ref[0, :, sl])
        m = jnp.max(s, axis=-1, keepdims=True)
        p = jnp.exp(s - m)
        l = jnp.sum(p, axis=-1, keepdims=True)
        outs.append(_dot(p.astype(BF16), v_ref[0]) / l)
    lane = lax.broadcasted_iota(I32, (1, 2 * MLA_V), 1)
    o_ref[0] = jnp.where(lane < MLA_V, outs[0], outs[1]).astype(o_ref.dtype)


def _mla_attention(qm, km, vm, batch, seq):
    hs = MLA_HEADS * SLOT
    q3 = qm.reshape(batch, seq, hs)
    k3 = km.reshape(batch, seq, hs)
    v3 = vm.reshape(batch, seq, MLA_HEADS * MLA_V)
    tq = _tile(seq, 256)
    out = pl.pallas_call(
        _mla_attn_kernel,
        out_shape=jax.ShapeDtypeStruct((batch, seq, MLA_HEADS * MLA_V), BF16),
        grid=(batch, MLA_HEADS // 2, seq // tq),
        in_specs=[
            pl.BlockSpec((1, tq, 2 * SLOT), lambda b, p, i: (b, i, p)),
            pl.BlockSpec((1, seq, 2 * SLOT), lambda b, p, i: (b, 0, p)),
            pl.BlockSpec((1, seq, 2 * MLA_V), lambda b, p, i: (b, 0, p)),
        ],
        out_specs=pl.BlockSpec((1, tq, 2 * MLA_V), lambda b, p, i: (b, i, p)),
        compiler_params=_cparams(("parallel", "parallel", "arbitrary"), VMEM_LIMIT),
        name="mla_attention",
    )(q3, k3, v3)
    return out.reshape(batch * seq, MLA_HEADS * MLA_V)


def _window_start(i, bq, radius, n, kw):
    ws = jnp.clip(i * bq - radius, 0, n - kw)
    return pl.multiple_of(ws, 16)


def _band_valid(i, bq, ws, kw, radius):
    rows = lax.broadcasted_iota(I32, (bq, kw), 0)
    cols = lax.broadcasted_iota(I32, (bq, kw), 1)
    rel = rows - cols + (i * bq - ws)
    return jnp.maximum(rel, -rel) <= radius


def _abs_dist(pq, pk):
    d = pq - pk
    return jnp.maximum(d, -d).astype(F32)


def _dil_kernel(q_ref, k_ref, v_ref, pq_ref, pk_ref, gq_ref, gk_ref, mseg_ref, o_ref, lse_ref, kn_scr,
                *, n, bq, kw, slopes):
    i = pl.program_id(2)
    mseg = mseg_ref[...]

    @pl.when(i == 0)
    def _():
        k = k_ref[0].astype(F32)
        kn_scr[...] = (k * lax.rsqrt(_seg_mean_sq(k, mseg) + EPS) * gk_ref[...]).astype(BF16)

    q = q_ref[0].astype(F32)
    qn = q * lax.rsqrt(_seg_mean_sq(q, mseg) + EPS) * gq_ref[...]
    ws = _window_start(i, bq, 64, n, kw)
    kwin = kn_scr[pl.ds(ws, kw), :]
    vwin = v_ref[0, pl.ds(ws, kw), :]
    dist = _abs_dist(pq_ref[0], pk_ref[0])
    valid = _band_valid(i, bq, ws, kw, 64)
    head = lax.broadcasted_iota(I32, (1, DIL_HPG * HEAD_DIM), 1) >> 6
    out = jnp.zeros((bq, DIL_HPG * HEAD_DIM), F32)
    lse = jnp.zeros((bq, DIL_HPG * HEAD_DIM), F32)
    for hh in range(DIL_HPG):
        hm = head == hh
        s = _dot_nt(jnp.where(hm, qn, 0.0).astype(BF16), kwin) - slopes[hh] * dist
        s = jnp.where(valid, s, NEG_INF)
        m = jnp.max(s, axis=-1, keepdims=True)
        p = jnp.exp(s - m)
        l = jnp.sum(p, axis=-1, keepdims=True)
        oh = _dot(p.astype(BF16), vwin) / l
        out = jnp.where(hm, oh, out)
        lse = jnp.where(hm, m + jnp.log(l), lse)
    o_ref[0] = out
    lse_ref[0] = lse


def _dilated_group(cols, pos, gq_row, gk_row, mseg, gi, batch, seq):
    _, dil = DIL_PATTERNS[gi]
    n = seq // dil
    bq = min(128, n)
    kw = min(bq + 128, n)
    nblk = n // bq
    width = DIL_HPG * HEAD_DIM
    cols_v = cols.reshape(batch, n, dil * N_COLS)
    cpb = N_COLS // width
    base = OFF_DIL // width
    pos_sub = jnp.transpose(pos.reshape(batch, n, dil), (0, 2, 1)).reshape(batch * dil, n)
    pq = pos_sub.reshape(batch * dil, n, 1)
    starts = [min(max(i * bq - 64, 0), n - kw) for i in range(nblk)]
    pk = jnp.stack([pos_sub[:, s:s + kw] for s in starts], axis=1).reshape(batch * dil * nblk, 1, kw)
    slopes = tuple(float(2.0 ** (-8.0 * (gi * DIL_HPG + hh + 1) / DIL_HEADS)) for hh in range(DIL_HPG))
    kern = functools.partial(_dil_kernel, n=n, bq=bq, kw=kw, slopes=slopes)
    o, lse = pl.pallas_call(
        kern,
        out_shape=(jax.ShapeDtypeStruct((batch, n, dil * width), F32),
                   jax.ShapeDtypeStruct((batch, n, dil * width), F32)),
        grid=(batch, dil, nblk),
        in_specs=[
            pl.BlockSpec((1, bq, width), lambda b, r, i: (b, i, r * cpb + base + gi)),
            pl.BlockSpec((1, n, width), lambda b, r, i: (b, 0, r * cpb + base + DIL_GROUPS + gi)),
            pl.BlockSpec((1, n, width), lambda b, r, i: (b, 0, r * cpb + base + 2 * DIL_GROUPS + gi)),
            pl.BlockSpec((1, bq, 1), lambda b, r, i: (b * dil + r, i, 0)),
            pl.BlockSpec((1, 1, kw), lambda b, r, i: ((b * dil + r) * nblk + i, 0, 0)),
            pl.BlockSpec((1, width), lambda b, r, i: (0, 0)),
            pl.BlockSpec((1, width), lambda b, r, i: (0, 0)),
            pl.BlockSpec((width, width), lambda b, r, i: (0, 0)),
        ],
        out_specs=(pl.BlockSpec((1, bq, width), lambda b, r, i: (b, i, r)),
                   pl.BlockSpec((1, bq, width), lambda b, r, i: (b, i, r))),
        scratch_shapes=[pltpu.VMEM((n, width), BF16)],
        compiler_params=_cparams(("parallel", "parallel", "arbitrary"), VMEM_LIMIT),
        name=f"dilated_group{gi}",
    )(cols_v, cols_v, cols_v, pq, pk, gq_row, gk_row, mseg)
    return o.reshape(batch * seq, width), lse.reshape(batch * seq, width)


def _win_kernel(q_ref, k_ref, v_ref, pq_ref, pk_ref, gq_ref, gk_ref, sink_ref, mseg_ref, msegk_ref, o_ref, kn_scr,
                *, n, bq, kw, slopes):
    i = pl.program_id(1)

    @pl.when(i == 0)
    def _():
        k = k_ref[0].astype(F32)
        kn_scr[...] = (k * lax.rsqrt(_seg_mean_sq(k, msegk_ref[...]) + EPS) * gk_ref[...]).astype(BF16)

    ws = _window_start(i, bq, WIN_RADIUS, n, kw)
    kwin = kn_scr[pl.ds(ws, kw), :]
    vwin = v_ref[0, pl.ds(ws, kw), :]
    dist = _abs_dist(pq_ref[0], pk_ref[0])
    valid = _band_valid(i, bq, ws, kw, WIN_RADIUS)
    half_of_lane = lax.broadcasted_iota(I32, (1, 2 * HEAD_DIM), 1) >> 6
    rep = WIN_HEADS // WIN_KV_HEADS
    mseg = mseg_ref[...]
    for c in range(WIN_HEADS * HEAD_DIM // 256):
        q = q_ref[0, :, c * 256:(c + 1) * 256].astype(F32)
        qn = q * lax.rsqrt(_seg_mean_sq(q, mseg) + EPS) * gq_ref[...]
        for pp in range(2):
            pair = 2 * c + pp
            qp = qn[:, pp * 128:(pp + 1) * 128]
            out = jnp.zeros((bq, 2 * HEAD_DIM), F32)
            for half in range(2):
                h = 2 * pair + half
                g = h // rep
                hm = half_of_lane == half
                qm = jnp.where(hm, qp, 0.0)
                if half != g:
                    qm = pltpu.roll(qm, HEAD_DIM, 1)
                s = _dot_nt(qm.astype(BF16), kwin) - slopes[h] * dist
                s = jnp.where(valid, s, NEG_INF)
                sink = sink_ref[h:h + 1, 0:1]
                m = jnp.maximum(jnp.max(s, axis=-1, keepdims=True), sink)
                e = jnp.exp(s - m)
                den = jnp.sum(e, axis=-1, keepdims=True) + jnp.exp(sink - m)
                oh = _dot(e.astype(BF16), vwin) / den
                if half != g:
                    oh = pltpu.roll(oh, HEAD_DIM, 1)
                out = jnp.where(hm, oh, out)
            o_ref[0, :, pair * 128:(pair + 1) * 128] = out.astype(o_ref.dtype)


def _window_mixer(cols, pos, gq_row, gk_row, sink_rows, mseg256, mseg128, batch, seq):
    n = seq
    bq = min(128, n)
    kw = min(3 * bq, n)
    nblk = n // bq
    qw = WIN_HEADS * HEAD_DIM
    kvw = WIN_KV_HEADS * HEAD_DIM
    cols_v = cols.reshape(batch, seq, N_COLS)
    pq = pos.reshape(batch, seq, 1)
    starts = [min(max(i * bq - WIN_RADIUS, 0), n - kw) for i in range(nblk)]
    pk = jnp.stack([pos[:, s:s + kw] for s in starts], axis=1).reshape(batch * nblk, 1, kw)
    slopes = tuple(float(2.0 ** (-8.0 * (h + 1) / WIN_HEADS)) for h in range(WIN_HEADS))
    kern = functools.partial(_win_kernel, n=n, bq=bq, kw=kw, slopes=slopes)
    o = pl.pallas_call(
        kern,
        out_shape=jax.ShapeDtypeStruct((batch, seq, qw), BF16),
        grid=(batch, nblk),
        in_specs=[
            pl.BlockSpec((1, bq, qw), lambda b, i: (b, i, OFF_WQ // qw)),
            pl.BlockSpec((1, n, kvw), lambda b, i: (b, 0, OFF_WK // kvw)),
            pl.BlockSpec((1, n, kvw), lambda b, i: (b, 0, OFF_WV // kvw)),
            pl.BlockSpec((1, bq, 1), lambda b, i: (b, i, 0)),
            pl.BlockSpec((1, 1, kw), lambda b, i: (b * nblk + i, 0, 0)),
            pl.BlockSpec((1, 256), lambda b, i: (0, 0)),
            pl.BlockSpec((1, kvw), lambda b, i: (0, 0)),
            pl.BlockSpec((WIN_HEADS, LANES), lambda b, i: (0, 0)),
            pl.BlockSpec((256, 256), lambda b, i: (0, 0)),
            pl.BlockSpec((kvw, kvw), lambda b, i: (0, 0)),
        ],
        out_specs=pl.BlockSpec((1, bq, qw), lambda b, i: (b, i, 0)),
        scratch_shapes=[pltpu.VMEM((n, kvw), BF16)],
        compiler_params=_cparams(("parallel", "arbitrary"), VMEM_LIMIT),
        name="window_mixer",
    )(cols_v, cols_v, cols_v, pq, pk, gq_row, gk_row, sink_rows, mseg256, mseg128)
    return o.reshape(batch * seq, qw)


def _merge_kernel(gm_ref, gd_ref, gw_ref, om_ref, od0_ref, od1_ref, od2_ref, l0_ref, l1_ref, l2_ref, ow_ref,
                  x_ref, mod_ref, wm_ref, wd_ref, ww_ref, wo_ref, o_ref):
    l0, l1, l2 = l0_ref[...], l1_ref[...], l2_ref[...]
    m = jnp.maximum(jnp.maximum(l0, l1), l2)
    e0, e1, e2 = jnp.exp(l0 - m), jnp.exp(l1 - m), jnp.exp(l2 - m)
    od = (e0 * od0_ref[...] + e1 * od1_ref[...] + e2 * od2_ref[...]) / (e0 + e1 + e2)
    y = jax.nn.sigmoid(gm_ref[...].astype(F32)) * _dot(om_ref[...], wm_ref[...])
    y += jax.nn.sigmoid(gd_ref[...].astype(F32)) * _dot(od.astype(BF16), wd_ref[...])
    y += jax.nn.sigmoid(gw_ref[...].astype(F32)) * _dot(ow_ref[...], ww_ref[...])
    z = _dot(y.astype(BF16), wo_ref[...])
    o_ref[...] = x_ref[...] + mod_ref[0][2:3] * z


def _merge(cols, o_mla, o_dil, lse_dil, o_win, x2d, mod_l, wm, wd, ww, wo, seq):
    t, d = x2d.shape
    tm = _tile(seq, 512)
    per_b = seq // tm
    row = lambda w, j=0: pl.BlockSpec((tm, w), lambda i: (i, j))
    full = lambda a: pl.BlockSpec(a.shape, lambda i: (0, 0))
    dw = DIL_HPG * HEAD_DIM
    return pl.pallas_call(
        _merge_kernel,
        out_shape=jax.ShapeDtypeStruct((t, d), F32),
        grid=(t // tm,),
        in_specs=[row(d, 0), row(d, 1), row(d, 2), row(o_mla.shape[1]),
                  row(dw), row(dw), row(dw), row(dw), row(dw), row(dw), row(o_win.shape[1]),
                  row(d), pl.BlockSpec((1, 6, d), lambda i: (i // per_b, 0, 0)),
                  full(wm), full(wd), full(ww), full(wo)],
        out_specs=row(d),
        compiler_params=_cparams(("parallel",), VMEM_LIMIT),
        name="merge_out_proj",
    )(cols, cols, cols, o_mla, o_dil[0], o_dil[1], o_dil[2], lse_dil[0], lse_dil[1], lse_dil[2], o_win,
      x2d, mod_l, wm, wd, ww, wo)


def _router_kernel(x_ref, mod_ref, g_ref, wr_ref, br_ref, h_ref, ids_ref, gate_ref, cnt_ref):
    m = mod_ref[0]
    h = _rms(x_ref[...], g_ref[...]) * (1.0 + m[4:5]) + m[3:4]
    h_ref[...] = h
    tm = h.shape[0]
    logits = jnp.dot(h, wr_ref[...], precision=lax.Precision.HIGHEST, preferred_element_type=F32) + br_ref[...]
    lane = lax.broadcasted_iota(I32, (tm, LANES), 1)
    lane_f = lane.astype(F32)

    def first_argmax(vals, mx):
        return jnp.min(jnp.where(vals == mx, lane_f, float(LANES)), axis=-1, keepdims=True).astype(I32)

    lg = jnp.where(lane < N_EXPERT_GROUPS, logits, NEG_INF)
    mg = jnp.max(lg, axis=-1, keepdims=True)
    g_w = 1.0 / jnp.sum(jnp.exp(lg - mg), axis=-1, keepdims=True)
    g_idx = first_argmax(lg, mg)
    eid = lane - N_EXPERT_GROUPS
    in_grp = (eid >= 0) & (eid < N_EXPERTS) & ((eid >> 3) == g_idx)
    le = jnp.where(in_grp, logits, NEG_INF)
    m1 = jnp.max(le, axis=-1, keepdims=True)
    i1 = first_argmax(le, m1)
    le2 = jnp.where(lane == i1, NEG_INF, le)
    m2 = jnp.max(le2, axis=-1, keepdims=True)
    i2 = first_argmax(le2, m2)
    r = jnp.exp(m2 - m1)
    gate1 = g_w / (1.0 + r)
    gate2 = g_w * r / (1.0 + r)
    e1 = i1 - N_EXPERT_GROUPS
    e2 = i2 - N_EXPERT_GROUPS
    hit1 = lane == e1
    hit2 = lane == e2
    onehot = jnp.where(hit1 | hit2, 1.0, 0.0)
    rows = lax.broadcasted_iota(I32, (tm, tm), 0)
    cols = lax.broadcasted_iota(I32, (tm, tm), 1)
    before = jnp.where(rows > cols, 1.0, 0.0).astype(BF16)
    rank = _dot(before, onehot.astype(BF16))
    r1 = jnp.sum(jnp.where(hit1, rank, 0.0), axis=-1, keepdims=True).astype(I32)
    r2 = jnp.sum(jnp.where(hit2, rank, 0.0), axis=-1, keepdims=True).astype(I32)
    ids_ref[...] = jnp.where(lane == 0, e1, jnp.where(lane == 1, e2, jnp.where(lane == 2, r1,
                                                                               jnp.where(lane == 3, r2, 0))))
    gate_ref[...] = jnp.where(lane == 0, gate1, jnp.where(lane == 1, gate2, 0.0))
    cnt_ref[0] = jnp.sum(onehot, axis=0, keepdims=True).astype(I32)


def _router(x2d, mod_l, g_norm, w_router, b_router, seq):
    t, d = x2d.shape
    tm = _tile(seq, 512)
    per_b = seq // tm
    nt = t // tm
    row = lambda w, dt=None: pl.BlockSpec((tm, w), lambda i: (i, 0))
    return pl.pallas_call(
        _router_kernel,
        out_shape=(jax.ShapeDtypeStruct((t, d), F32), jax.ShapeDtypeStruct((t, LANES), I32),
                   jax.ShapeDtypeStruct((t, LANES), F32), jax.ShapeDtypeStruct((nt, 1, LANES), I32)),
        grid=(nt,),
        in_specs=[row(d), pl.BlockSpec((1, 6, d), lambda i: (i // per_b, 0, 0)),
                  pl.BlockSpec((1, d), lambda i: (0, 0)), pl.BlockSpec((d, LANES), lambda i: (0, 0)),
                  pl.BlockSpec((1, LANES), lambda i: (0, 0))],
        out_specs=(row(d), row(LANES), row(LANES), pl.BlockSpec((1, 1, LANES), lambda i: (i, 0, 0))),
        compiler_params=_cparams(("parallel",), VMEM_LIMIT),
        name="moe_router",
    )(x2d, mod_l, g_norm, w_router, b_router)


def _dest_kernel(ids_ref, base_ref, o_ref):
    ids = ids_ref[...]
    lane = lax.broadcasted_iota(I32, ids.shape, 1)
    base = base_ref[0].astype(F32)

    def pick(col):
        e = ids[:, col:col + 1]
        return jnp.sum(jnp.where(lane == e, base, 0.0), axis=-1, keepdims=True).astype(I32)

    d1 = ids[:, 2:3] + pick(0)
    d2 = ids[:, 3:4] + pick(1)
    o_ref[...] = jnp.where(lane == 0, d1, jnp.where(lane == 1, d2, 0))


def _dest_rows(ids, base, tm):
    t = ids.shape[0]
    return pl.pallas_call(
        _dest_kernel,
        out_shape=jax.ShapeDtypeStruct((t, LANES), I32),
        grid=(t // tm,),
        in_specs=[pl.BlockSpec((tm, LANES), lambda i: (i, 0)), pl.BlockSpec((1, 1, LANES), lambda i: (i, 0, 0))],
        out_specs=pl.BlockSpec((tm, LANES), lambda i: (i, 0)),
        compiler_params=_cparams(("parallel",)),
        name="moe_dest",
    )(ids, base)


def _row_copy(src, s, dst, d, sem):
    return pltpu.make_async_copy(src.at[pl.ds(s, 1)], dst.at[pl.ds(d, 1)], sem)


def _scatter_kernel(dest_ref, h_ref, rows_in_ref, rows_ref, idx_smem, idx_sem, sem, *, tm):
    del rows_in_ref
    i = pl.program_id(0)
    cp = pltpu.make_async_copy(dest_ref.at[i], idx_smem, idx_sem)
    cp.start()
    cp.wait()

    def issue(t, carry):
        tok = i * tm + t
        _row_copy(h_ref, tok, rows_ref, idx_smem[2 * t], sem).start()
        _row_copy(h_ref, tok, rows_ref, idx_smem[2 * t + 1], sem).start()
        return carry

    lax.fori_loop(0, tm, issue, 0)

    def drain(t, carry):
        _row_copy(h_ref, 0, rows_ref, 0, sem).wait()
        _row_copy(h_ref, 0, rows_ref, 0, sem).wait()
        return carry

    lax.fori_loop(0, tm, drain, 0)


def _scatter_rows(dest_tiles, h2d, n_rows, tm):
    t, d = h2d.shape
    nt = t // tm
    zeros = jnp.zeros((n_rows, d), h2d.dtype)
    return pl.pallas_call(
        functools.partial(_scatter_kernel, tm=tm),
        out_shape=jax.ShapeDtypeStruct((n_rows, d), h2d.dtype),
        grid=(nt,),
        in_specs=[pl.BlockSpec(memory_space=pl.ANY), pl.BlockSpec(memory_space=pl.ANY),
                  pl.BlockSpec(memory_space=pl.ANY)],
        out_specs=pl.BlockSpec(memory_space=pl.ANY),
        scratch_shapes=[pltpu.SMEM((2 * tm,), I32), pltpu.SemaphoreType.DMA, pltpu.SemaphoreType.DMA],
        input_output_aliases={2: 0},
        compiler_params=_cparams(("arbitrary",)),
        name="moe_scatter",
    )(dest_tiles, h2d, zeros)


def _ffn_kernel(be_ref, nu_ref, rows_ref, w1_ref, w3_ref, w2_ref, y_ref, w13_scr, w2_scr):
    j = pl.program_id(0)
    used = j < nu_ref[0]
    new_expert = (j == 0) | (be_ref[j] != be_ref[jnp.maximum(j - 1, 0)])

    @pl.when(used & new_expert)
    def _():
        w13_scr[:, :D_EXPERT] = w1_ref[0].astype(BF16)
        w13_scr[:, D_EXPERT:] = w3_ref[0].astype(BF16)
        w2_scr[...] = w2_ref[0].astype(BF16)

    @pl.when(used)
    def _():
        h = _dot(rows_ref[...].astype(BF16), w13_scr[...])
        a = h[:, :D_EXPERT]
        act = a * jax.nn.sigmoid(a) * h[:, D_EXPERT:]
        y_ref[...] = _dot(act.astype(BF16), w2_scr[...])

    @pl.when(jnp.logical_not(used))
    def _():
        y_ref[...] = jnp.zeros_like(y_ref)


def _grouped_ffn(block_expert, n_used, rows, w1, w3, w2):
    n_rows, d = rows.shape
    nb = n_rows // MOE_BM
    grid_spec = pltpu.PrefetchScalarGridSpec(
        num_scalar_prefetch=2,
        grid=(nb,),
        in_specs=[
            pl.BlockSpec((MOE_BM, d), lambda j, be, nu: (j, 0)),
            pl.BlockSpec((1, d, D_EXPERT), lambda j, be, nu: (be[j], 0, 0)),
            pl.BlockSpec((1, d, D_EXPERT), lambda j, be, nu: (be[j], 0, 0)),
            pl.BlockSpec((1, D_EXPERT, d), lambda j, be, nu: (be[j], 0, 0)),
        ],
        out_specs=pl.BlockSpec((MOE_BM, d), lambda j, be, nu: (j, 0)),
        scratch_shapes=[pltpu.VMEM((d, 2 * D_EXPERT), BF16), pltpu.VMEM((D_EXPERT, d), BF16)],
    )
    return pl.pallas_call(
        _ffn_kernel,
        out_shape=jax.ShapeDtypeStruct((n_rows, d), F32),
        grid_spec=grid_spec,
        compiler_params=_cparams(("arbitrary",), VMEM_LIMIT),
        name="moe_grouped_ffn",
    )(block_expert, n_used, rows, w1, w3, w2)


def _combine_kernel(dest_ref, y_ref, gate_ref, x_ref, mod_ref, o_ref, idx_smem, ybuf, idx_sem, sem, *, tm):
    i = pl.program_id(0)
    cp = pltpu.make_async_copy(dest_ref.at[i], idx_smem, idx_sem)
    cp.start()
    cp.wait()

    def issue(t, carry):
        _row_copy(y_ref, idx_smem[2 * t], ybuf.at[0], t, sem).start()
        _row_copy(y_ref, idx_smem[2 * t + 1], ybuf.at[1], t, sem).start()
        return carry

    lax.fori_loop(0, tm, issue, 0)

    def drain(t, carry):
        _row_copy(y_ref, 0, ybuf.at[0], 0, sem).wait()
        _row_copy(y_ref, 0, ybuf.at[1], 0, sem).wait()
        return carry

    lax.fori_loop(0, tm, drain, 0)
    g = gate_ref[...]
    moe = g[:, 0:1] * ybuf[0] + g[:, 1:2] * ybuf[1]
    o_ref[...] = x_ref[...] + mod_ref[0][5:6] * moe


def _combine(dest_tiles, y, gates, x2d, mod_l, seq, tm):
    t, d = x2d.shape
    per_b = seq // tm
    return pl.pallas_call(
        functools.partial(_combine_kernel, tm=tm),
        out_shape=jax.ShapeDtypeStruct((t, d), F32),
        grid=(t // tm,),
        in_specs=[pl.BlockSpec(memory_space=pl.ANY), pl.BlockSpec(memory_space=pl.ANY),
                  pl.BlockSpec((tm, LANES), lambda i: (i, 0)), pl.BlockSpec((tm, d), lambda i: (i, 0)),
                  pl.BlockSpec((1, 6, d), lambda i: (i // per_b, 0, 0))],
        out_specs=pl.BlockSpec((tm, d), lambda i: (i, 0)),
        scratch_shapes=[pltpu.SMEM((2 * tm,), I32), pltpu.VMEM((2, tm, d), F32),
                        pltpu.SemaphoreType.DMA, pltpu.SemaphoreType.DMA],
        compiler_params=_cparams(("arbitrary",), VMEM_LIMIT),
        name="moe_combine",
    )(dest_tiles, y, gates, x2d, mod_l)


def _moe(x2d, mod_l, g_norm2, w_gr, b_gr, w_er, b_er, w1, w3, w2, seq):
    t, d = x2d.shape
    pad = LANES - N_EXPERT_GROUPS - N_EXPERTS
    w_router = jnp.concatenate([w_gr, w_er, jnp.zeros((d, pad), F32)], axis=1)
    b_router = jnp.concatenate([b_gr, b_er, jnp.zeros((pad,), F32)]).reshape(1, LANES)
    tm = _tile(seq, 512)
    nt = t // tm
    h2, ids, gates, cnt = _router(x2d, mod_l, g_norm2, w_router, b_router, seq)

    cnt = cnt[:, 0, :N_EXPERTS]
    total = jnp.sum(cnt, axis=0)
    padded = (total + MOE_BM - 1) // MOE_BM * MOE_BM
    ends = jnp.cumsum(padded)
    tile_off = jnp.cumsum(cnt, axis=0) - cnt
    base = (ends - padded)[None, :] + tile_off
    base = jnp.pad(base, ((0, 0), (0, LANES - N_EXPERTS))).reshape(nt, 1, LANES)
    nb = (2 * t) // MOE_BM + N_EXPERTS
    block_start = jnp.arange(nb, dtype=I32) * MOE_BM
    block_expert = jnp.minimum(jnp.sum(block_start[:, None] >= ends[None, :], axis=1), N_EXPERTS - 1).astype(I32)
    n_used = (ends[-1] // MOE_BM).astype(I32).reshape(1)

    dest = _dest_rows(ids, base.astype(I32), tm)
    dest_tiles = dest[:, :2].reshape(nt, 2 * tm)
    rows = _scatter_rows(dest_tiles, h2, nb * MOE_BM, tm)
    y = _grouped_ffn(block_expert, n_used, rows, w1, w3, w2)
    return _combine(dest_tiles, y, gates, x2d, mod_l, seq, tm)


def _seg_matrix(width, segments):
    idx = jnp.arange(width)
    m = jnp.zeros((width, width), F32)
    for start, length in segments:
        inside = (idx >= start) & (idx < start + length)
        m = m + jnp.where(inside[:, None] & inside[None, :], 1.0 / length, 0.0)
    return m.astype(BF16)


def _layout_w_in(w_in):
    depth, d, _ = w_in.shape
    sizes = [MLA_Q_RANK, MLA_KV_RANK, MLA_ROPE, 3 * DIL_HEADS * HEAD_DIM,
             (WIN_HEADS + 2 * WIN_KV_HEADS) * HEAD_DIM, 3 * D_MODEL]
    bounds = [sum(sizes[:k + 1]) for k in range(len(sizes) - 1)]
    c_q, c_kv, k_rope, dil, win, gate = jnp.split(w_in, bounds, axis=-1)
    z = lambda w: jnp.zeros((depth, d, w), w_in.dtype)
    out = jnp.concatenate([gate, c_q, dil, c_kv, win, z(MLA_NOPE), k_rope,
                           z(N_COLS - OFF_KR - MLA_NOPE - MLA_ROPE)], axis=-1)
    assert out.shape[-1] == N_COLS
    return out.astype(BF16)


def _layout_mla(w_uq, w_ukv, g_q, g_k):
    qd = MLA_NOPE + MLA_ROPE
    wq = w_uq.reshape(MLA_Q_RANK, MLA_HEADS, qd)
    wq = jnp.pad(wq, ((0, 0), (0, 0), (0, SLOT - qd))).reshape(MLA_Q_RANK, MLA_HEADS * SLOT)
    wkv = w_ukv.reshape(MLA_KV_RANK, MLA_HEADS, MLA_NOPE + MLA_V)
    wk = jnp.pad(wkv[:, :, :MLA_NOPE], ((0, 0), (0, 0), (0, SLOT - MLA_NOPE))).reshape(MLA_KV_RANK, MLA_HEADS * SLOT)
    wv = wkv[:, :, MLA_NOPE:].reshape(MLA_KV_RANK, MLA_HEADS * MLA_V)
    scale = float(qd) ** -0.5
    gq_slot = (jnp.pad(g_q, (0, SLOT - qd)) * scale).reshape(1, SLOT)
    gkn_slot = jnp.pad(g_k[:MLA_NOPE], (0, SLOT - MLA_NOPE)).reshape(1, SLOT)
    gkr_slot = jnp.pad(g_k[MLA_NOPE:], (MLA_NOPE, SLOT - qd)).reshape(1, SLOT)
    return wq.astype(BF16), jnp.concatenate([wk, wv], axis=1).astype(BF16), gq_slot, gkn_slot, gkr_slot


def kernel(x, c, pos, w_ada, b_ada, g_norm1, w_in, g_cq, w_uq, g_ckv, w_ukv, g_q_mla, g_k_mla, g_q_dil, g_k_dil,
           g_q_win, g_k_win, sink_win, w_br_mla, w_br_dil, w_br_win, w_out, g_norm2, w_gr, b_gr, w_er, b_er,
           w1, w3, w2):
    batch, seq, d = x.shape
    depth = w_ada.shape[0]
    t = batch * seq
    half = MLA_ROPE // 2
    inv_freq = ROPE_THETA ** (-jnp.arange(half, dtype=F32) / half)
    invf_slot = jnp.concatenate([jnp.zeros((MLA_NOPE,), F32), inv_freq, inv_freq,
                                 jnp.zeros((SLOT - MLA_NOPE - MLA_ROPE,), F32)]).reshape(1, SLOT)
    cos_t, sin_t = _rope_tables(pos.reshape(t, 1), invf_slot)
    mod = _modulation(c, w_ada, b_ada)
    w_in_k = _layout_w_in(w_in)
    mseg_slot = _seg_matrix(SLOT, ((0, MLA_NOPE), (MLA_NOPE, MLA_ROPE)))
    mseg256 = _seg_matrix(256, tuple((k * HEAD_DIM, HEAD_DIM) for k in range(4)))
    mseg128 = _seg_matrix(128, tuple((k * HEAD_DIM, HEAD_DIM) for k in range(2)))
    head_scale = float(HEAD_DIM) ** -0.5

    x2d = x.reshape(t, d)
    for l in range(depth):
        mod_l = mod[l].reshape(batch, 6, d)
        cols = _in_projection(x2d, mod_l, g_norm1[l].reshape(1, d), w_in_k[l], seq)

        wuq, wukv, gq_slot, gkn_slot, gkr_slot = _layout_mla(w_uq[l], w_ukv[l], g_q_mla[l], g_k_mla[l])
        qm, km, vm = _mla_prep(cols, cos_t, sin_t, g_cq[l].reshape(1, -1), g_ckv[l].reshape(1, -1),
                               wuq, wukv, gq_slot, gkn_slot, gkr_slot, mseg_slot)
        o_mla = _mla_attention(qm, km, vm, batch, seq)

        gq_dil = (jnp.tile(g_q_dil[l], DIL_HPG) * head_scale).reshape(1, -1)
        gk_dil = jnp.tile(g_k_dil[l], DIL_HPG).reshape(1, -1)
        o_dil, lse_dil = [], []
        for gi in range(DIL_GROUPS):
            o_g, lse_g = _dilated_group(cols, pos, gq_dil, gk_dil, mseg256, gi, batch, seq)
            o_dil.append(o_g)
            lse_dil.append(lse_g)

        gq_win = (jnp.tile(g_q_win[l], 4) * head_scale).reshape(1, -1)
        gk_win = jnp.tile(g_k_win[l], WIN_KV_HEADS).reshape(1, -1)
        sink_rows = jnp.broadcast_to(sink_win[l].astype(F32)[:, None], (WIN_HEADS, LANES))
        o_win = _window_mixer(cols, pos, gq_win, gk_win, sink_rows, mseg256, mseg128, batch, seq)

        x2d = _merge(cols, o_mla, o_dil, lse_dil, o_win, x2d, mod_l, w_br_mla[l].astype(BF16),
                     w_br_dil[l].astype(BF16), w_br_win[l].astype(BF16), w_out[l].astype(BF16), seq)
        x2d = _moe(x2d, mod_l, g_norm2[l].reshape(1, d), w_gr[l], b_gr[l], w_er[l], b_er[l],
                   w1[l], w3[l], w2[l], seq)
    return x2d.reshape(batch, seq, d)
```

```python
import functools

import jax
import jax.numpy as jnp
from jax import lax
from jax.experimental import pallas as pl
from jax.experimental.pallas import tpu as pltpu

F32 = jnp.float32
BF16 = jnp.bfloat16
I32 = jnp.int32

D_MODEL = 1024
HEAD_DIM = 64
NEG_INF = -1e30
EPS = 1e-6
MLA_HEADS = 8
MLA_Q_RANK = 512
MLA_KV_RANK = 256
MLA_NOPE = 64
MLA_ROPE = 32
MLA_V = 64
ROPE_THETA = 10000.0
DIL_PATTERNS = ((128, 1), (512, 4), (2048, 16))
DIL_GROUPS = 3
DIL_HPG = 4
DIL_HEADS = DIL_GROUPS * DIL_HPG
DIL_RADIUS = 64
WIN_HEADS = 8
WIN_KV_HEADS = 2
WIN_RADIUS = 128
N_EXPERT_GROUPS = 4
EXPERTS_PER_GROUP = 8
N_EXPERTS = N_EXPERT_GROUPS * EXPERTS_PER_GROUP
D_EXPERT = 384

LANES = 128
SLOT = 128
VMEM_LIMIT = 48 * 1024 * 1024

OFF_GATE = 0
OFF_CQ = 3072
OFF_WQ = 3584
OFF_CKV = 4096
OFF_WK = 4352
OFF_WV = 4480
OFF_KR = 4608
N_COLS = 4864
N_DIL_COLS = 3 * DIL_HEADS * HEAD_DIM

MOE_BM = 256


def _cparams(sem, vmem=None):
    return pltpu.CompilerParams(dimension_semantics=sem, vmem_limit_bytes=vmem)


def _tile(n, pref):
    t = min(n, pref)
    assert n % t == 0, (n, pref)
    return t


def _dot(a, b):
    return jnp.dot(a, b, preferred_element_type=F32)


def _dot_nt(a, b):
    return lax.dot_general(a, b, (((1,), (1,)), ((), ())), preferred_element_type=F32)


def _seg_mean_sq(x, mseg):
    x2 = x * x
    hi = x2.astype(BF16)
    lo = (x2 - hi.astype(F32)).astype(BF16)
    return _dot(hi, mseg) + _dot(lo, mseg)


def _seg_norm(x, mseg, gain):
    return x * lax.rsqrt(_seg_mean_sq(x, mseg) + EPS) * gain


def _rms(x, gain):
    ms = jnp.mean(x * x, axis=-1, keepdims=True)
    return x * lax.rsqrt(ms + EPS) * gain


def _mod_kernel(c_ref, w_ref, b_ref, o_ref):
    c = c_ref[...]
    cond = (c * jax.nn.sigmoid(c)).astype(BF16)
    o_ref[0] = _dot(cond, w_ref[0].astype(BF16)) + b_ref[0]


def _modulation(c, w_ada, b_ada):
    depth, d, n = w_ada.shape
    b = c.shape[0]
    tn = _tile(n, 1536)
    return pl.pallas_call(
        _mod_kernel,
        out_shape=jax.ShapeDtypeStruct((depth, b, n), F32),
        grid=(depth, n // tn),
        in_specs=[
            pl.BlockSpec((b, d), lambda l, j: (0, 0)),
            pl.BlockSpec((1, d, tn), lambda l, j: (l, 0, j)),
            pl.BlockSpec((1, 1, tn), lambda l, j: (l, 0, j)),
        ],
        out_specs=pl.BlockSpec((1, b, tn), lambda l, j: (l, 0, j)),
        compiler_params=_cparams(("parallel", "parallel")),
        name="adaln_mod",
    )(c, w_ada, b_ada.reshape(depth, 1, n))


def _rope_table_kernel(pos_ref, invf_ref, cos_ref, sin_ref):
    ang = pos_ref[...].astype(F32) * invf_ref[...]
    cos_ref[...] = jnp.cos(ang)
    sin_ref[...] = jnp.sin(ang)


def _rope_tables(pos_col, invf_slot):
    t = pos_col.shape[0]
    tm = _tile(t, 1024)
    return pl.pallas_call(
        _rope_table_kernel,
        out_shape=(jax.ShapeDtypeStruct((t, SLOT), F32), jax.ShapeDtypeStruct((t, SLOT), F32)),
        grid=(t // tm,),
        in_specs=[pl.BlockSpec((tm, 1), lambda i: (i, 0)), pl.BlockSpec((1, SLOT), lambda i: (0, 0))],
        out_specs=(pl.BlockSpec((tm, SLOT), lambda i: (i, 0)), pl.BlockSpec((tm, SLOT), lambda i: (i, 0))),
        compiler_params=_cparams(("parallel",)),
        name="rope_tables",
    )(pos_col, invf_slot)


def _inproj_kernel(x_ref, mod_ref, g_ref, w_ref, o_ref, h_scr):
    @pl.when(pl.program_id(1) == 0)
    def _():
        m = mod_ref[0]
        h = _rms(x_ref[...], g_ref[...]) * (1.0 + m[1:2]) + m[0:1]
        h_scr[...] = h.astype(BF16)

    o_ref[...] = _dot(h_scr[...], w_ref[...]).astype(o_ref.dtype)


def _in_projection(x2d, mod_l, g_norm, w_in_l, seq, out_dtype, tn, name):
    t, d = x2d.shape
    nc = w_in_l.shape[1]
    tm = _tile(seq, 1024)
    assert nc % tn == 0
    per_b = seq // tm
    return pl.pallas_call(
        _inproj_kernel,
        out_shape=jax.ShapeDtypeStruct((t, nc), out_dtype),
        grid=(t // tm, nc // tn),
        in_specs=[
            pl.BlockSpec((tm, d), lambda i, j: (i, 0)),
            pl.BlockSpec((1, 6, d), lambda i, j: (i // per_b, 0, 0)),
            pl.BlockSpec((1, d), lambda i, j: (0, 0)),
            pl.BlockSpec((d, tn), lambda i, j: (0, j)),
        ],
        out_specs=pl.BlockSpec((tm, tn), lambda i, j: (i, j)),
        scratch_shapes=[pltpu.VMEM((tm, d), BF16)],
        compiler_params=_cparams(("parallel", "arbitrary"), VMEM_LIMIT),
        name=name,
    )(x2d, mod_l, g_norm, w_in_l)


def _mla_prep_kernel(cq_ref, ckv_ref, kr_ref, cos_ref, sin_ref, gcq_ref, gckv_ref, wuq_ref, wukv_ref,
                     gq_ref, gkn_ref, gkr_ref, mseg_ref, q_out, k_out, v_out):
    cos = cos_ref[...]
    sin = sin_ref[...]
    lane = lax.broadcasted_iota(I32, (1, SLOT), 1)
    s_neg = jnp.where((lane >= 64) & (lane < 80), -sin, 0.0)
    s_pos = jnp.where((lane >= 80) & (lane < 96), sin, 0.0)
    mseg = mseg_ref[...]

    def rope(xn):
        return xn * cos + pltpu.roll(xn, SLOT - 16, 1) * s_neg + pltpu.roll(xn, 16, 1) * s_pos

    cqn = _rms(cq_ref[...].astype(F32), gcq_ref[...]).astype(BF16)
    q = _dot(cqn, wuq_ref[...])
    for h in range(MLA_HEADS):
        sl = slice(h * SLOT, (h + 1) * SLOT)
        q_out[:, sl] = rope(_seg_norm(q[:, sl], mseg, gq_ref[...])).astype(q_out.dtype)

    ckvn = _rms(ckv_ref[...].astype(F32), gckv_ref[...]).astype(BF16)
    kv = _dot(ckvn, wukv_ref[...])
    kr = rope(_seg_norm(kr_ref[...].astype(F32), mseg, gkr_ref[...]))
    for h in range(MLA_HEADS):
        sl = slice(h * SLOT, (h + 1) * SLOT)
        k_out[:, sl] = (_seg_norm(kv[:, sl], mseg, gkn_ref[...]) + kr).astype(k_out.dtype)
    v_out[...] = kv[:, MLA_HEADS * SLOT:].astype(v_out.dtype)


def _mla_prep(cols, cos_t, sin_t, gcq, gckv, wuq, wukv, gq_slot, gkn_slot, gkr_slot, mseg):
    t = cols.shape[0]
    tm = _tile(t, 512)
    hs = MLA_HEADS * SLOT
    full = lambda shape: pl.BlockSpec(shape, lambda i: (0,) * len(shape))
    return pl.pallas_call(
        _mla_prep_kernel,
        out_shape=(jax.ShapeDtypeStruct((t, hs), BF16), jax.ShapeDtypeStruct((t, hs), BF16),
                   jax.ShapeDtypeStruct((t, MLA_HEADS * MLA_V), BF16)),
        grid=(t // tm,),
        in_specs=[
            pl.BlockSpec((tm, MLA_Q_RANK), lambda i: (i, OFF_CQ // MLA_Q_RANK)),
            pl.BlockSpec((tm, MLA_KV_RANK), lambda i: (i, OFF_CKV // MLA_KV_RANK)),
            pl.BlockSpec((tm, SLOT), lambda i: (i, OFF_KR // SLOT)),
            pl.BlockSpec((tm, SLOT), lambda i: (i, 0)),
            pl.BlockSpec((tm, SLOT), lambda i: (i, 0)),
            full((1, MLA_Q_RANK)), full((1, MLA_KV_RANK)),
            full((MLA_Q_RANK, hs)), full((MLA_KV_RANK, hs + MLA_HEADS * MLA_V)),
            full((1, SLOT)), full((1, SLOT)), full((1, SLOT)), full((SLOT, SLOT)),
        ],
        out_specs=(pl.BlockSpec((tm, hs), lambda i: (i, 0)), pl.BlockSpec((tm, hs), lambda i: (i, 0)),
                   pl.BlockSpec((tm, MLA_HEADS * MLA_V), lambda i: (i, 0))),
        compiler_params=_cparams(("parallel",), VMEM_LIMIT),
        name="mla_prep",
    )(cols, cols, cols, cos_t, sin_t, gcq, gckv, wuq, wukv, gq_slot, gkn_slot, gkr_slot, mseg)


def _mla_attn_kernel(q_ref, k_ref, v_ref, o_ref):
    outs = []
    for a in range(2):
        sl = slice(a * SLOT, (a + 1) * SLOT)
        s = _dot_nt(q_ref[0, :, sl], k_ref[0, :, sl])
        m = jnp.max(s, axis=-1, keepdims=True)
        p = jnp.exp(s - m)
        l = jnp.sum(p, axis=-1, keepdims=True)
        outs.append(_dot(p.astype(BF16), v_ref[0]) / l)
    lane = lax.broadcasted_iota(I32, (1, 2 * MLA_V), 1)
    o_ref[0] = jnp.where(lane < MLA_V, outs[0], outs[1]).astype(o_ref.dtype)


def _mla_attention(qm, km, vm, batch, seq):
    hs = MLA_HEADS * SLOT
    q3 = qm.reshape(batch, seq, hs)
    k3 = km.reshape(batch, seq, hs)
    v3 = vm.reshape(batch, seq, MLA_HEADS * MLA_V)
    tq = _tile(seq, 256)
    out = pl.pallas_call(
        _mla_attn_kernel,
        out_shape=jax.ShapeDtypeStruct((batch, seq, MLA_HEADS * MLA_V), BF16),
        grid=(batch, MLA_HEADS // 2, seq // tq),
        in_specs=[
            pl.BlockSpec((1, tq, 2 * SLOT), lambda b, p, i: (b, i, p)),
            pl.BlockSpec((1, seq, 2 * SLOT), lambda b, p, i: (b, 0, p)),
            pl.BlockSpec((1, seq, 2 * MLA_V), lambda b, p, i: (b, 0, p)),
        ],
        out_specs=pl.BlockSpec((1, tq, 2 * MLA_V), lambda b, p, i: (b, i, p)),
        compiler_params=_cparams(("parallel", "parallel", "arbitrary"), VMEM_LIMIT),
        name="mla_attention",
    )(q3, k3, v3)
    return out.reshape(batch * seq, MLA_HEADS * MLA_V)


def _window_start(i, bq, radius, n, kw):
    ws = jnp.clip(i * bq - radius, 0, n - kw)
    return pl.multiple_of(ws, 16)


def _band_valid(i, bq, ws, kw, radius, heads):
    rows = lax.broadcasted_iota(I32, (heads * bq, kw), 0) & (bq - 1)
    cols = lax.broadcasted_iota(I32, (heads * bq, kw), 1)
    rel = rows - cols + (i * bq - ws)
    return jnp.maximum(rel, -rel) <= radius


def _abs_dist(pq, pk):
    d = pq - pk
    return jnp.maximum(d, -d).astype(F32)


def _window_positions(pos_sub, n, bq, kw, radius):
    nblk = n // bq
    starts = [min(max(i * bq - radius, 0), n - kw) for i in range(nblk)]
    pk = jnp.stack([pos_sub[:, s:s + kw] for s in starts], axis=1)
    return pk.reshape(pos_sub.shape[0] * nblk, 1, kw)


def _dil_kernel(q_ref, k_ref, v_ref, pq_ref, pk_ref, gq_ref, gk_ref, slope_ref, mseg_ref, o_ref, lse_ref,
                kn_scr, vn_scr, *, n, bq, kw, dil):
    half = pl.program_id(1)
    r = pl.program_id(2)
    i = pl.program_id(3)
    mseg = mseg_ref[...]

    def rows(start, size):
        return pl.ds(start, size) if dil == 1 else pl.ds(start, size, stride=dil)

    @pl.when(i == 0)
    def _():
        kn_scr[...] = _seg_norm(k_ref[0, rows(r, n), :], mseg, gk_ref[...]).astype(BF16)
        vn_scr[...] = v_ref[0, rows(r, n), :].astype(BF16)

    q_rows = rows(r + i * (bq * dil), bq)
    qn = _seg_norm(q_ref[0, q_rows, :], mseg, gq_ref[...])
    ws = _window_start(i, bq, DIL_RADIUS, n, kw)
    kwin = kn_scr[pl.ds(ws, kw), :]
    vwin = vn_scr[pl.ds(ws, kw), :]
    dist = _abs_dist(pq_ref[0], pk_ref[0])
    low = lax.broadcasted_iota(I32, (1, 2 * HEAD_DIM), 1) < HEAD_DIM
    qs = jnp.concatenate([jnp.where(low, qn, 0.0), jnp.where(low, 0.0, qn)], axis=0).astype(BF16)
    slope0 = slope_ref[pl.ds(2 * half, 1), 0:1]
    slope1 = slope_ref[pl.ds(2 * half + 1, 1), 0:1]
    bias = jnp.concatenate([slope0 * dist, slope1 * dist], axis=0)
    s = jnp.where(_band_valid(i, bq, ws, kw, DIL_RADIUS, 2), _dot_nt(qs, kwin) - bias, NEG_INF)
    m = jnp.max(s, axis=-1, keepdims=True)
    p = jnp.exp(s - m)
    l = jnp.sum(p, axis=-1, keepdims=True)
    o = _dot(p.astype(BF16), vwin) / l
    lse = m + jnp.log(l)
    o_ref[0, q_rows, :] = jnp.where(low, o[:bq], o[bq:])
    lse_ref[0, q_rows, :] = jnp.where(low, lse[:bq], lse[bq:])


def _dilated_group(dil_cols, pos, gq_row, gk_row, mseg, gi, batch, seq):
    _, dil = DIL_PATTERNS[gi]
    n = seq // dil
    bq = min(128, n)
    kw = min(bq + 2 * DIL_RADIUS, n)
    nblk = n // bq
    width = DIL_HPG * HEAD_DIM
    hw = 2 * HEAD_DIM
    cols3 = dil_cols.reshape(batch, seq, N_DIL_COLS)
    pos_sub = jnp.transpose(pos.reshape(batch, n, dil), (0, 2, 1)).reshape(batch * dil, n)
    pq = pos_sub.reshape(batch * dil, n, 1)
    pk = _window_positions(pos_sub, n, bq, kw, DIL_RADIUS)
    slopes = jnp.asarray([2.0 ** (-8.0 * (gi * DIL_HPG + hh + 1) / DIL_HEADS) for hh in range(DIL_HPG)], F32)
    slope_rows = jnp.broadcast_to(jnp.pad(slopes, (0, 8 - DIL_HPG))[:, None], (8, LANES))
    kern = functools.partial(_dil_kernel, n=n, bq=bq, kw=kw, dil=dil)
    col = lambda which: (lambda b, h, r, i: (b, 0, (which * DIL_GROUPS + gi) * 2 + h))
    small = lambda shape: pl.BlockSpec(shape, lambda b, h, r, i: (0,) * len(shape))
    o, lse = pl.pallas_call(
        kern,
        out_shape=(jax.ShapeDtypeStruct((batch, seq, width), F32), jax.ShapeDtypeStruct((batch, seq, width), F32)),
        grid=(batch, 2, dil, nblk),
        in_specs=[
            pl.BlockSpec((1, seq, hw), col(0)),
            pl.BlockSpec((1, seq, hw), col(1)),
            pl.BlockSpec((1, seq, hw), col(2)),
            pl.BlockSpec((1, bq, 1), lambda b, h, r, i: (b * dil + r, i, 0)),
            pl.BlockSpec((1, 1, kw), lambda b, h, r, i: ((b * dil + r) * nblk + i, 0, 0)),
            small((1, hw)), small((1, hw)), small((8, LANES)), small((hw, hw)),
        ],
        out_specs=(pl.BlockSpec((1, seq, hw), lambda b, h, r, i: (b, 0, h)),
                   pl.BlockSpec((1, seq, hw), lambda b, h, r, i: (b, 0, h))),
        scratch_shapes=[pltpu.VMEM((n, hw), BF16), pltpu.VMEM((n, hw), BF16)],
        compiler_params=_cparams(("parallel", "parallel", "arbitrary", "arbitrary"), VMEM_LIMIT),
        name=f"dilated_group{gi}",
    )(cols3, cols3, cols3, pq, pk, gq_row, gk_row, slope_rows, mseg)
    return o.reshape(batch * seq, width), lse.reshape(batch * seq, width)


def _win_kernel(q_ref, k_ref, v_ref, pq_ref, pk_ref, gq_ref, gk_ref, sink_ref, mseg_ref, o_ref, kn_scr,
                *, n, bq, kw, slopes):
    i = pl.program_id(1)
    mseg = mseg_ref[...]

    @pl.when(i == 0)
    def _():
        kn_scr[...] = _seg_norm(k_ref[0].astype(F32), mseg, gk_ref[...]).astype(BF16)

    ws = _window_start(i, bq, WIN_RADIUS, n, kw)
    kwin = kn_scr[pl.ds(ws, kw), :]
    vwin = v_ref[0, pl.ds(ws, kw), :]
    dist = _abs_dist(pq_ref[0], pk_ref[0])
    low = lax.broadcasted_iota(I32, (1, 2 * HEAD_DIM), 1) < HEAD_DIM
    rep = WIN_HEADS // WIN_KV_HEADS

    q_parts, bias_parts, sink_parts = [], [], []
    qn_pairs = [_seg_norm(q_ref[0, :, pair * 128:(pair + 1) * 128].astype(F32), mseg, gq_ref[...])
                for pair in range(WIN_HEADS // 2)]
    for h in range(WIN_HEADS):
        pair, upper = divmod(h, 2)
        qn = qn_pairs[pair]
        qm = jnp.where(low, 0.0, qn) if upper else jnp.where(low, qn, 0.0)
        if upper != h // rep:
            qm = pltpu.roll(qm, HEAD_DIM, 1)
        q_parts.append(qm.astype(BF16))
        bias_parts.append(slopes[h] * dist)
        sink_parts.append(jnp.broadcast_to(sink_ref[h:h + 1, 0:1], (bq, 1)))
    qs = jnp.concatenate(q_parts, axis=0)
    bias = jnp.concatenate(bias_parts, axis=0)
    sink = jnp.concatenate(sink_parts, axis=0)
    s = jnp.where(_band_valid(i, bq, ws, kw, WIN_RADIUS, WIN_HEADS), _dot_nt(qs, kwin) - bias, NEG_INF)
    m = jnp.maximum(jnp.max(s, axis=-1, keepdims=True), sink)
    e = jnp.exp(s - m)
    den = jnp.sum(e, axis=-1, keepdims=True) + jnp.exp(sink - m)
    o = _dot(e.astype(BF16), vwin) / den
    for pair in range(WIN_HEADS // 2):
        halves = []
        for upper in range(2):
            h = 2 * pair + upper
            oh = o[h * bq:(h + 1) * bq]
            if upper != h // rep:
                oh = pltpu.roll(oh, HEAD_DIM, 1)
            halves.append(oh)
        o_ref[0, :, pair * 128:(pair + 1) * 128] = jnp.where(low, halves[0], halves[1]).astype(o_ref.dtype)


def _window_mixer(cols, pos, gq_row, gk_row, sink_rows, mseg128, batch, seq):
    n = seq
    bq = min(128, n)
    kw = min(bq + 2 * WIN_RADIUS, n)
    nblk = n // bq
    qw = WIN_HEADS * HEAD_DIM
    kvw = WIN_KV_HEADS * HEAD_DIM
    cols_v = cols.reshape(batch, seq, N_COLS)
    pq = pos.reshape(batch, seq, 1)
    pk = _window_positions(pos, n, bq, kw, WIN_RADIUS)
    slopes = tuple(float(2.0 ** (-8.0 * (h + 1) / WIN_HEADS)) for h in range(WIN_HEADS))
    kern = functools.partial(_win_kernel, n=n, bq=bq, kw=kw, slopes=slopes)
    small = lambda shape: pl.BlockSpec(shape, lambda b, i: (0,) * len(shape))
    o = pl.pallas_call(
        kern,
        out_shape=jax.ShapeDtypeStruct((batch, seq, qw), BF16),
        grid=(batch, nblk),
        in_specs=[
            pl.BlockSpec((1, bq, qw), lambda b, i: (b, i, OFF_WQ // qw)),
            pl.BlockSpec((1, n, kvw), lambda b, i: (b, 0, OFF_WK // kvw)),
            pl.BlockSpec((1, n, kvw), lambda b, i: (b, 0, OFF_WV // kvw)),
            pl.BlockSpec((1, bq, 1), lambda b, i: (b, i, 0)),
            pl.BlockSpec((1, 1, kw), lambda b, i: (b * nblk + i, 0, 0)),
            small((1, kvw)), small((1, kvw)), small((WIN_HEADS, LANES)), small((kvw, kvw)),
        ],
        out_specs=pl.BlockSpec((1, bq, qw), lambda b, i: (b, i, 0)),
        scratch_shapes=[pltpu.VMEM((n, kvw), BF16)],
        compiler_params=_cparams(("parallel", "arbitrary"), VMEM_LIMIT),
        name="window_mixer",
    )(cols_v, cols_v, cols_v, pq, pk, gq_row, gk_row, sink_rows, mseg128)
    return o.reshape(batch * seq, qw)


def _merge_kernel(gm_ref, gd_ref, gw_ref, om_ref, od0_ref, od1_ref, od2_ref, l0_ref, l1_ref, l2_ref, ow_ref,
                  x_ref, mod_ref, wm_ref, wd_ref, ww_ref, wo_ref, o_ref):
    l0, l1, l2 = l0_ref[...], l1_ref[...], l2_ref[...]
    m = jnp.maximum(jnp.maximum(l0, l1), l2)
    e0, e1, e2 = jnp.exp(l0 - m), jnp.exp(l1 - m), jnp.exp(l2 - m)
    od = (e0 * od0_ref[...] + e1 * od1_ref[...] + e2 * od2_ref[...]) / (e0 + e1 + e2)
    y = jax.nn.sigmoid(gm_ref[...].astype(F32)) * _dot(om_ref[...], wm_ref[...])
    y += jax.nn.sigmoid(gd_ref[...].astype(F32)) * _dot(od.astype(BF16), wd_ref[...])
    y += jax.nn.sigmoid(gw_ref[...].astype(F32)) * _dot(ow_ref[...], ww_ref[...])
    z = _dot(y.astype(BF16), wo_ref[...])
    o_ref[...] = x_ref[...] + mod_ref[0][2:3] * z


def _merge(cols, o_mla, o_dil, lse_dil, o_win, x2d, mod_l, wm, wd, ww, wo, seq):
    t, d = x2d.shape
    tm = _tile(seq, 512)
    per_b = seq // tm
    row = lambda w, j=0: pl.BlockSpec((tm, w), lambda i: (i, j))
    full = lambda a: pl.BlockSpec(a.shape, lambda i: (0, 0))
    dw = DIL_HPG * HEAD_DIM
    return pl.pallas_call(
        _merge_kernel,
        out_shape=jax.ShapeDtypeStruct((t, d), F32),
        grid=(t // tm,),
        in_specs=[row(d, 0), row(d, 1), row(d, 2), row(o_mla.shape[1]),
                  row(dw), row(dw), row(dw), row(dw), row(dw), row(dw), row(o_win.shape[1]),
                  row(d), pl.BlockSpec((1, 6, d), lambda i: (i // per_b, 0, 0)),
                  full(wm), full(wd), full(ww), full(wo)],
        out_specs=row(d),
        compiler_params=_cparams(("parallel",), VMEM_LIMIT),
        name="merge_out_proj",
    )(cols, cols, cols, o_mla, o_dil[0], o_dil[1], o_dil[2], lse_dil[0], lse_dil[1], lse_dil[2], o_win,
      x2d, mod_l, wm, wd, ww, wo)


def _router_kernel(x_ref, mod_ref, g_ref, wr_ref, br_ref, h_ref, ids_ref, gate_ref, cnt_ref):
    m = mod_ref[0]
    h = _rms(x_ref[...], g_ref[...]) * (1.0 + m[4:5]) + m[3:4]
    h_ref[...] = h
    tm = h.shape[0]
    logits = jnp.dot(h, wr_ref[...], precision=lax.Precision.HIGHEST, preferred_element_type=F32) + br_ref[...]
    lane = lax.broadcasted_iota(I32, (tm, LANES), 1)
    lane_f = lane.astype(F32)

    def first_argmax(vals, mx):
        return jnp.min(jnp.where(vals == mx, lane_f, float(LANES)), axis=-1, keepdims=True).astype(I32)

    lg = jnp.where(lane < N_EXPERT_GROUPS, logits, NEG_INF)
    mg = jnp.max(lg, axis=-1, keepdims=True)
    g_w = 1.0 / jnp.sum(jnp.exp(lg - mg), axis=-1, keepdims=True)
    g_idx = first_argmax(lg, mg)
    eid = lane - N_EXPERT_GROUPS
    in_grp = (eid >= 0) & (eid < N_EXPERTS) & ((eid >> 3) == g_idx)
    le = jnp.where(in_grp, logits, NEG_INF)
    m1 = jnp.max(le, axis=-1, keepdims=True)
    i1 = first_argmax(le, m1)
    le2 = jnp.where(lane == i1, NEG_INF, le)
    m2 = jnp.max(le2, axis=-1, keepdims=True)
    i2 = first_argmax(le2, m2)
    r = jnp.exp(m2 - m1)
    gate1 = g_w / (1.0 + r)
    gate2 = g_w * r / (1.0 + r)
    e1 = i1 - N_EXPERT_GROUPS
    e2 = i2 - N_EXPERT_GROUPS
    hit1 = lane == e1
    hit2 = lane == e2
    onehot = jnp.where(hit1 | hit2, 1.0, 0.0)
    rows = lax.broadcasted_iota(I32, (tm, tm), 0)
    cols = lax.broadcasted_iota(I32, (tm, tm), 1)
    before = jnp.where(rows > cols, 1.0, 0.0).astype(BF16)
    rank = _dot(before, onehot.astype(BF16))
    r1 = jnp.sum(jnp.where(hit1, rank, 0.0), axis=-1, keepdims=True).astype(I32)
    r2 = jnp.sum(jnp.where(hit2, rank, 0.0), axis=-1, keepdims=True).astype(I32)
    ids_ref[...] = jnp.where(lane == 0, e1, jnp.where(lane == 1, e2, jnp.where(lane == 2, r1,
                                                                               jnp.where(lane == 3, r2, 0))))
    gate_ref[...] = jnp.where(lane == 0, gate1, jnp.where(lane == 1, gate2, 0.0))
    cnt_ref[0] = jnp.sum(onehot, axis=0, keepdims=True).astype(I32)


def _router(x2d, mod_l, g_norm, w_router, b_router, seq):
    t, d = x2d.shape
    tm = _tile(seq, 512)
    per_b = seq // tm
    nt = t // tm
    row = lambda w: pl.BlockSpec((tm, w), lambda i: (i, 0))
    return pl.pallas_call(
        _router_kernel,
        out_shape=(jax.ShapeDtypeStruct((t, d), F32), jax.ShapeDtypeStruct((t, LANES), I32),
                   jax.ShapeDtypeStruct((t, LANES), F32), jax.ShapeDtypeStruct((nt, 1, LANES), I32)),
        grid=(nt,),
        in_specs=[row(d), pl.BlockSpec((1, 6, d), lambda i: (i // per_b, 0, 0)),
                  pl.BlockSpec((1, d), lambda i: (0, 0)), pl.BlockSpec((d, LANES), lambda i: (0, 0)),
                  pl.BlockSpec((1, LANES), lambda i: (0, 0))],
        out_specs=(row(d), row(LANES), row(LANES), pl.BlockSpec((1, 1, LANES), lambda i: (i, 0, 0))),
        compiler_params=_cparams(("parallel",), VMEM_LIMIT),
        name="moe_router",
    )(x2d, mod_l, g_norm, w_router, b_router)


def _dest_kernel(ids_ref, base_ref, o_ref):
    ids = ids_ref[...]
    lane = lax.broadcasted_iota(I32, ids.shape, 1)
    base = base_ref[0].astype(F32)

    def pick(col):
        e = ids[:, col:col + 1]
        return jnp.sum(jnp.where(lane == e, base, 0.0), axis=-1, keepdims=True).astype(I32)

    d1 = ids[:, 2:3] + pick(0)
    d2 = ids[:, 3:4] + pick(1)
    o_ref[...] = jnp.where(lane == 0, d1, jnp.where(lane == 1, d2, 0))


def _dest_rows(ids, base, tm):
    t = ids.shape[0]
    return pl.pallas_call(
        _dest_kernel,
        out_shape=jax.ShapeDtypeStruct((t, LANES), I32),
        grid=(t // tm,),
        in_specs=[pl.BlockSpec((tm, LANES), lambda i: (i, 0)), pl.BlockSpec((1, 1, LANES), lambda i: (i, 0, 0))],
        out_specs=pl.BlockSpec((tm, LANES), lambda i: (i, 0)),
        compiler_params=_cparams(("parallel",)),
        name="moe_dest",
    )(ids, base)


DMA_UNROLL = 8


def _scatter_kernel(dest_ref, h_ref, rows_in_ref, rows_ref, idx_smem, idx_sem, sem, *, tm):
    del rows_in_ref
    i = pl.program_id(0)
    cp = pltpu.make_async_copy(dest_ref.at[i], idx_smem, idx_sem)
    cp.start()
    cp.wait()

    def row_copy(t, k):
        return pltpu.make_async_copy(h_ref.at[pl.ds(t, 1)], rows_ref.at[pl.ds(idx_smem[2 * t + k], 1)], sem)

    def issue(t, carry):
        row_copy(t, 0).start(priority=0)
        row_copy(t, 1).start(priority=1)
        return carry

    lax.fori_loop(0, tm, issue, 0, unroll=DMA_UNROLL)

    def drain(t, carry):
        row_copy(t, 0).wait()
        row_copy(t, 1).wait()
        return carry

    lax.fori_loop(0, tm, drain, 0, unroll=DMA_UNROLL)


def _scatter_rows(dest_tiles, h2d, n_rows, tm):
    t, d = h2d.shape
    nt = t // tm
    zeros = jnp.zeros((n_rows, d), h2d.dtype)
    return pl.pallas_call(
        functools.partial(_scatter_kernel, tm=tm),
        out_shape=jax.ShapeDtypeStruct((n_rows, d), h2d.dtype),
        grid=(nt,),
        in_specs=[pl.BlockSpec(memory_space=pl.ANY), pl.BlockSpec((tm, d), lambda i: (i, 0)),
                  pl.BlockSpec(memory_space=pl.ANY)],
        out_specs=pl.BlockSpec(memory_space=pl.ANY),
        scratch_shapes=[pltpu.SMEM((2 * tm,), I32), pltpu.SemaphoreType.DMA, pltpu.SemaphoreType.DMA],
        input_output_aliases={2: 0},
        compiler_params=_cparams(("arbitrary",)),
        name="moe_scatter",
    )(dest_tiles, h2d, zeros)


def _ffn_kernel(be_ref, nu_ref, rows_ref, w1_ref, w3_ref, w2_ref, y_ref, w13_scr, w2_scr):
    j = pl.program_id(0)
    used = j < nu_ref[0]
    new_expert = (j == 0) | (be_ref[j] != be_ref[jnp.maximum(j - 1, 0)])

    @pl.when(used & new_expert)
    def _():
        w13_scr[:, :D_EXPERT] = w1_ref[0].astype(BF16)
        w13_scr[:, D_EXPERT:] = w3_ref[0].astype(BF16)
        w2_scr[...] = w2_ref[0].astype(BF16)

    @pl.when(used)
    def _():
        h = _dot(rows_ref[...].astype(BF16), w13_scr[...])
        a = h[:, :D_EXPERT]
        act = a * jax.nn.sigmoid(a) * h[:, D_EXPERT:]
        y_ref[...] = _dot(act.astype(BF16), w2_scr[...])

    @pl.when(jnp.logical_not(used))
    def _():
        y_ref[...] = jnp.zeros_like(y_ref)


def _grouped_ffn(block_expert, n_used, rows, w1, w3, w2):
    n_rows, d = rows.shape
    nb = n_rows // MOE_BM
    grid_spec = pltpu.PrefetchScalarGridSpec(
        num_scalar_prefetch=2,
        grid=(nb,),
        in_specs=[
            pl.BlockSpec((MOE_BM, d), lambda j, be, nu: (j, 0)),
            pl.BlockSpec((1, d, D_EXPERT), lambda j, be, nu: (be[j], 0, 0)),
            pl.BlockSpec((1, d, D_EXPERT), lambda j, be, nu: (be[j], 0, 0)),
            pl.BlockSpec((1, D_EXPERT, d), lambda j, be, nu: (be[j], 0, 0)),
        ],
        out_specs=pl.BlockSpec((MOE_BM, d), lambda j, be, nu: (j, 0)),
        scratch_shapes=[pltpu.VMEM((d, 2 * D_EXPERT), BF16), pltpu.VMEM((D_EXPERT, d), BF16)],
    )
    return pl.pallas_call(
        _ffn_kernel,
        out_shape=jax.ShapeDtypeStruct((n_rows, d), F32),
        grid_spec=grid_spec,
        compiler_params=_cparams(("arbitrary",), VMEM_LIMIT),
        name="moe_grouped_ffn",
    )(block_expert, n_used, rows, w1, w3, w2)


def _combine_kernel(dest_ref, y_ref, gate_ref, x_ref, mod_ref, o_ref, idx_smem, ybuf, idx_sem, sem, *, tm):
    i = pl.program_id(0)
    cp = pltpu.make_async_copy(dest_ref.at[i], idx_smem, idx_sem)
    cp.start()
    cp.wait()

    def row_copy(t, k):
        return pltpu.make_async_copy(y_ref.at[pl.ds(idx_smem[2 * t + k], 1)], ybuf.at[k].at[pl.ds(t, 1)], sem)

    def issue(t, carry):
        row_copy(t, 0).start(priority=0)
        row_copy(t, 1).start(priority=1)
        return carry

    lax.fori_loop(0, tm, issue, 0, unroll=DMA_UNROLL)

    def drain(t, carry):
        row_copy(t, 0).wait()
        row_copy(t, 1).wait()
        return carry

    lax.fori_loop(0, tm, drain, 0, unroll=DMA_UNROLL)
    g = gate_ref[...]
    moe = g[:, 0:1] * ybuf[0] + g[:, 1:2] * ybuf[1]
    o_ref[...] = x_ref[...] + mod_ref[0][5:6] * moe


def _combine(dest_tiles, y, gates, x2d, mod_l, seq, tm):
    t, d = x2d.shape
    per_b = seq // tm
    return pl.pallas_call(
        functools.partial(_combine_kernel, tm=tm),
        out_shape=jax.ShapeDtypeStruct((t, d), F32),
        grid=(t // tm,),
        in_specs=[pl.BlockSpec(memory_space=pl.ANY), pl.BlockSpec(memory_space=pl.ANY),
                  pl.BlockSpec((tm, LANES), lambda i: (i, 0)), pl.BlockSpec((tm, d), lambda i: (i, 0)),
                  pl.BlockSpec((1, 6, d), lambda i: (i // per_b, 0, 0))],
        out_specs=pl.BlockSpec((tm, d), lambda i: (i, 0)),
        scratch_shapes=[pltpu.SMEM((2 * tm,), I32), pltpu.VMEM((2, tm, d), F32),
                        pltpu.SemaphoreType.DMA, pltpu.SemaphoreType.DMA],
        compiler_params=_cparams(("arbitrary",), VMEM_LIMIT),
        name="moe_combine",
    )(dest_tiles, y, gates, x2d, mod_l)


def _moe(x2d, mod_l, g_norm2, w_gr, b_gr, w_er, b_er, w1, w3, w2, seq):
    t, d = x2d.shape
    pad = LANES - N_EXPERT_GROUPS - N_EXPERTS
    w_router = jnp.concatenate([w_gr, w_er, jnp.zeros((d, pad), F32)], axis=1)
    b_router = jnp.concatenate([b_gr, b_er, jnp.zeros((pad,), F32)]).reshape(1, LANES)
    tm = _tile(seq, 512)
    nt = t // tm
    h2, ids, gates, cnt = _router(x2d, mod_l, g_norm2, w_router, b_router, seq)

    cnt = cnt[:, 0, :N_EXPERTS]
    total = jnp.sum(cnt, axis=0)
    padded = (total + MOE_BM - 1) // MOE_BM * MOE_BM
    ends = jnp.cumsum(padded)
    tile_off = jnp.cumsum(cnt, axis=0) - cnt
    base = (ends - padded)[None, :] + tile_off
    base = jnp.pad(base, ((0, 0), (0, LANES - N_EXPERTS))).reshape(nt, 1, LANES)
    nb = (2 * t) // MOE_BM + N_EXPERTS
    block_start = jnp.arange(nb, dtype=I32) * MOE_BM
    block_expert = jnp.minimum(jnp.sum(block_start[:, None] >= ends[None, :], axis=1), N_EXPERTS - 1).astype(I32)
    n_used = (ends[-1] // MOE_BM).astype(I32).reshape(1)

    dest = _dest_rows(ids, base.astype(I32), tm)
    dest_tiles = dest[:, :2].reshape(nt, 2 * tm)
    rows = _scatter_rows(dest_tiles, h2, nb * MOE_BM, tm)
    y = _grouped_ffn(block_expert, n_used, rows, w1, w3, w2)
    return _combine(dest_tiles, y, gates, x2d, mod_l, seq, tm)


def _seg_matrix(width, segments):
    idx = jnp.arange(width)
    m = jnp.zeros((width, width), F32)
    for start, length in segments:
        inside = (idx >= start) & (idx < start + length)
        m = m + jnp.where(inside[:, None] & inside[None, :], 1.0 / length, 0.0)
    return m.astype(BF16)


def _layout_w_in(w_in):
    depth, d, _ = w_in.shape
    sizes = [MLA_Q_RANK, MLA_KV_RANK, MLA_ROPE, N_DIL_COLS,
             (WIN_HEADS + 2 * WIN_KV_HEADS) * HEAD_DIM, 3 * D_MODEL]
    bounds = [sum(sizes[:k + 1]) for k in range(len(sizes) - 1)]
    c_q, c_kv, k_rope, dil, win, gate = jnp.split(w_in, bounds, axis=-1)
    win_q, win_kv = win[..., :WIN_HEADS * HEAD_DIM], win[..., WIN_HEADS * HEAD_DIM:]
    z = lambda w: jnp.zeros((depth, d, w), w_in.dtype)
    out = jnp.concatenate([gate, c_q, win_q, c_kv, win_kv, z(MLA_NOPE), k_rope,
                           z(N_COLS - OFF_KR - MLA_NOPE - MLA_ROPE)], axis=-1)
    assert out.shape[-1] == N_COLS
    return out.astype(BF16), dil.astype(BF16)


def _layout_mla(w_uq, w_ukv, g_q, g_k):
    qd = MLA_NOPE + MLA_ROPE
    wq = w_uq.reshape(MLA_Q_RANK, MLA_HEADS, qd)
    wq = jnp.pad(wq, ((0, 0), (0, 0), (0, SLOT - qd))).reshape(MLA_Q_RANK, MLA_HEADS * SLOT)
    wkv = w_ukv.reshape(MLA_KV_RANK, MLA_HEADS, MLA_NOPE + MLA_V)
    wk = jnp.pad(wkv[:, :, :MLA_NOPE], ((0, 0), (0, 0), (0, SLOT - MLA_NOPE))).reshape(MLA_KV_RANK, MLA_HEADS * SLOT)
    wv = wkv[:, :, MLA_NOPE:].reshape(MLA_KV_RANK, MLA_HEADS * MLA_V)
    scale = float(qd) ** -0.5
    gq_slot = (jnp.pad(g_q, (0, SLOT - qd)) * scale).reshape(1, SLOT)
    gkn_slot = jnp.pad(g_k[:MLA_NOPE], (0, SLOT - MLA_NOPE)).reshape(1, SLOT)
    gkr_slot = jnp.pad(g_k[MLA_NOPE:], (MLA_NOPE, SLOT - qd)).reshape(1, SLOT)
    return wq.astype(BF16), jnp.concatenate([wk, wv], axis=1).astype(BF16), gq_slot, gkn_slot, gkr_slot


def kernel(x, c, pos, w_ada, b_ada, g_norm1, w_in, g_cq, w_uq, g_ckv, w_ukv, g_q_mla, g_k_mla, g_q_dil, g_k_dil,
           g_q_win, g_k_win, sink_win, w_br_mla, w_br_dil, w_br_win, w_out, g_norm2, w_gr, b_gr, w_er, b_er,
           w1, w3, w2):
    batch, seq, d = x.shape
    depth = w_ada.shape[0]
    t = batch * seq
    half = MLA_ROPE // 2
    inv_freq = ROPE_THETA ** (-jnp.arange(half, dtype=F32) / half)
    invf_slot = jnp.concatenate([jnp.zeros((MLA_NOPE,), F32), inv_freq, inv_freq,
                                 jnp.zeros((SLOT - MLA_NOPE - MLA_ROPE,), F32)]).reshape(1, SLOT)
    cos_t, sin_t = _rope_tables(pos.reshape(t, 1), invf_slot)
    mod = _modulation(c, w_ada, b_ada)
    w_in_k, w_dil_k = _layout_w_in(w_in)
    mseg_slot = _seg_matrix(SLOT, ((0, MLA_NOPE), (MLA_NOPE, MLA_ROPE)))
    mseg128 = _seg_matrix(128, tuple((k * HEAD_DIM, HEAD_DIM) for k in range(2)))
    head_scale = float(HEAD_DIM) ** -0.5

    x2d = x.reshape(t, d)
    for l in range(depth):
        mod_l = mod[l].reshape(batch, 6, d)
        g1 = g_norm1[l].reshape(1, d)
        cols = _in_projection(x2d, mod_l, g1, w_in_k[l], seq, BF16, N_COLS // 2, "in_projection")
        dil_cols = _in_projection(x2d, mod_l, g1, w_dil_k[l], seq, F32, N_DIL_COLS // 3, "in_projection_dil")

        wuq, wukv, gq_slot, gkn_slot, gkr_slot = _layout_mla(w_uq[l], w_ukv[l], g_q_mla[l], g_k_mla[l])
        qm, km, vm = _mla_prep(cols, cos_t, sin_t, g_cq[l].reshape(1, -1), g_ckv[l].reshape(1, -1),
                               wuq, wukv, gq_slot, gkn_slot, gkr_slot, mseg_slot)
        o_mla = _mla_attention(qm, km, vm, batch, seq)

        gq_dil = (jnp.tile(g_q_dil[l], 2) * head_scale).reshape(1, -1)
        gk_dil = jnp.tile(g_k_dil[l], 2).reshape(1, -1)
        o_dil, lse_dil = [], []
        for gi in range(DIL_GROUPS):
            o_g, lse_g = _dilated_group(dil_cols, pos, gq_dil, gk_dil, mseg128, gi, batch, seq)
            o_dil.append(o_g)
            lse_dil.append(lse_g)

        gq_win = (jnp.tile(g_q_win[l], 2) * head_scale).reshape(1, -1)
        gk_win = jnp.tile(g_k_win[l], WIN_KV_HEADS).reshape(1, -1)
        sink_rows = jnp.broadcast_to(sink_win[l].astype(F32)[:, None], (WIN_HEADS, LANES))
        o_win = _window_mixer(cols, pos, gq_win, gk_win, sink_rows, mseg128, batch, seq)

        x2d = _merge(cols, o_mla, o_dil, lse_dil, o_win, x2d, mod_l, w_br_mla[l].astype(BF16),
                     w_br_dil[l].astype(BF16), w_br_win[l].astype(BF16), w_out[l].astype(BF16), seq)
        x2d = _moe(x2d, mod_l, g_norm2[l].reshape(1, d), w_gr[l], b_gr[l], w_er[l], b_er[l],
                   w1[l], w3[l], w2[l], seq)
    return x2d.reshape(batch, seq, d)
```

```python
import functools
import math

import jax
import jax.numpy as jnp
from jax import lax
from jax.experimental import pallas as pl
from jax.experimental.pallas import tpu as pltpu

F32 = jnp.float32
BF16 = jnp.bfloat16
I32 = jnp.int32

D_MODEL = 1024
HEAD_DIM = 64
NEG_INF = -1e30
EPS = 1e-6
LOG2E = math.log2(math.e)
LN2 = math.log(2.0)
MLA_HEADS = 8
MLA_Q_RANK = 512
MLA_KV_RANK = 256
MLA_NOPE = 64
MLA_ROPE = 32
MLA_V = 64
ROPE_THETA = 10000.0
DIL_PATTERNS = ((128, 1), (512, 4), (2048, 16))
DIL_GROUPS = 3
DIL_HPG = 4
DIL_HEADS = DIL_GROUPS * DIL_HPG
DIL_RADIUS = 64
WIN_HEADS = 8
WIN_KV_HEADS = 2
WIN_RADIUS = 128
N_EXPERT_GROUPS = 4
EXPERTS_PER_GROUP = 8
N_EXPERTS = N_EXPERT_GROUPS * EXPERTS_PER_GROUP
D_EXPERT = 384

LANES = 128
SLOT = 128
VMEM_LIMIT = 48 * 1024 * 1024

OFF_GATE = 0
OFF_CQ = 3072
OFF_WQ = 3584
OFF_CKV = 4096
OFF_WK = 4352
OFF_WV = 4480
OFF_KR = 4608
N_COLS = 4864
N_DIL_COLS = 3 * DIL_HEADS * HEAD_DIM

MOE_BM = 256
MLA_TQ = 256


def _cparams(sem, vmem=None, flags=None):
    return pltpu.CompilerParams(dimension_semantics=sem, vmem_limit_bytes=vmem, flags=flags)


def _tile(n, pref):
    t = min(n, pref)
    assert n % t == 0, (n, pref)
    return t


def _dot(a, b):
    return jnp.dot(a, b, preferred_element_type=F32)


def _dot_nt(a, b):
    return lax.dot_general(a, b, (((1,), (1,)), ((), ())), preferred_element_type=F32)


def _dot_tn(a, b):
    return lax.dot_general(a, b, (((0,), (0,)), ((), ())), preferred_element_type=F32)


def _seg_mean_sq(x, mseg):
    x2 = x * x
    hi = x2.astype(BF16)
    lo = (x2 - hi.astype(F32)).astype(BF16)
    return _dot(hi, mseg) + _dot(lo, mseg)


def _seg_norm(x, mseg, gain):
    return x * lax.rsqrt(_seg_mean_sq(x, mseg) + EPS) * gain


def _rms(x, gain):
    ms = jnp.mean(x * x, axis=-1, keepdims=True)
    return x * lax.rsqrt(ms + EPS) * gain


def _mod_kernel(c_ref, w_ref, b_ref, o_ref):
    c = c_ref[...]
    cond = (c * jax.nn.sigmoid(c)).astype(BF16)
    o_ref[0] = _dot(cond, w_ref[0].astype(BF16)) + b_ref[0]


def _modulation(c, w_ada, b_ada):
    depth, d, n = w_ada.shape
    b = c.shape[0]
    tn = _tile(n, 1536)
    return pl.pallas_call(
        _mod_kernel,
        out_shape=jax.ShapeDtypeStruct((depth, b, n), F32),
        grid=(depth, n // tn),
        in_specs=[
            pl.BlockSpec((b, d), lambda l, j: (0, 0)),
            pl.BlockSpec((1, d, tn), lambda l, j: (l, 0, j)),
            pl.BlockSpec((1, 1, tn), lambda l, j: (l, 0, j)),
        ],
        out_specs=pl.BlockSpec((1, b, tn), lambda l, j: (l, 0, j)),
        compiler_params=_cparams(("parallel", "parallel")),
        name="adaln_mod",
    )(c, w_ada, b_ada.reshape(depth, 1, n))


def _rope_table_kernel(pos_ref, invf_ref, cos_ref, sin_ref):
    ang = pos_ref[...].astype(F32) * invf_ref[...]
    cos_ref[...] = jnp.cos(ang)
    sin_ref[...] = jnp.sin(ang)


def _rope_tables(pos_col, invf_slot):
    t = pos_col.shape[0]
    tm = _tile(t, 1024)
    return pl.pallas_call(
        _rope_table_kernel,
        out_shape=(jax.ShapeDtypeStruct((t, SLOT), F32), jax.ShapeDtypeStruct((t, SLOT), F32)),
        grid=(t // tm,),
        in_specs=[pl.BlockSpec((tm, 1), lambda i: (i, 0)), pl.BlockSpec((1, SLOT), lambda i: (0, 0))],
        out_specs=(pl.BlockSpec((tm, SLOT), lambda i: (i, 0)), pl.BlockSpec((tm, SLOT), lambda i: (i, 0))),
        compiler_params=_cparams(("parallel",)),
        name="rope_tables",
    )(pos_col, invf_slot)


def _inproj_kernel(x_ref, mod_ref, g_ref, w_ref, o_ref, h_scr):
    @pl.when(pl.program_id(1) == 0)
    def _():
        m = mod_ref[0]
        h = _rms(x_ref[...], g_ref[...]) * (1.0 + m[1:2]) + m[0:1]
        h_scr[...] = h.astype(BF16)

    o_ref[...] = _dot(h_scr[...], w_ref[...]).astype(o_ref.dtype)


def _in_projection(x2d, mod_l, g_norm, w_in_l, seq, out_dtype, tn, name):
    t, d = x2d.shape
    nc = w_in_l.shape[1]
    tm = _tile(seq, 1024)
    assert nc % tn == 0
    per_b = seq // tm
    return pl.pallas_call(
        _inproj_kernel,
        out_shape=jax.ShapeDtypeStruct((t, nc), out_dtype),
        grid=(t // tm, nc // tn),
        in_specs=[
            pl.BlockSpec((tm, d), lambda i, j: (i, 0)),
            pl.BlockSpec((1, 6, d), lambda i, j: (i // per_b, 0, 0)),
            pl.BlockSpec((1, d), lambda i, j: (0, 0)),
            pl.BlockSpec((d, tn), lambda i, j: (0, j)),
        ],
        out_specs=pl.BlockSpec((tm, tn), lambda i, j: (i, j)),
        scratch_shapes=[pltpu.VMEM((tm, d), BF16)],
        compiler_params=_cparams(("parallel", "arbitrary"), VMEM_LIMIT),
        name=name,
    )(x2d, mod_l, g_norm, w_in_l)


def _mla_prep_kernel(cq_ref, ckv_ref, kr_ref, cos_ref, sin_ref, gcq_ref, gckv_ref, wuq_ref, wukv_ref,
                     gq_ref, gkn_ref, gkr_ref, mseg_ref, q_out, k_out, v_out):
    cos = cos_ref[...]
    sin = sin_ref[...]
    lane = lax.broadcasted_iota(I32, (1, SLOT), 1)
    s_neg = jnp.where((lane >= 64) & (lane < 80), -sin, 0.0)
    s_pos = jnp.where((lane >= 80) & (lane < 96), sin, 0.0)
    mseg = mseg_ref[...]

    def rope(xn):
        return xn * cos + pltpu.roll(xn, SLOT - 16, 1) * s_neg + pltpu.roll(xn, 16, 1) * s_pos

    cqn = _rms(cq_ref[...].astype(F32), gcq_ref[...]).astype(BF16)
    q = _dot(cqn, wuq_ref[...])
    for h in range(MLA_HEADS):
        sl = slice(h * SLOT, (h + 1) * SLOT)
        q_out[:, sl] = rope(_seg_norm(q[:, sl], mseg, gq_ref[...])).astype(q_out.dtype)

    ckvn = _rms(ckv_ref[...].astype(F32), gckv_ref[...]).astype(BF16)
    kv = _dot(ckvn, wukv_ref[...])
    kr = rope(_seg_norm(kr_ref[...].astype(F32), mseg, gkr_ref[...]))
    for h in range(MLA_HEADS):
        sl = slice(h * SLOT, (h + 1) * SLOT)
        k_out[:, sl] = (_seg_norm(kv[:, sl], mseg, gkn_ref[...]) + kr).astype(k_out.dtype)
    v_out[...] = kv[:, MLA_HEADS * SLOT:].astype(v_out.dtype)


def _mla_prep(cols, cos_t, sin_t, gcq, gckv, wuq, wukv, gq_slot, gkn_slot, gkr_slot, mseg):
    t = cols.shape[0]
    tm = _tile(t, 512)
    hs = MLA_HEADS * SLOT
    full = lambda shape: pl.BlockSpec(shape, lambda i: (0,) * len(shape))
    return pl.pallas_call(
        _mla_prep_kernel,
        out_shape=(jax.ShapeDtypeStruct((t, hs), BF16), jax.ShapeDtypeStruct((t, hs), BF16),
                   jax.ShapeDtypeStruct((t, MLA_HEADS * MLA_V), BF16)),
        grid=(t // tm,),
        in_specs=[
            pl.BlockSpec((tm, MLA_Q_RANK), lambda i: (i, OFF_CQ // MLA_Q_RANK)),
            pl.BlockSpec((tm, MLA_KV_RANK), lambda i: (i, OFF_CKV // MLA_KV_RANK)),
            pl.BlockSpec((tm, SLOT), lambda i: (i, OFF_KR // SLOT)),
            pl.BlockSpec((tm, SLOT), lambda i: (i, 0)),
            pl.BlockSpec((tm, SLOT), lambda i: (i, 0)),
            full((1, MLA_Q_RANK)), full((1, MLA_KV_RANK)),
            full((MLA_Q_RANK, hs)), full((MLA_KV_RANK, hs + MLA_HEADS * MLA_V)),
            full((1, SLOT)), full((1, SLOT)), full((1, SLOT)), full((SLOT, SLOT)),
        ],
        out_specs=(pl.BlockSpec((tm, hs), lambda i: (i, 0)), pl.BlockSpec((tm, hs), lambda i: (i, 0)),
                   pl.BlockSpec((tm, MLA_HEADS * MLA_V), lambda i: (i, 0))),
        compiler_params=_cparams(("parallel",), VMEM_LIMIT),
        name="mla_prep",
    )(cols, cols, cols, cos_t, sin_t, gcq, gckv, wuq, wukv, gq_slot, gkn_slot, gkr_slot, mseg)


def _mla_attn_kernel(q_ref, k_ref, v_ref, o_ref):
    outs = []
    for a in range(2):
        sl = slice(a * SLOT, (a + 1) * SLOT)
        s = _dot_nt(q_ref[0, :, sl], k_ref[0, :, sl])
        m = jnp.max(s, axis=-1, keepdims=True)
        p = jnp.exp2(s - m)
        l = jnp.sum(p, axis=-1, keepdims=True)
        outs.append(_dot(p.astype(BF16), v_ref[0]) / l)
    lane = lax.broadcasted_iota(I32, (1, 2 * MLA_V), 1)
    o_ref[0] = jnp.where(lane < MLA_V, outs[0], outs[1]).astype(o_ref.dtype)


def _mla_attention(qm, km, vm, batch, seq):
    hs = MLA_HEADS * SLOT
    q3 = qm.reshape(batch, seq, hs)
    k3 = km.reshape(batch, seq, hs)
    v3 = vm.reshape(batch, seq, MLA_HEADS * MLA_V)
    tq = _tile(seq, MLA_TQ)
    out = pl.pallas_call(
        _mla_attn_kernel,
        out_shape=jax.ShapeDtypeStruct((batch, seq, MLA_HEADS * MLA_V), BF16),
        grid=(batch, MLA_HEADS // 2, seq // tq),
        in_specs=[
            pl.BlockSpec((1, tq, 2 * SLOT), lambda b, p, i: (b, i, p)),
            pl.BlockSpec((1, seq, 2 * SLOT), lambda b, p, i: (b, 0, p)),
            pl.BlockSpec((1, seq, 2 * MLA_V), lambda b, p, i: (b, 0, p)),
        ],
        out_specs=pl.BlockSpec((1, tq, 2 * MLA_V), lambda b, p, i: (b, i, p)),
        compiler_params=_cparams(("parallel", "parallel", "arbitrary"), VMEM_LIMIT),
        name="mla_attention",
    )(q3, k3, v3)
    return out.reshape(batch * seq, MLA_HEADS * MLA_V)


BAND_UNROLL = 8


def _window_start(i, bq, radius, n, kw):
    ws = jnp.clip(i * bq - radius, 0, n - kw)
    return pl.multiple_of(ws, 16)


def _band_valid_t(i, bq, ws, kw, radius, heads):
    keys = lax.broadcasted_iota(I32, (kw, heads * bq), 0)
    queries = lax.broadcasted_iota(I32, (kw, heads * bq), 1) & (bq - 1)
    rel = queries - keys + (i * bq - ws)
    return jnp.maximum(rel, -rel) <= radius


def _abs_dist(pa, pb):
    d = pa - pb
    return jnp.maximum(d, -d).astype(F32)


def _sub_positions(pos, batch, n, dil, bq):
    pos_sub = jnp.transpose(pos.reshape(batch, n, dil), (0, 2, 1))
    return pos_sub.reshape(batch, dil * n, 1), pos_sub.reshape(batch, dil * (n // bq), 1, bq)


def _dil_kernel(q_ref, k_ref, v_ref, pcol_ref, prow_ref, gq_ref, gk_ref, slope_ref, mseg_ref, o_ref, lse_ref,
                kn_scr, vn_scr, *, n, bq, kw, dil):
    half = pl.program_id(1)
    nblk = n // bq
    mseg = mseg_ref[...]
    gq = gq_ref[...]
    low_lane = lax.broadcasted_iota(I32, (1, 2 * HEAD_DIM), 1) < HEAD_DIM
    slope0 = slope_ref[pl.ds(2 * half, 1), 0:1]
    slope1 = slope_ref[pl.ds(2 * half + 1, 1), 0:1]

    def rows(start, size):
        return pl.ds(start, size) if dil == 1 else pl.ds(start, size, stride=dil)

    def normed_kv(r):
        kn = _seg_norm(k_ref[0, rows(r, n), :], mseg, gk_ref[...]).astype(BF16)
        return kn, v_ref[0, rows(r, n), :].astype(BF16)

    def block(r, i, ws, kwin, vwin):
        q_rows = rows(r + i * (bq * dil), bq)
        qn = _seg_norm(q_ref[0, q_rows, :], mseg, gq)
        qs = jnp.concatenate([jnp.where(low_lane, qn, 0.0), jnp.where(low_lane, 0.0, qn)], axis=0).astype(BF16)
        dist = _abs_dist(pcol_ref[0, pl.ds(r * n + ws, kw), :], prow_ref[0, r * nblk + i])
        bias = jnp.concatenate([slope0 * dist, slope1 * dist], axis=1)
        s = jnp.where(_band_valid_t(i, bq, ws, kw, DIL_RADIUS, 2), _dot_nt(kwin, qs) - bias, NEG_INF)
        m = jnp.max(s, axis=0, keepdims=True)
        p = jnp.exp2(s - m)
        l = jnp.sum(p, axis=0, keepdims=True)
        ot = _dot_tn(vwin, p.astype(BF16)) / l
        lse = (m + jnp.log2(l)) * LN2
        o_ref[0, q_rows, :] = jnp.concatenate([ot[:HEAD_DIM, :bq], ot[HEAD_DIM:, bq:]], axis=0).T
        lse_ref[0, q_rows, :] = jnp.concatenate([jnp.broadcast_to(lse[:, :bq], (HEAD_DIM, bq)),
                                                 jnp.broadcast_to(lse[:, bq:], (HEAD_DIM, bq))], axis=0).T

    def fill(r, carry):
        base = pl.multiple_of(r * n, 16)
        kn_scr[pl.ds(base, n), :], vn_scr[pl.ds(base, n), :] = normed_kv(r)
        return carry

    lax.fori_loop(0, dil, fill, 0)

    def query_block(b, carry):
        r = b >> (nblk.bit_length() - 1)
        i = b & (nblk - 1)
        ws = _window_start(i, bq, DIL_RADIUS, n, kw)
        start = pl.multiple_of(r * n + ws, 16)
        block(r, i, ws, kn_scr[pl.ds(start, kw), :], vn_scr[pl.ds(start, kw), :])
        return carry

    lax.fori_loop(0, dil * nblk, query_block, 0, unroll=min(dil * nblk, BAND_UNROLL))


def _dilated_group(dil_cols, pos, gq_row, gk_row, mseg, gi, batch, seq):
    _, dil = DIL_PATTERNS[gi]
    n = seq // dil
    bq = min(128, n)
    kw = min(bq + 2 * DIL_RADIUS, n)
    nblk = n // bq
    width = DIL_HPG * HEAD_DIM
    hw = 2 * HEAD_DIM
    cols3 = dil_cols.reshape(batch, seq, N_DIL_COLS)
    pcol, prow = _sub_positions(pos, batch, n, dil, bq)
    slopes = jnp.asarray([LOG2E * 2.0 ** (-8.0 * (gi * DIL_HPG + hh + 1) / DIL_HEADS) for hh in range(DIL_HPG)], F32)
    slope_rows = jnp.broadcast_to(jnp.pad(slopes, (0, 8 - DIL_HPG))[:, None], (8, LANES))
    kern = functools.partial(_dil_kernel, n=n, bq=bq, kw=kw, dil=dil)
    col = lambda which: (lambda b, h: (b, 0, (which * DIL_GROUPS + gi) * 2 + h))
    small = lambda shape: pl.BlockSpec(shape, lambda b, h: (0,) * len(shape))
    o, lse = pl.pallas_call(
        kern,
        out_shape=(jax.ShapeDtypeStruct((batch, seq, width), F32), jax.ShapeDtypeStruct((batch, seq, width), F32)),
        grid=(batch, 2),
        in_specs=[
            pl.BlockSpec((1, seq, hw), col(0)),
            pl.BlockSpec((1, seq, hw), col(1)),
            pl.BlockSpec((1, seq, hw), col(2)),
            pl.BlockSpec((1, seq, 1), lambda b, h: (b, 0, 0)),
            pl.BlockSpec((1, dil * nblk, 1, bq), lambda b, h: (b, 0, 0, 0)),
            small((1, hw)), small((1, hw)), small((8, LANES)), small((hw, hw)),
        ],
        out_specs=(pl.BlockSpec((1, seq, hw), lambda b, h: (b, 0, h)),
                   pl.BlockSpec((1, seq, hw), lambda b, h: (b, 0, h))),
        scratch_shapes=[pltpu.VMEM((seq, hw), BF16), pltpu.VMEM((seq, hw), BF16)],
        compiler_params=_cparams(("parallel", "parallel"), VMEM_LIMIT),
        name=f"dilated_group{gi}",
    )(cols3, cols3, cols3, pcol, prow, gq_row, gk_row, slope_rows, mseg)
    return o.reshape(batch * seq, width), lse.reshape(batch * seq, width)


def _win_kernel(q_ref, k_ref, v_ref, pcol_ref, prow_ref, gq_ref, gk_ref, sink_ref, mseg_ref, o_ref, kn_scr,
                *, n, bq, kw, slopes):
    i = pl.program_id(1)
    mseg = mseg_ref[...]

    @pl.when(i == 0)
    def _():
        kn_scr[...] = _seg_norm(k_ref[0].astype(F32), mseg, gk_ref[...]).astype(BF16)

    ws = _window_start(i, bq, WIN_RADIUS, n, kw)
    kwin = kn_scr[pl.ds(ws, kw), :]
    vwin = v_ref[0, pl.ds(ws, kw), :]
    dist = _abs_dist(pcol_ref[0, pl.ds(ws, kw), :], prow_ref[0, 0])
    low_lane = lax.broadcasted_iota(I32, (1, 2 * HEAD_DIM), 1) < HEAD_DIM
    rep = WIN_HEADS // WIN_KV_HEADS

    qn_pairs = [_seg_norm(q_ref[0, :, pair * 128:(pair + 1) * 128].astype(F32), mseg, gq_ref[...])
                for pair in range(WIN_HEADS // 2)]
    q_parts = []
    for h in range(WIN_HEADS):
        pair, upper = divmod(h, 2)
        qm = jnp.where(low_lane, 0.0, qn_pairs[pair]) if upper else jnp.where(low_lane, qn_pairs[pair], 0.0)
        if upper != h // rep:
            qm = pltpu.roll(qm, HEAD_DIM, 1)
        q_parts.append(qm.astype(BF16))
    qs = jnp.concatenate(q_parts, axis=0)
    bias = jnp.concatenate([slopes[h] * dist for h in range(WIN_HEADS)], axis=1)
    sink = sink_ref[...]
    s = jnp.where(_band_valid_t(i, bq, ws, kw, WIN_RADIUS, WIN_HEADS), _dot_nt(kwin, qs) - bias, NEG_INF)
    m = jnp.maximum(jnp.max(s, axis=0, keepdims=True), sink)
    e = jnp.exp2(s - m)
    den = jnp.sum(e, axis=0, keepdims=True) + jnp.exp2(sink - m)
    ot = _dot_tn(vwin, e.astype(BF16)) / den
    for pair in range(WIN_HEADS // 2):
        slabs = []
        for h in (2 * pair, 2 * pair + 1):
            g = h // rep
            slabs.append(ot[g * HEAD_DIM:(g + 1) * HEAD_DIM, h * bq:(h + 1) * bq])
        o_ref[0, :, pair * 128:(pair + 1) * 128] = jnp.concatenate(slabs, axis=0).T.astype(o_ref.dtype)


def _window_mixer(cols, pos, gq_row, gk_row, sink_row, mseg128, batch, seq):
    n = seq
    bq = min(128, n)
    kw = min(bq + 2 * WIN_RADIUS, n)
    nblk = n // bq
    qw = WIN_HEADS * HEAD_DIM
    kvw = WIN_KV_HEADS * HEAD_DIM
    cols_v = cols.reshape(batch, seq, N_COLS)
    pcol = pos.reshape(batch, seq, 1)
    prow = pos.reshape(batch, nblk, 1, bq)
    slopes = tuple(float(LOG2E * 2.0 ** (-8.0 * (h + 1) / WIN_HEADS)) for h in range(WIN_HEADS))
    kern = functools.partial(_win_kernel, n=n, bq=bq, kw=kw, slopes=slopes)
    small = lambda shape: pl.BlockSpec(shape, lambda b, i: (0,) * len(shape))
    o = pl.pallas_call(
        kern,
        out_shape=jax.ShapeDtypeStruct((batch, seq, qw), BF16),
        grid=(batch, nblk),
        in_specs=[
            pl.BlockSpec((1, bq, qw), lambda b, i: (b, i, OFF_WQ // qw)),
            pl.BlockSpec((1, n, kvw), lambda b, i: (b, 0, OFF_WK // kvw)),
            pl.BlockSpec((1, n, kvw), lambda b, i: (b, 0, OFF_WV // kvw)),
            pl.BlockSpec((1, n, 1), lambda b, i: (b, 0, 0)),
            pl.BlockSpec((1, 1, 1, bq), lambda b, i: (b, i, 0, 0)),
            small((1, kvw)), small((1, kvw)), small((1, WIN_HEADS * bq)), small((kvw, kvw)),
        ],
        out_specs=pl.BlockSpec((1, bq, qw), lambda b, i: (b, i, 0)),
        scratch_shapes=[pltpu.VMEM((n, kvw), BF16)],
        compiler_params=_cparams(("parallel", "arbitrary"), VMEM_LIMIT),
        name="window_mixer",
    )(cols_v, cols_v, cols_v, pcol, prow, gq_row, gk_row, sink_row, mseg128)
    return o.reshape(batch * seq, qw)


def _merge_kernel(gm_ref, gd_ref, gw_ref, om_ref, od0_ref, od1_ref, od2_ref, l0_ref, l1_ref, l2_ref, ow_ref,
                  x_ref, mod_ref, wm_ref, wd_ref, ww_ref, wo_ref, o_ref):
    l0, l1, l2 = l0_ref[...], l1_ref[...], l2_ref[...]
    m = jnp.maximum(jnp.maximum(l0, l1), l2)
    e0, e1, e2 = jnp.exp(l0 - m), jnp.exp(l1 - m), jnp.exp(l2 - m)
    od = (e0 * od0_ref[...] + e1 * od1_ref[...] + e2 * od2_ref[...]) / (e0 + e1 + e2)
    y = jax.nn.sigmoid(gm_ref[...].astype(F32)) * _dot(om_ref[...], wm_ref[...])
    y += jax.nn.sigmoid(gd_ref[...].astype(F32)) * _dot(od.astype(BF16), wd_ref[...])
    y += jax.nn.sigmoid(gw_ref[...].astype(F32)) * _dot(ow_ref[...], ww_ref[...])
    z = _dot(y.astype(BF16), wo_ref[...])
    o_ref[...] = x_ref[...] + mod_ref[0][2:3] * z


def _merge(cols, o_mla, o_dil, lse_dil, o_win, x2d, mod_l, wm, wd, ww, wo, seq):
    t, d = x2d.shape
    tm = _tile(seq, 512)
    per_b = seq // tm
    row = lambda w, j=0: pl.BlockSpec((tm, w), lambda i: (i, j))
    full = lambda a: pl.BlockSpec(a.shape, lambda i: (0, 0))
    dw = DIL_HPG * HEAD_DIM
    return pl.pallas_call(
        _merge_kernel,
        out_shape=jax.ShapeDtypeStruct((t, d), F32),
        grid=(t // tm,),
        in_specs=[row(d, 0), row(d, 1), row(d, 2), row(o_mla.shape[1]),
                  row(dw), row(dw), row(dw), row(dw), row(dw), row(dw), row(o_win.shape[1]),
                  row(d), pl.BlockSpec((1, 6, d), lambda i: (i // per_b, 0, 0)),
                  full(wm), full(wd), full(ww), full(wo)],
        out_specs=row(d),
        compiler_params=_cparams(("parallel",), VMEM_LIMIT),
        name="merge_out_proj",
    )(cols, cols, cols, o_mla, o_dil[0], o_dil[1], o_dil[2], lse_dil[0], lse_dil[1], lse_dil[2], o_win,
      x2d, mod_l, wm, wd, ww, wo)


def _router_kernel(x_ref, mod_ref, g_ref, wr_ref, br_ref, h_ref, ids_ref, gate_ref, cnt_ref):
    m = mod_ref[0]
    h = _rms(x_ref[...], g_ref[...]) * (1.0 + m[4:5]) + m[3:4]
    h_ref[...] = h
    tm = h.shape[0]
    logits = jnp.dot(h, wr_ref[...], precision=lax.Precision.HIGHEST, preferred_element_type=F32) + br_ref[...]
    lane = lax.broadcasted_iota(I32, (tm, LANES), 1)
    lane_f = lane.astype(F32)

    def first_argmax(vals, mx):
        return jnp.min(jnp.where(vals == mx, lane_f, float(LANES)), axis=-1, keepdims=True).astype(I32)

    lg = jnp.where(lane < N_EXPERT_GROUPS, logits, NEG_INF)
    mg = jnp.max(lg, axis=-1, keepdims=True)
    g_w = 1.0 / jnp.sum(jnp.exp(lg - mg), axis=-1, keepdims=True)
    g_idx = first_argmax(lg, mg)
    eid = lane - N_EXPERT_GROUPS
    in_grp = (eid >= 0) & (eid < N_EXPERTS) & ((eid >> 3) == g_idx)
    le = jnp.where(in_grp, logits, NEG_INF)
    m1 = jnp.max(le, axis=-1, keepdims=True)
    i1 = first_argmax(le, m1)
    le2 = jnp.where(lane == i1, NEG_INF, le)
    m2 = jnp.max(le2, axis=-1, keepdims=True)
    i2 = first_argmax(le2, m2)
    r = jnp.exp(m2 - m1)
    gate1 = g_w / (1.0 + r)
    gate2 = g_w * r / (1.0 + r)
    e1 = i1 - N_EXPERT_GROUPS
    e2 = i2 - N_EXPERT_GROUPS
    hit1 = lane == e1
    hit2 = lane == e2
    onehot = jnp.where(hit1 | hit2, 1.0, 0.0)
    rows = lax.broadcasted_iota(I32, (tm, tm), 0)
    cols = lax.broadcasted_iota(I32, (tm, tm), 1)
    before = jnp.where(rows > cols, 1.0, 0.0).astype(BF16)
    rank = _dot(before, onehot.astype(BF16))
    r1 = jnp.sum(jnp.where(hit1, rank, 0.0), axis=-1, keepdims=True).astype(I32)
    r2 = jnp.sum(jnp.where(hit2, rank, 0.0), axis=-1, keepdims=True).astype(I32)
    ids_ref[...] = jnp.where(lane == 0, e1, jnp.where(lane == 1, e2, jnp.where(lane == 2, r1,
                                                                               jnp.where(lane == 3, r2, 0))))
    gate_ref[...] = jnp.where(lane == 0, gate1, jnp.where(lane == 1, gate2, 0.0))
    cnt_ref[0] = jnp.sum(onehot, axis=0, keepdims=True).astype(I32)


def _router(x2d, mod_l, g_norm, w_router, b_router, seq):
    t, d = x2d.shape
    tm = _tile(seq, 512)
    per_b = seq // tm
    nt = t // tm
    row = lambda w: pl.BlockSpec((tm, w), lambda i: (i, 0))
    return pl.pallas_call(
        _router_kernel,
        out_shape=(jax.ShapeDtypeStruct((t, d), F32), jax.ShapeDtypeStruct((t, LANES), I32),
                   jax.ShapeDtypeStruct((t, LANES), F32), jax.ShapeDtypeStruct((nt, 1, LANES), I32)),
        grid=(nt,),
        in_specs=[row(d), pl.BlockSpec((1, 6, d), lambda i: (i // per_b, 0, 0)),
                  pl.BlockSpec((1, d), lambda i: (0, 0)), pl.BlockSpec((d, LANES), lambda i: (0, 0)),
                  pl.BlockSpec((1, LANES), lambda i: (0, 0))],
        out_specs=(row(d), row(LANES), row(LANES), pl.BlockSpec((1, 1, LANES), lambda i: (i, 0, 0))),
        compiler_params=_cparams(("parallel",), VMEM_LIMIT),
        name="moe_router",
    )(x2d, mod_l, g_norm, w_router, b_router)


def _dest_kernel(ids_ref, base_ref, o_ref):
    ids = ids_ref[...]
    lane = lax.broadcasted_iota(I32, ids.shape, 1)
    base = base_ref[0].astype(F32)

    def pick(col):
        e = ids[:, col:col + 1]
        return jnp.sum(jnp.where(lane == e, base, 0.0), axis=-1, keepdims=True).astype(I32)

    d1 = ids[:, 2:3] + pick(0)
    d2 = ids[:, 3:4] + pick(1)
    o_ref[...] = jnp.where(lane == 0, d1, jnp.where(lane == 1, d2, 0))


def _dest_rows(ids, base, tm):
    t = ids.shape[0]
    return pl.pallas_call(
        _dest_kernel,
        out_shape=jax.ShapeDtypeStruct((t, LANES), I32),
        grid=(t // tm,),
        in_specs=[pl.BlockSpec((tm, LANES), lambda i: (i, 0)), pl.BlockSpec((1, 1, LANES), lambda i: (i, 0, 0))],
        out_specs=pl.BlockSpec((tm, LANES), lambda i: (i, 0)),
        compiler_params=_cparams(("parallel",)),
        name="moe_dest",
    )(ids, base)


DMA_UNROLL = 8


def _scatter_kernel(dest_ref, h_ref, rows_in_ref, rows_ref, idx_smem, idx_sem, sem, *, tm):
    del rows_in_ref
    i = pl.program_id(0)
    cp = pltpu.make_async_copy(dest_ref.at[i], idx_smem, idx_sem)
    cp.start()
    cp.wait()

    def row_copy(t, k):
        return pltpu.make_async_copy(h_ref.at[pl.ds(t, 1)], rows_ref.at[pl.ds(idx_smem[2 * t + k], 1)], sem)

    def issue(t, carry):
        row_copy(t, 0).start(priority=0)
        row_copy(t, 1).start(priority=1)
        return carry

    lax.fori_loop(0, tm, issue, 0, unroll=DMA_UNROLL)

    def drain(t, carry):
        row_copy(t, 0).wait()
        row_copy(t, 1).wait()
        return carry

    lax.fori_loop(0, tm, drain, 0, unroll=DMA_UNROLL)


def _scatter_rows(dest_tiles, h2d, n_rows, tm):
    t, d = h2d.shape
    nt = t // tm
    zeros = jnp.zeros((n_rows, d), h2d.dtype)
    return pl.pallas_call(
        functools.partial(_scatter_kernel, tm=tm),
        out_shape=jax.ShapeDtypeStruct((n_rows, d), h2d.dtype),
        grid=(nt,),
        in_specs=[pl.BlockSpec(memory_space=pl.ANY), pl.BlockSpec((tm, d), lambda i: (i, 0)),
                  pl.BlockSpec(memory_space=pl.ANY)],
        out_specs=pl.BlockSpec(memory_space=pl.ANY),
        scratch_shapes=[pltpu.SMEM((2 * tm,), I32), pltpu.SemaphoreType.DMA, pltpu.SemaphoreType.DMA],
        input_output_aliases={2: 0},
        compiler_params=_cparams(("arbitrary",)),
        name="moe_scatter",
    )(dest_tiles, h2d, zeros)


def _ffn_kernel(be_ref, nu_ref, rows_ref, w1_ref, w3_ref, w2_ref, y_ref, w13_scr, w2_scr):
    j = pl.program_id(0)
    used = j < nu_ref[0]
    new_expert = (j == 0) | (be_ref[j] != be_ref[jnp.maximum(j - 1, 0)])

    @pl.when(used & new_expert)
    def _():
        w13_scr[:, :D_EXPERT] = w1_ref[0].astype(BF16)
        w13_scr[:, D_EXPERT:] = w3_ref[0].astype(BF16)
        w2_scr[...] = w2_ref[0].astype(BF16)

    @pl.when(used)
    def _():
        h = _dot(rows_ref[...].astype(BF16), w13_scr[...])
        a = h[:, :D_EXPERT]
        act = a * jax.nn.sigmoid(a) * h[:, D_EXPERT:]
        y_ref[...] = _dot(act.astype(BF16), w2_scr[...])

    @pl.when(jnp.logical_not(used))
    def _():
        y_ref[...] = jnp.zeros_like(y_ref)


def _grouped_ffn(block_expert, n_used, rows, w1, w3, w2):
    n_rows, d = rows.shape
    nb = n_rows // MOE_BM
    grid_spec = pltpu.PrefetchScalarGridSpec(
        num_scalar_prefetch=2,
        grid=(nb,),
        in_specs=[
            pl.BlockSpec((MOE_BM, d), lambda j, be, nu: (j, 0)),
            pl.BlockSpec((1, d, D_EXPERT), lambda j, be, nu: (be[j], 0, 0)),
            pl.BlockSpec((1, d, D_EXPERT), lambda j, be, nu: (be[j], 0, 0)),
            pl.BlockSpec((1, D_EXPERT, d), lambda j, be, nu: (be[j], 0, 0)),
        ],
        out_specs=pl.BlockSpec((MOE_BM, d), lambda j, be, nu: (j, 0)),
        scratch_shapes=[pltpu.VMEM((d, 2 * D_EXPERT), BF16), pltpu.VMEM((D_EXPERT, d), BF16)],
    )
    return pl.pallas_call(
        _ffn_kernel,
        out_shape=jax.ShapeDtypeStruct((n_rows, d), F32),
        grid_spec=grid_spec,
        compiler_params=_cparams(("arbitrary",), VMEM_LIMIT),
        name="moe_grouped_ffn",
    )(block_expert, n_used, rows, w1, w3, w2)


def _combine_kernel(dest_ref, y_ref, gate_ref, x_ref, mod_ref, o_ref, idx_smem, ybuf, idx_sem, sem, *, tm):
    i = pl.program_id(0)
    cp = pltpu.make_async_copy(dest_ref.at[i], idx_smem, idx_sem)
    cp.start()
    cp.wait()

    def row_copy(t, k):
        return pltpu.make_async_copy(y_ref.at[pl.ds(idx_smem[2 * t + k], 1)], ybuf.at[k].at[pl.ds(t, 1)], sem)

    def issue(t, carry):
        row_copy(t, 0).start(priority=0)
        row_copy(t, 1).start(priority=1)
        return carry

    lax.fori_loop(0, tm, issue, 0, unroll=DMA_UNROLL)

    def drain(t, carry):
        row_copy(t, 0).wait()
        row_copy(t, 1).wait()
        return carry

    lax.fori_loop(0, tm, drain, 0, unroll=DMA_UNROLL)
    g = gate_ref[...]
    moe = g[:, 0:1] * ybuf[0] + g[:, 1:2] * ybuf[1]
    o_ref[...] = x_ref[...] + mod_ref[0][5:6] * moe


def _combine(dest_tiles, y, gates, x2d, mod_l, seq, tm):
    t, d = x2d.shape
    per_b = seq // tm
    return pl.pallas_call(
        functools.partial(_combine_kernel, tm=tm),
        out_shape=jax.ShapeDtypeStruct((t, d), F32),
        grid=(t // tm,),
        in_specs=[pl.BlockSpec(memory_space=pl.ANY), pl.BlockSpec(memory_space=pl.ANY),
                  pl.BlockSpec((tm, LANES), lambda i: (i, 0)), pl.BlockSpec((tm, d), lambda i: (i, 0)),
                  pl.BlockSpec((1, 6, d), lambda i: (i // per_b, 0, 0))],
        out_specs=pl.BlockSpec((tm, d), lambda i: (i, 0)),
        scratch_shapes=[pltpu.SMEM((2 * tm,), I32), pltpu.VMEM((2, tm, d), F32),
                        pltpu.SemaphoreType.DMA, pltpu.SemaphoreType.DMA],
        compiler_params=_cparams(("arbitrary",), VMEM_LIMIT),
        name="moe_combine",
    )(dest_tiles, y, gates, x2d, mod_l)


def _moe(x2d, mod_l, g_norm2, w_gr, b_gr, w_er, b_er, w1, w3, w2, seq):
    t, d = x2d.shape
    pad = LANES - N_EXPERT_GROUPS - N_EXPERTS
    w_router = jnp.concatenate([w_gr, w_er, jnp.zeros((d, pad), F32)], axis=1)
    b_router = jnp.concatenate([b_gr, b_er, jnp.zeros((pad,), F32)]).reshape(1, LANES)
    tm = _tile(seq, 512)
    nt = t // tm
    h2, ids, gates, cnt = _router(x2d, mod_l, g_norm2, w_router, b_router, seq)

    cnt = cnt[:, 0, :N_EXPERTS]
    total = jnp.sum(cnt, axis=0)
    padded = (total + MOE_BM - 1) // MOE_BM * MOE_BM
    ends = jnp.cumsum(padded)
    tile_off = jnp.cumsum(cnt, axis=0) - cnt
    base = (ends - padded)[None, :] + tile_off
    base = jnp.pad(base, ((0, 0), (0, LANES - N_EXPERTS))).reshape(nt, 1, LANES)
    nb = (2 * t) // MOE_BM + N_EXPERTS
    block_start = jnp.arange(nb, dtype=I32) * MOE_BM
    block_expert = jnp.minimum(jnp.sum(block_start[:, None] >= ends[None, :], axis=1), N_EXPERTS - 1).astype(I32)
    n_used = (ends[-1] // MOE_BM).astype(I32).reshape(1)

    dest = _dest_rows(ids, base.astype(I32), tm)
    dest_tiles = dest[:, :2].reshape(nt, 2 * tm)
    rows = _scatter_rows(dest_tiles, h2, nb * MOE_BM, tm)
    y = _grouped_ffn(block_expert, n_used, rows, w1, w3, w2)
    return _combine(dest_tiles, y, gates, x2d, mod_l, seq, tm)


def _seg_matrix(width, segments):
    idx = jnp.arange(width)
    m = jnp.zeros((width, width), F32)
    for start, length in segments:
        inside = (idx >= start) & (idx < start + length)
        m = m + jnp.where(inside[:, None] & inside[None, :], 1.0 / length, 0.0)
    return m.astype(BF16)


def _layout_w_in(w_in):
    depth, d, _ = w_in.shape
    sizes = [MLA_Q_RANK, MLA_KV_RANK, MLA_ROPE, N_DIL_COLS,
             (WIN_HEADS + 2 * WIN_KV_HEADS) * HEAD_DIM, 3 * D_MODEL]
    bounds = [sum(sizes[:k + 1]) for k in range(len(sizes) - 1)]
    c_q, c_kv, k_rope, dil, win, gate = jnp.split(w_in, bounds, axis=-1)
    win_q, win_kv = win[..., :WIN_HEADS * HEAD_DIM], win[..., WIN_HEADS * HEAD_DIM:]
    z = lambda w: jnp.zeros((depth, d, w), w_in.dtype)
    out = jnp.concatenate([gate, c_q, win_q, c_kv, win_kv, z(MLA_NOPE), k_rope,
                           z(N_COLS - OFF_KR - MLA_NOPE - MLA_ROPE)], axis=-1)
    assert out.shape[-1] == N_COLS
    return out.astype(BF16), dil.astype(BF16)


def _layout_mla(w_uq, w_ukv, g_q, g_k):
    qd = MLA_NOPE + MLA_ROPE
    wq = w_uq.reshape(MLA_Q_RANK, MLA_HEADS, qd)
    wq = jnp.pad(wq, ((0, 0), (0, 0), (0, SLOT - qd))).reshape(MLA_Q_RANK, MLA_HEADS * SLOT)
    wkv = w_ukv.reshape(MLA_KV_RANK, MLA_HEADS, MLA_NOPE + MLA_V)
    wk = jnp.pad(wkv[:, :, :MLA_NOPE], ((0, 0), (0, 0), (0, SLOT - MLA_NOPE))).reshape(MLA_KV_RANK, MLA_HEADS * SLOT)
    wv = wkv[:, :, MLA_NOPE:].reshape(MLA_KV_RANK, MLA_HEADS * MLA_V)
    scale = LOG2E * float(qd) ** -0.5
    gq_slot = (jnp.pad(g_q, (0, SLOT - qd)) * scale).reshape(1, SLOT)
    gkn_slot = jnp.pad(g_k[:MLA_NOPE], (0, SLOT - MLA_NOPE)).reshape(1, SLOT)
    gkr_slot = jnp.pad(g_k[MLA_NOPE:], (MLA_NOPE, SLOT - qd)).reshape(1, SLOT)
    return wq.astype(BF16), jnp.concatenate([wk, wv], axis=1).astype(BF16), gq_slot, gkn_slot, gkr_slot


def kernel(x, c, pos, w_ada, b_ada, g_norm1, w_in, g_cq, w_uq, g_ckv, w_ukv, g_q_mla, g_k_mla, g_q_dil, g_k_dil,
           g_q_win, g_k_win, sink_win, w_br_mla, w_br_dil, w_br_win, w_out, g_norm2, w_gr, b_gr, w_er, b_er,
           w1, w3, w2):
    batch, seq, d = x.shape
    depth = w_ada.shape[0]
    t = batch * seq
    half = MLA_ROPE // 2
    inv_freq = ROPE_THETA ** (-jnp.arange(half, dtype=F32) / half)
    invf_slot = jnp.concatenate([jnp.zeros((MLA_NOPE,), F32), inv_freq, inv_freq,
                                 jnp.zeros((SLOT - MLA_NOPE - MLA_ROPE,), F32)]).reshape(1, SLOT)
    cos_t, sin_t = _rope_tables(pos.reshape(t, 1), invf_slot)
    mod = _modulation(c, w_ada, b_ada)
    w_in_k, w_dil_k = _layout_w_in(w_in)
    mseg_slot = _seg_matrix(SLOT, ((0, MLA_NOPE), (MLA_NOPE, MLA_ROPE)))
    mseg128 = _seg_matrix(128, tuple((k * HEAD_DIM, HEAD_DIM) for k in range(2)))
    head_scale = LOG2E * float(HEAD_DIM) ** -0.5
    win_bq = min(128, seq)

    x2d = x.reshape(t, d)
    for l in range(depth):
        mod_l = mod[l].reshape(batch, 6, d)
        g1 = g_norm1[l].reshape(1, d)
        cols = _in_projection(x2d, mod_l, g1, w_in_k[l], seq, BF16, N_COLS // 2, "in_projection")
        dil_cols = _in_projection(x2d, mod_l, g1, w_dil_k[l], seq, F32, N_DIL_COLS // 3, "in_projection_dil")

        wuq, wukv, gq_slot, gkn_slot, gkr_slot = _layout_mla(w_uq[l], w_ukv[l], g_q_mla[l], g_k_mla[l])
        qm, km, vm = _mla_prep(cols, cos_t, sin_t, g_cq[l].reshape(1, -1), g_ckv[l].reshape(1, -1),
                               wuq, wukv, gq_slot, gkn_slot, gkr_slot, mseg_slot)
        o_mla = _mla_attention(qm, km, vm, batch, seq)

        gq_dil = (jnp.tile(g_q_dil[l], 2) * head_scale).reshape(1, -1)
        gk_dil = jnp.tile(g_k_dil[l], 2).reshape(1, -1)
        o_dil, lse_dil = [], []
        for gi in range(DIL_GROUPS):
            o_g, lse_g = _dilated_group(dil_cols, pos, gq_dil, gk_dil, mseg128, gi, batch, seq)
            o_dil.append(o_g)
            lse_dil.append(lse_g)

        gq_win = (jnp.tile(g_q_win[l], 2) * head_scale).reshape(1, -1)
        gk_win = jnp.tile(g_k_win[l], WIN_KV_HEADS).reshape(1, -1)
        sink_row = jnp.repeat(sink_win[l].astype(F32) * LOG2E, win_bq).reshape(1, WIN_HEADS * win_bq)
        o_win = _window_mixer(cols, pos, gq_win, gk_win, sink_row, mseg128, batch, seq)

        x2d = _merge(cols, o_mla, o_dil, lse_dil, o_win, x2d, mod_l, w_br_mla[l].astype(BF16),
                     w_br_dil[l].astype(BF16), w_br_win[l].astype(BF16), w_out[l].astype(BF16), seq)
        x2d = _moe(x2d, mod_l, g_norm2[l].reshape(1, d), w_gr[l], b_gr[l], w_er[l], b_er[l],
                   w1[l], w3[l], w2[l], seq)
    return x2d.reshape(batch, seq, d)
```

```python
import functools
import math

import jax
import jax.numpy as jnp
from jax import lax
from jax.experimental import pallas as pl
from jax.experimental.pallas import tpu as pltpu

F32 = jnp.float32
BF16 = jnp.bfloat16
I32 = jnp.int32

D_MODEL = 1024
HEAD_DIM = 64
NEG_INF = -1e30
EPS = 1e-6
LOG2E = math.log2(math.e)
LN2 = math.log(2.0)
MLA_HEADS = 8
MLA_Q_RANK = 512
MLA_KV_RANK = 256
MLA_NOPE = 64
MLA_ROPE = 32
MLA_V = 64
ROPE_THETA = 10000.0
DIL_PATTERNS = ((128, 1), (512, 4), (2048, 16))
DIL_GROUPS = 3
DIL_HPG = 4
DIL_HEADS = DIL_GROUPS * DIL_HPG
DIL_RADIUS = 64
WIN_HEADS = 8
WIN_KV_HEADS = 2
WIN_RADIUS = 128
N_EXPERT_GROUPS = 4
EXPERTS_PER_GROUP = 8
N_EXPERTS = N_EXPERT_GROUPS * EXPERTS_PER_GROUP
D_EXPERT = 384

LANES = 128
SLOT = 128
VMEM_LIMIT = 48 * 1024 * 1024

OFF_GATE = 0
OFF_CQ = 3072
OFF_WQ = 3584
OFF_CKV = 4096
OFF_WK = 4352
OFF_WV = 4480
OFF_KR = 4608
N_COLS = 4864
N_DIL_COLS = 3 * DIL_HEADS * HEAD_DIM

MOE_BM = 256
MOE_CHUNK = 8
MLA_TQ = 256


def _cparams(sem, vmem=None, flags=None):
    return pltpu.CompilerParams(dimension_semantics=sem, vmem_limit_bytes=vmem, flags=flags)


def _tile(n, pref):
    t = min(n, pref)
    assert n % t == 0, (n, pref)
    return t


def _dot(a, b):
    return jnp.dot(a, b, preferred_element_type=F32)


def _dot_nt(a, b):
    return lax.dot_general(a, b, (((1,), (1,)), ((), ())), preferred_element_type=F32)


def _dot_tn(a, b):
    return lax.dot_general(a, b, (((0,), (0,)), ((), ())), preferred_element_type=F32)


def _seg_mean_sq(x, mseg):
    x2 = x * x
    hi = x2.astype(BF16)
    lo = (x2 - hi.astype(F32)).astype(BF16)
    return _dot(hi, mseg) + _dot(lo, mseg)


def _seg_norm(x, mseg, gain):
    return x * lax.rsqrt(_seg_mean_sq(x, mseg) + EPS) * gain


def _rms(x, gain):
    ms = jnp.mean(x * x, axis=-1, keepdims=True)
    return x * lax.rsqrt(ms + EPS) * gain


def _mod_kernel(c_ref, w_ref, b_ref, o_ref):
    c = c_ref[...]
    cond = (c * jax.nn.sigmoid(c)).astype(BF16)
    o_ref[0] = _dot(cond, w_ref[0].astype(BF16)) + b_ref[0]


def _modulation(c, w_ada, b_ada):
    depth, d, n = w_ada.shape
    b = c.shape[0]
    tn = _tile(n, 1536)
    return pl.pallas_call(
        _mod_kernel,
        out_shape=jax.ShapeDtypeStruct((depth, b, n), F32),
        grid=(depth, n // tn),
        in_specs=[
            pl.BlockSpec((b, d), lambda l, j: (0, 0)),
            pl.BlockSpec((1, d, tn), lambda l, j: (l, 0, j)),
            pl.BlockSpec((1, 1, tn), lambda l, j: (l, 0, j)),
        ],
        out_specs=pl.BlockSpec((1, b, tn), lambda l, j: (l, 0, j)),
        compiler_params=_cparams(("parallel", "parallel")),
        name="adaln_mod",
    )(c, w_ada, b_ada.reshape(depth, 1, n))


def _rope_table_kernel(pos_ref, invf_ref, cos_ref, sin_ref):
    ang = pos_ref[...].astype(F32) * invf_ref[...]
    cos_ref[...] = jnp.cos(ang)
    sin_ref[...] = jnp.sin(ang)


def _rope_tables(pos_col, invf_slot):
    t = pos_col.shape[0]
    tm = _tile(t, 1024)
    return pl.pallas_call(
        _rope_table_kernel,
        out_shape=(jax.ShapeDtypeStruct((t, SLOT), F32), jax.ShapeDtypeStruct((t, SLOT), F32)),
        grid=(t // tm,),
        in_specs=[pl.BlockSpec((tm, 1), lambda i: (i, 0)), pl.BlockSpec((1, SLOT), lambda i: (0, 0))],
        out_specs=(pl.BlockSpec((tm, SLOT), lambda i: (i, 0)), pl.BlockSpec((tm, SLOT), lambda i: (i, 0))),
        compiler_params=_cparams(("parallel",)),
        name="rope_tables",
    )(pos_col, invf_slot)


def _inproj_kernel(x_ref, mod_ref, g_ref, w_ref, o_ref, h_scr):
    @pl.when(pl.program_id(1) == 0)
    def _():
        m = mod_ref[0]
        h = _rms(x_ref[...], g_ref[...]) * (1.0 + m[1:2]) + m[0:1]
        h_scr[...] = h.astype(BF16)

    o_ref[...] = _dot(h_scr[...], w_ref[...]).astype(o_ref.dtype)


def _in_projection(x2d, mod_l, g_norm, w_in_l, seq, out_dtype, tn, name):
    t, d = x2d.shape
    nc = w_in_l.shape[1]
    tm = _tile(seq, 1024)
    assert nc % tn == 0
    per_b = seq // tm
    return pl.pallas_call(
        _inproj_kernel,
        out_shape=jax.ShapeDtypeStruct((t, nc), out_dtype),
        grid=(t // tm, nc // tn),
        in_specs=[
            pl.BlockSpec((tm, d), lambda i, j: (i, 0)),
            pl.BlockSpec((1, 6, d), lambda i, j: (i // per_b, 0, 0)),
            pl.BlockSpec((1, d), lambda i, j: (0, 0)),
            pl.BlockSpec((d, tn), lambda i, j: (0, j)),
        ],
        out_specs=pl.BlockSpec((tm, tn), lambda i, j: (i, j)),
        scratch_shapes=[pltpu.VMEM((tm, d), BF16)],
        compiler_params=_cparams(("parallel", "arbitrary"), VMEM_LIMIT),
        name=name,
    )(x2d, mod_l, g_norm, w_in_l)


def _mla_prep_kernel(cq_ref, ckv_ref, kr_ref, cos_ref, sin_ref, gcq_ref, gckv_ref, wuq_ref, wukv_ref,
                     gq_ref, gkn_ref, gkr_ref, mseg_ref, q_out, k_out, v_out):
    cos = cos_ref[...]
    sin = sin_ref[...]
    lane = lax.broadcasted_iota(I32, (1, SLOT), 1)
    s_neg = jnp.where((lane >= 64) & (lane < 80), -sin, 0.0)
    s_pos = jnp.where((lane >= 80) & (lane < 96), sin, 0.0)
    mseg = mseg_ref[...]

    def rope(xn):
        return xn * cos + pltpu.roll(xn, SLOT - 16, 1) * s_neg + pltpu.roll(xn, 16, 1) * s_pos

    cqn = _rms(cq_ref[...].astype(F32), gcq_ref[...]).astype(BF16)
    q = _dot(cqn, wuq_ref[...])
    for h in range(MLA_HEADS):
        sl = slice(h * SLOT, (h + 1) * SLOT)
        q_out[:, sl] = rope(_seg_norm(q[:, sl], mseg, gq_ref[...])).astype(q_out.dtype)

    ckvn = _rms(ckv_ref[...].astype(F32), gckv_ref[...]).astype(BF16)
    kv = _dot(ckvn, wukv_ref[...])
    kr = rope(_seg_norm(kr_ref[...].astype(F32), mseg, gkr_ref[...]))
    for h in range(MLA_HEADS):
        sl = slice(h * SLOT, (h + 1) * SLOT)
        k_out[:, sl] = (_seg_norm(kv[:, sl], mseg, gkn_ref[...]) + kr).astype(k_out.dtype)
    v_out[...] = kv[:, MLA_HEADS * SLOT:].astype(v_out.dtype)


def _mla_prep(cols, cos_t, sin_t, gcq, gckv, wuq, wukv, gq_slot, gkn_slot, gkr_slot, mseg):
    t = cols.shape[0]
    tm = _tile(t, 512)
    hs = MLA_HEADS * SLOT
    full = lambda shape: pl.BlockSpec(shape, lambda i: (0,) * len(shape))
    return pl.pallas_call(
        _mla_prep_kernel,
        out_shape=(jax.ShapeDtypeStruct((t, hs), BF16), jax.ShapeDtypeStruct((t, hs), BF16),
                   jax.ShapeDtypeStruct((t, MLA_HEADS * MLA_V), BF16)),
        grid=(t // tm,),
        in_specs=[
            pl.BlockSpec((tm, MLA_Q_RANK), lambda i: (i, OFF_CQ // MLA_Q_RANK)),
            pl.BlockSpec((tm, MLA_KV_RANK), lambda i: (i, OFF_CKV // MLA_KV_RANK)),
            pl.BlockSpec((tm, SLOT), lambda i: (i, OFF_KR // SLOT)),
            pl.BlockSpec((tm, SLOT), lambda i: (i, 0)),
            pl.BlockSpec((tm, SLOT), lambda i: (i, 0)),
            full((1, MLA_Q_RANK)), full((1, MLA_KV_RANK)),
            full((MLA_Q_RANK, hs)), full((MLA_KV_RANK, hs + MLA_HEADS * MLA_V)),
            full((1, SLOT)), full((1, SLOT)), full((1, SLOT)), full((SLOT, SLOT)),
        ],
        out_specs=(pl.BlockSpec((tm, hs), lambda i: (i, 0)), pl.BlockSpec((tm, hs), lambda i: (i, 0)),
                   pl.BlockSpec((tm, MLA_HEADS * MLA_V), lambda i: (i, 0))),
        compiler_params=_cparams(("parallel",), VMEM_LIMIT),
        name="mla_prep",
    )(cols, cols, cols, cos_t, sin_t, gcq, gckv, wuq, wukv, gq_slot, gkn_slot, gkr_slot, mseg)


def _mla_attn_kernel(q_ref, k_ref, v_ref, o_ref):
    outs = []
    for a in range(2):
        sl = slice(a * SLOT, (a + 1) * SLOT)
        s = _dot_nt(q_ref[0, :, sl], k_ref[0, :, sl])
        m = jnp.max(s, axis=-1, keepdims=True)
        p = jnp.exp2(s - m)
        l = jnp.sum(p, axis=-1, keepdims=True)
        outs.append(_dot(p.astype(BF16), v_ref[0]) / l)
    lane = lax.broadcasted_iota(I32, (1, 2 * MLA_V), 1)
    o_ref[0] = jnp.where(lane < MLA_V, outs[0], outs[1]).astype(o_ref.dtype)


def _mla_attention(qm, km, vm, batch, seq):
    hs = MLA_HEADS * SLOT
    q3 = qm.reshape(batch, seq, hs)
    k3 = km.reshape(batch, seq, hs)
    v3 = vm.reshape(batch, seq, MLA_HEADS * MLA_V)
    tq = _tile(seq, MLA_TQ)
    out = pl.pallas_call(
        _mla_attn_kernel,
        out_shape=jax.ShapeDtypeStruct((batch, seq, MLA_HEADS * MLA_V), BF16),
        grid=(batch, MLA_HEADS // 2, seq // tq),
        in_specs=[
            pl.BlockSpec((1, tq, 2 * SLOT), lambda b, p, i: (b, i, p)),
            pl.BlockSpec((1, seq, 2 * SLOT), lambda b, p, i: (b, 0, p)),
            pl.BlockSpec((1, seq, 2 * MLA_V), lambda b, p, i: (b, 0, p)),
        ],
        out_specs=pl.BlockSpec((1, tq, 2 * MLA_V), lambda b, p, i: (b, i, p)),
        compiler_params=_cparams(("parallel", "parallel", "arbitrary"), VMEM_LIMIT),
        name="mla_attention",
    )(q3, k3, v3)
    return out.reshape(batch * seq, MLA_HEADS * MLA_V)


BAND_UNROLL = 8


def _window_start(i, bq, radius, n, kw):
    ws = jnp.clip(i * bq - radius, 0, n - kw)
    return pl.multiple_of(ws, 16)


def _band_valid_t(i, bq, ws, kw, radius, heads):
    keys = lax.broadcasted_iota(I32, (kw, heads * bq), 0)
    queries = lax.broadcasted_iota(I32, (kw, heads * bq), 1) & (bq - 1)
    rel = queries - keys + (i * bq - ws)
    return jnp.maximum(rel, -rel) <= radius


def _abs_dist(pa, pb):
    d = pa - pb
    return jnp.maximum(d, -d).astype(F32)


def _sub_positions(pos, batch, n, dil, bq):
    pos_sub = jnp.transpose(pos.reshape(batch, n, dil), (0, 2, 1))
    return pos_sub.reshape(batch, dil * n, 1), pos_sub.reshape(batch, dil * (n // bq), 1, bq)


def _dil_kernel(q_ref, k_ref, v_ref, pcol_ref, prow_ref, gq_ref, gk_ref, slope_ref, mseg_ref, o_ref, lse_ref,
                kn_scr, vn_scr, *, n, bq, kw, dil):
    half = pl.program_id(1)
    nblk = n // bq
    mseg = mseg_ref[...]
    gq = gq_ref[...]
    low_lane = lax.broadcasted_iota(I32, (1, 2 * HEAD_DIM), 1) < HEAD_DIM
    slope0 = slope_ref[pl.ds(2 * half, 1), 0:1]
    slope1 = slope_ref[pl.ds(2 * half + 1, 1), 0:1]

    def rows(start, size):
        return pl.ds(start, size) if dil == 1 else pl.ds(start, size, stride=dil)

    def normed_kv(r):
        kn = _seg_norm(k_ref[0, rows(r, n), :], mseg, gk_ref[...]).astype(BF16)
        return kn, v_ref[0, rows(r, n), :].astype(BF16)

    def block(r, i, ws, kwin, vwin):
        q_rows = rows(r + i * (bq * dil), bq)
        qn = _seg_norm(q_ref[0, q_rows, :], mseg, gq)
        qs = jnp.concatenate([jnp.where(low_lane, qn, 0.0), jnp.where(low_lane, 0.0, qn)], axis=0).astype(BF16)
        dist = _abs_dist(pcol_ref[0, pl.ds(r * n + ws, kw), :], prow_ref[0, r * nblk + i])
        bias = jnp.concatenate([slope0 * dist, slope1 * dist], axis=1)
        s = jnp.where(_band_valid_t(i, bq, ws, kw, DIL_RADIUS, 2), _dot_nt(kwin, qs) - bias, NEG_INF)
        m = jnp.max(s, axis=0, keepdims=True)
        p = jnp.exp2(s - m)
        l = jnp.sum(p, axis=0, keepdims=True)
        ot = _dot_tn(vwin, p.astype(BF16)) / l
        lse = (m + jnp.log2(l)) * LN2
        o_ref[0, q_rows, :] = jnp.concatenate([ot[:HEAD_DIM, :bq], ot[HEAD_DIM:, bq:]], axis=0).T
        lse_ref[0, q_rows, :] = jnp.concatenate([jnp.broadcast_to(lse[:, :bq], (HEAD_DIM, bq)),
                                                 jnp.broadcast_to(lse[:, bq:], (HEAD_DIM, bq))], axis=0).T

    def fill(r, carry):
        base = pl.multiple_of(r * n, 16)
        kn_scr[pl.ds(base, n), :], vn_scr[pl.ds(base, n), :] = normed_kv(r)
        return carry

    lax.fori_loop(0, dil, fill, 0)

    def query_block(b, carry):
        r = b >> (nblk.bit_length() - 1)
        i = b & (nblk - 1)
        ws = _window_start(i, bq, DIL_RADIUS, n, kw)
        start = pl.multiple_of(r * n + ws, 16)
        block(r, i, ws, kn_scr[pl.ds(start, kw), :], vn_scr[pl.ds(start, kw), :])
        return carry

    lax.fori_loop(0, dil * nblk, query_block, 0, unroll=min(dil * nblk, BAND_UNROLL))


def _dilated_group(dil_cols, pos, gq_row, gk_row, mseg, gi, batch, seq):
    _, dil = DIL_PATTERNS[gi]
    n = seq // dil
    bq = min(128, n)
    kw = min(bq + 2 * DIL_RADIUS, n)
    nblk = n // bq
    width = DIL_HPG * HEAD_DIM
    hw = 2 * HEAD_DIM
    cols3 = dil_cols.reshape(batch, seq, N_DIL_COLS)
    pcol, prow = _sub_positions(pos, batch, n, dil, bq)
    slopes = jnp.asarray([LOG2E * 2.0 ** (-8.0 * (gi * DIL_HPG + hh + 1) / DIL_HEADS) for hh in range(DIL_HPG)], F32)
    slope_rows = jnp.broadcast_to(jnp.pad(slopes, (0, 8 - DIL_HPG))[:, None], (8, LANES))
    kern = functools.partial(_dil_kernel, n=n, bq=bq, kw=kw, dil=dil)
    col = lambda which: (lambda b, h: (b, 0, (which * DIL_GROUPS + gi) * 2 + h))
    small = lambda shape: pl.BlockSpec(shape, lambda b, h: (0,) * len(shape))
    o, lse = pl.pallas_call(
        kern,
        out_shape=(jax.ShapeDtypeStruct((batch, seq, width), F32), jax.ShapeDtypeStruct((batch, seq, width), F32)),
        grid=(batch, 2),
        in_specs=[
            pl.BlockSpec((1, seq, hw), col(0)),
            pl.BlockSpec((1, seq, hw), col(1)),
            pl.BlockSpec((1, seq, hw), col(2)),
            pl.BlockSpec((1, seq, 1), lambda b, h: (b, 0, 0)),
            pl.BlockSpec((1, dil * nblk, 1, bq), lambda b, h: (b, 0, 0, 0)),
            small((1, hw)), small((1, hw)), small((8, LANES)), small((hw, hw)),
        ],
        out_specs=(pl.BlockSpec((1, seq, hw), lambda b, h: (b, 0, h)),
                   pl.BlockSpec((1, seq, hw), lambda b, h: (b, 0, h))),
        scratch_shapes=[pltpu.VMEM((seq, hw), BF16), pltpu.VMEM((seq, hw), BF16)],
        compiler_params=_cparams(("parallel", "parallel"), VMEM_LIMIT),
        name=f"dilated_group{gi}",
    )(cols3, cols3, cols3, pcol, prow, gq_row, gk_row, slope_rows, mseg)
    return o.reshape(batch * seq, width), lse.reshape(batch * seq, width)


def _win_kernel(q_ref, k_ref, v_ref, pcol_ref, prow_ref, gq_ref, gk_ref, sink_ref, mseg_ref, o_ref, kn_scr,
                *, n, bq, kw, slopes):
    i = pl.program_id(1)
    mseg = mseg_ref[...]

    @pl.when(i == 0)
    def _():
        kn_scr[...] = _seg_norm(k_ref[0].astype(F32), mseg, gk_ref[...]).astype(BF16)

    ws = _window_start(i, bq, WIN_RADIUS, n, kw)
    kwin = kn_scr[pl.ds(ws, kw), :]
    vwin = v_ref[0, pl.ds(ws, kw), :]
    dist = _abs_dist(pcol_ref[0, pl.ds(ws, kw), :], prow_ref[0, 0])
    low_lane = lax.broadcasted_iota(I32, (1, 2 * HEAD_DIM), 1) < HEAD_DIM
    rep = WIN_HEADS // WIN_KV_HEADS

    qn_pairs = [_seg_norm(q_ref[0, :, pair * 128:(pair + 1) * 128].astype(F32), mseg, gq_ref[...])
                for pair in range(WIN_HEADS // 2)]
    q_parts = []
    for h in range(WIN_HEADS):
        pair, upper = divmod(h, 2)
        qm = jnp.where(low_lane, 0.0, qn_pairs[pair]) if upper else jnp.where(low_lane, qn_pairs[pair], 0.0)
        if upper != h // rep:
            qm = pltpu.roll(qm, HEAD_DIM, 1)
        q_parts.append(qm.astype(BF16))
    qs = jnp.concatenate(q_parts, axis=0)
    bias = jnp.concatenate([slopes[h] * dist for h in range(WIN_HEADS)], axis=1)
    sink = sink_ref[...]
    s = jnp.where(_band_valid_t(i, bq, ws, kw, WIN_RADIUS, WIN_HEADS), _dot_nt(kwin, qs) - bias, NEG_INF)
    m = jnp.maximum(jnp.max(s, axis=0, keepdims=True), sink)
    e = jnp.exp2(s - m)
    den = jnp.sum(e, axis=0, keepdims=True) + jnp.exp2(sink - m)
    ot = _dot_tn(vwin, e.astype(BF16)) / den
    for pair in range(WIN_HEADS // 2):
        slabs = []
        for h in (2 * pair, 2 * pair + 1):
            g = h // rep
            slabs.append(ot[g * HEAD_DIM:(g + 1) * HEAD_DIM, h * bq:(h + 1) * bq])
        o_ref[0, :, pair * 128:(pair + 1) * 128] = jnp.concatenate(slabs, axis=0).T.astype(o_ref.dtype)


def _window_mixer(cols, pos, gq_row, gk_row, sink_row, mseg128, batch, seq):
    n = seq
    bq = min(128, n)
    kw = min(bq + 2 * WIN_RADIUS, n)
    nblk = n // bq
    qw = WIN_HEADS * HEAD_DIM
    kvw = WIN_KV_HEADS * HEAD_DIM
    cols_v = cols.reshape(batch, seq, N_COLS)
    pcol = pos.reshape(batch, seq, 1)
    prow = pos.reshape(batch, nblk, 1, bq)
    slopes = tuple(float(LOG2E * 2.0 ** (-8.0 * (h + 1) / WIN_HEADS)) for h in range(WIN_HEADS))
    kern = functools.partial(_win_kernel, n=n, bq=bq, kw=kw, slopes=slopes)
    small = lambda shape: pl.BlockSpec(shape, lambda b, i: (0,) * len(shape))
    o = pl.pallas_call(
        kern,
        out_shape=jax.ShapeDtypeStruct((batch, seq, qw), BF16),
        grid=(batch, nblk),
        in_specs=[
            pl.BlockSpec((1, bq, qw), lambda b, i: (b, i, OFF_WQ // qw)),
            pl.BlockSpec((1, n, kvw), lambda b, i: (b, 0, OFF_WK // kvw)),
            pl.BlockSpec((1, n, kvw), lambda b, i: (b, 0, OFF_WV // kvw)),
            pl.BlockSpec((1, n, 1), lambda b, i: (b, 0, 0)),
            pl.BlockSpec((1, 1, 1, bq), lambda b, i: (b, i, 0, 0)),
            small((1, kvw)), small((1, kvw)), small((1, WIN_HEADS * bq)), small((kvw, kvw)),
        ],
        out_specs=pl.BlockSpec((1, bq, qw), lambda b, i: (b, i, 0)),
        scratch_shapes=[pltpu.VMEM((n, kvw), BF16)],
        compiler_params=_cparams(("parallel", "arbitrary"), VMEM_LIMIT),
        name="window_mixer",
    )(cols_v, cols_v, cols_v, pcol, prow, gq_row, gk_row, sink_row, mseg128)
    return o.reshape(batch * seq, qw)


def _merge_kernel(gm_ref, gd_ref, gw_ref, om_ref, od0_ref, od1_ref, od2_ref, l0_ref, l1_ref, l2_ref, ow_ref,
                  x_ref, mod_ref, wm_ref, wd_ref, ww_ref, wo_ref, o_ref):
    l0, l1, l2 = l0_ref[...], l1_ref[...], l2_ref[...]
    m = jnp.maximum(jnp.maximum(l0, l1), l2)
    e0, e1, e2 = jnp.exp(l0 - m), jnp.exp(l1 - m), jnp.exp(l2 - m)
    od = (e0 * od0_ref[...] + e1 * od1_ref[...] + e2 * od2_ref[...]) / (e0 + e1 + e2)
    y = jax.nn.sigmoid(gm_ref[...].astype(F32)) * _dot(om_ref[...], wm_ref[...])
    y += jax.nn.sigmoid(gd_ref[...].astype(F32)) * _dot(od.astype(BF16), wd_ref[...])
    y += jax.nn.sigmoid(gw_ref[...].astype(F32)) * _dot(ow_ref[...], ww_ref[...])
    z = _dot(y.astype(BF16), wo_ref[...])
    o_ref[...] = x_ref[...] + mod_ref[0][2:3] * z


def _merge(cols, o_mla, o_dil, lse_dil, o_win, x2d, mod_l, wm, wd, ww, wo, seq):
    t, d = x2d.shape
    tm = _tile(seq, 512)
    per_b = seq // tm
    row = lambda w, j=0: pl.BlockSpec((tm, w), lambda i: (i, j))
    full = lambda a: pl.BlockSpec(a.shape, lambda i: (0, 0))
    dw = DIL_HPG * HEAD_DIM
    return pl.pallas_call(
        _merge_kernel,
        out_shape=jax.ShapeDtypeStruct((t, d), F32),
        grid=(t // tm,),
        in_specs=[row(d, 0), row(d, 1), row(d, 2), row(o_mla.shape[1]),
                  row(dw), row(dw), row(dw), row(dw), row(dw), row(dw), row(o_win.shape[1]),
                  row(d), pl.BlockSpec((1, 6, d), lambda i: (i // per_b, 0, 0)),
                  full(wm), full(wd), full(ww), full(wo)],
        out_specs=row(d),
        compiler_params=_cparams(("parallel",), VMEM_LIMIT),
        name="merge_out_proj",
    )(cols, cols, cols, o_mla, o_dil[0], o_dil[1], o_dil[2], lse_dil[0], lse_dil[1], lse_dil[2], o_win,
      x2d, mod_l, wm, wd, ww, wo)


def _pack_halves(y):
    n = y.shape[1] // 2
    hi = pltpu.bitcast(y[:, :n].astype(BF16).astype(F32), I32)
    lo = pltpu.bitcast(y[:, n:].astype(BF16).astype(F32), I32)
    return hi | lax.shift_right_logical(lo, jnp.int32(16))


def _unpack_halves(u):
    hi = pltpu.bitcast(u & jnp.int32(-65536), F32).astype(BF16)
    lo = pltpu.bitcast(u << 16, F32).astype(BF16)
    return jnp.concatenate([hi, lo], axis=1)


def _router_kernel(x_ref, mod_ref, g_ref, whi_ref, wlo_ref, br_ref, upper_ref, hs_ref, pos_ref, gate_ref, tab_ref,
                   *, lr):
    m = mod_ref[0]
    h = _rms(x_ref[...], g_ref[...]) * (1.0 + m[4:5]) + m[3:4]
    tm = h.shape[0]
    h_hi = h.astype(BF16)
    h_lo = (h - h_hi.astype(F32)).astype(BF16)
    logits = _dot(h_hi, whi_ref[...]) + (_dot(h_hi, wlo_ref[...]) + _dot(h_lo, whi_ref[...])) + br_ref[...]
    lane = lax.broadcasted_iota(I32, (tm, LANES), 1)
    lane_f = lane.astype(F32)

    def first_argmax(vals, mx):
        return jnp.min(jnp.where(vals == mx, lane_f, float(LANES)), axis=-1, keepdims=True).astype(I32)

    lg = jnp.where(lane < N_EXPERT_GROUPS, logits, NEG_INF)
    mg = jnp.max(lg, axis=-1, keepdims=True)
    g_w = 1.0 / jnp.sum(jnp.exp(lg - mg), axis=-1, keepdims=True)
    g_idx = first_argmax(lg, mg)
    eid = lane - N_EXPERT_GROUPS
    in_grp = (eid >= 0) & (eid < N_EXPERTS) & ((eid >> 3) == g_idx)
    le = jnp.where(in_grp, logits, NEG_INF)
    m1 = jnp.max(le, axis=-1, keepdims=True)
    i1 = first_argmax(le, m1)
    le2 = jnp.where(lane == i1, NEG_INF, le)
    m2 = jnp.max(le2, axis=-1, keepdims=True)
    i2 = first_argmax(le2, m2)
    r = jnp.exp(m2 - m1)
    gate1 = g_w / (1.0 + r)
    gate2 = g_w * r / (1.0 + r)
    e1 = i1 - N_EXPERT_GROUPS
    e2 = i2 - N_EXPERT_GROUPS
    hit1 = lane == e1
    hit2 = lane == e2
    onehot = jnp.where(hit1 | hit2, 1.0, 0.0)
    rows = lax.broadcasted_iota(I32, (tm, tm), 0)
    cols = lax.broadcasted_iota(I32, (tm, tm), 1)
    before = jnp.where(rows > cols, 1.0, 0.0).astype(BF16)
    rank = _dot(before, onehot.astype(BF16))
    cnt8 = (jnp.sum(onehot, axis=0, keepdims=True).astype(I32) + (MOE_CHUNK - 1)) & jnp.int32(-MOE_CHUNK)
    off8 = _dot(jnp.broadcast_to(cnt8.astype(F32), (8, LANES)).astype(BF16), upper_ref[...])[0:1]
    pos1 = jnp.sum(jnp.where(hit1, rank + off8, 0.0), axis=-1, keepdims=True).astype(I32)
    pos2 = jnp.sum(jnp.where(hit2, rank + off8, 0.0), axis=-1, keepdims=True).astype(I32)
    pos_lanes = jnp.where(lane == 0, pos1, jnp.where(lane == 1, pos2, 0))
    pos_rows = pos_lanes.astype(F32).T.astype(I32)
    local = lax.broadcasted_iota(I32, (lr, tm), 0)
    place = jnp.where((local == pos_rows[0:1]) | (local == pos_rows[1:2]), 1.0, 0.0).astype(BF16)
    hs_ref[0] = _pack_halves(_dot(place, h_hi))
    pos_ref[...] = pos_lanes
    gate_ref[...] = jnp.where(lane == 0, gate1, jnp.where(lane == 1, gate2, 0.0))
    sub = lax.broadcasted_iota(I32, (8, LANES), 0)
    tab_ref[0] = jnp.where(sub == 0, cnt8, jnp.where(sub == 1, off8.astype(I32), 0))


def _router(x2d, mod_l, g_norm, w_router, b_router, seq, tm, lr):
    t, d = x2d.shape
    per_b = seq // tm
    nt = t // tm
    row = lambda w: pl.BlockSpec((tm, w), lambda i: (i, 0))
    idx = jnp.arange(LANES)
    upper = jnp.where(idx[:, None] < idx[None, :], 1.0, 0.0).astype(BF16)
    w_hi = w_router.astype(BF16)
    w_lo = (w_router - w_hi.astype(F32)).astype(BF16)
    return pl.pallas_call(
        functools.partial(_router_kernel, lr=lr),
        out_shape=(jax.ShapeDtypeStruct((nt, lr, d // 2), I32), jax.ShapeDtypeStruct((t, LANES), I32),
                   jax.ShapeDtypeStruct((t, LANES), F32), jax.ShapeDtypeStruct((nt, 8, LANES), I32)),
        grid=(nt,),
        in_specs=[row(d), pl.BlockSpec((1, 6, d), lambda i: (i // per_b, 0, 0)),
                  pl.BlockSpec((1, d), lambda i: (0, 0)), pl.BlockSpec((d, LANES), lambda i: (0, 0)),
                  pl.BlockSpec((d, LANES), lambda i: (0, 0)),
                  pl.BlockSpec((1, LANES), lambda i: (0, 0)), pl.BlockSpec((LANES, LANES), lambda i: (0, 0))],
        out_specs=(pl.BlockSpec((1, lr, d // 2), lambda i: (i, 0, 0)), row(LANES), row(LANES),
                   pl.BlockSpec((1, 8, LANES), lambda i: (i, 0, 0))),
        compiler_params=_cparams(("parallel",), VMEM_LIMIT),
        name="moe_router",
    )(x2d, mod_l, g_norm, w_hi, w_lo, b_router, upper)


def _for_each_chunk(off_ref, n_ref, base_ref, tile, fn):
    def per_expert(e, carry):
        k = tile * N_EXPERTS + e
        off, base = off_ref[k], base_ref[k]

        def per_chunk(c, carry2):
            fn(pl.multiple_of(off + c * MOE_CHUNK, MOE_CHUNK), pl.multiple_of(base + c * MOE_CHUNK, MOE_CHUNK))
            return carry2

        return lax.fori_loop(0, n_ref[k], per_chunk, carry)

    lax.fori_loop(0, N_EXPERTS, per_expert, 0)


def _scatter_kernel(off_ref, n_ref, base_ref, hs_ref, rows_in_ref, rows_ref, sem):
    del rows_in_ref
    tile = pl.program_id(0)

    def chunk_copy(local, glob):
        return pltpu.make_async_copy(hs_ref.at[0, pl.ds(local, MOE_CHUNK)], rows_ref.at[pl.ds(glob, MOE_CHUNK)], sem)

    _for_each_chunk(off_ref, n_ref, base_ref, tile, lambda a, b: chunk_copy(a, b).start())
    _for_each_chunk(off_ref, n_ref, base_ref, tile, lambda a, b: chunk_copy(a, b).wait())


def _scatter_rows(off8, n8, base, hs, n_rows):
    nt, lr, w = hs.shape
    zeros = jnp.zeros((n_rows, w), hs.dtype)
    grid_spec = pltpu.PrefetchScalarGridSpec(
        num_scalar_prefetch=3,
        grid=(nt,),
        in_specs=[pl.BlockSpec((1, lr, w), lambda i, *_: (i, 0, 0)), pl.BlockSpec(memory_space=pl.ANY)],
        out_specs=pl.BlockSpec(memory_space=pl.ANY),
        scratch_shapes=[pltpu.SemaphoreType.DMA],
    )
    return pl.pallas_call(
        _scatter_kernel,
        out_shape=jax.ShapeDtypeStruct((n_rows, w), hs.dtype),
        grid_spec=grid_spec,
        input_output_aliases={4: 0},
        compiler_params=_cparams(("arbitrary",)),
        name="moe_scatter",
    )(off8, n8, base, hs, zeros)


def _ffn_kernel(be_ref, nu_ref, rows_ref, w1_ref, w3_ref, w2_ref, y_ref, w13_scr, w2_scr):
    j = pl.program_id(0)
    used = j < nu_ref[0]
    new_expert = (j == 0) | (be_ref[j] != be_ref[jnp.maximum(j - 1, 0)])

    @pl.when(used & new_expert)
    def _():
        w13_scr[:, :D_EXPERT] = w1_ref[0].astype(BF16)
        w13_scr[:, D_EXPERT:] = w3_ref[0].astype(BF16)
        w2_scr[...] = w2_ref[0].astype(BF16)

    @pl.when(used)
    def _():
        h = _dot(_unpack_halves(rows_ref[...]), w13_scr[...])
        a = h[:, :D_EXPERT]
        act = a * jax.nn.sigmoid(a) * h[:, D_EXPERT:]
        y_ref[...] = _pack_halves(_dot(act.astype(BF16), w2_scr[...]))

    @pl.when(jnp.logical_not(used))
    def _():
        y_ref[...] = jnp.zeros_like(y_ref)


def _grouped_ffn(block_expert, n_used, rows, w1, w3, w2):
    n_rows, w = rows.shape
    d = w1.shape[1]
    nb = n_rows // MOE_BM
    grid_spec = pltpu.PrefetchScalarGridSpec(
        num_scalar_prefetch=2,
        grid=(nb,),
        in_specs=[
            pl.BlockSpec((MOE_BM, w), lambda j, be, nu: (j, 0)),
            pl.BlockSpec((1, d, D_EXPERT), lambda j, be, nu: (be[j], 0, 0)),
            pl.BlockSpec((1, d, D_EXPERT), lambda j, be, nu: (be[j], 0, 0)),
            pl.BlockSpec((1, D_EXPERT, d), lambda j, be, nu: (be[j], 0, 0)),
        ],
        out_specs=pl.BlockSpec((MOE_BM, w), lambda j, be, nu: (j, 0)),
        scratch_shapes=[pltpu.VMEM((d, 2 * D_EXPERT), BF16), pltpu.VMEM((D_EXPERT, d), BF16)],
    )
    return pl.pallas_call(
        _ffn_kernel,
        out_shape=jax.ShapeDtypeStruct((n_rows, w), I32),
        grid_spec=grid_spec,
        compiler_params=_cparams(("arbitrary",), VMEM_LIMIT),
        name="moe_grouped_ffn",
    )(block_expert, n_used, rows, w1, w3, w2)


def _combine_kernel(off_ref, n_ref, base_ref, y_ref, pos_ref, gate_ref, x_ref, mod_ref, o_ref, ybuf, sem):
    tile = pl.program_id(0)
    ybuf[...] = jnp.zeros_like(ybuf)

    def chunk_copy(local, glob):
        return pltpu.make_async_copy(y_ref.at[pl.ds(glob, MOE_CHUNK)], ybuf.at[pl.ds(local, MOE_CHUNK)], sem)

    _for_each_chunk(off_ref, n_ref, base_ref, tile, lambda a, b: chunk_copy(a, b).start())
    _for_each_chunk(off_ref, n_ref, base_ref, tile, lambda a, b: chunk_copy(a, b).wait())
    tm = x_ref.shape[0]
    pos, g = pos_ref[...], gate_ref[...]
    local = lax.broadcasted_iota(I32, (tm, ybuf.shape[0]), 1)
    pick = (jnp.where(local == pos[:, 0:1], g[:, 0:1], 0.0)
            + jnp.where(local == pos[:, 1:2], g[:, 1:2], 0.0)).astype(BF16)
    moe = _dot(pick, _unpack_halves(ybuf[...]))
    o_ref[...] = x_ref[...] + mod_ref[0][5:6] * moe


def _combine(off8, n8, base, y, pos, gates, x2d, mod_l, seq, tm, lr):
    t, d = x2d.shape
    per_b = seq // tm
    grid_spec = pltpu.PrefetchScalarGridSpec(
        num_scalar_prefetch=3,
        grid=(t // tm,),
        in_specs=[pl.BlockSpec(memory_space=pl.ANY),
                  pl.BlockSpec((tm, LANES), lambda i, *_: (i, 0)), pl.BlockSpec((tm, LANES), lambda i, *_: (i, 0)),
                  pl.BlockSpec((tm, d), lambda i, *_: (i, 0)),
                  pl.BlockSpec((1, 6, d), lambda i, *_: (i // per_b, 0, 0))],
        out_specs=pl.BlockSpec((tm, d), lambda i, *_: (i, 0)),
        scratch_shapes=[pltpu.VMEM((lr, y.shape[1]), I32), pltpu.SemaphoreType.DMA],
    )
    return pl.pallas_call(
        _combine_kernel,
        out_shape=jax.ShapeDtypeStruct((t, d), F32),
        grid_spec=grid_spec,
        compiler_params=_cparams(("arbitrary",), VMEM_LIMIT),
        name="moe_combine",
    )(off8, n8, base, y, pos, gates, x2d, mod_l)


def _moe(x2d, mod_l, g_norm2, w_gr, b_gr, w_er, b_er, w1, w3, w2, seq):
    t, d = x2d.shape
    pad = LANES - N_EXPERT_GROUPS - N_EXPERTS
    w_router = jnp.concatenate([w_gr, w_er, jnp.zeros((d, pad), F32)], axis=1)
    b_router = jnp.concatenate([b_gr, b_er, jnp.zeros((pad,), F32)]).reshape(1, LANES)
    tm = _tile(seq, 512)
    nt = t // tm
    lr = 2 * tm + N_EXPERTS * MOE_CHUNK
    hs, pos, gates, tab = _router(x2d, mod_l, g_norm2, w_router, b_router, seq, tm, lr)

    cnt8 = tab[:, 0, :N_EXPERTS]
    off8 = tab[:, 1, :N_EXPERTS]
    total = jnp.sum(cnt8, axis=0)
    padded = (total + MOE_BM - 1) // MOE_BM * MOE_BM
    ends = jnp.cumsum(padded)
    base = (ends - padded)[None, :] + jnp.cumsum(cnt8, axis=0) - cnt8
    nb = (2 * t + nt * N_EXPERTS * MOE_CHUNK) // MOE_BM + N_EXPERTS
    block_start = jnp.arange(nb, dtype=I32) * MOE_BM
    block_expert = jnp.minimum(jnp.sum(block_start[:, None] >= ends[None, :], axis=1), N_EXPERTS - 1).astype(I32)
    n_used = (ends[-1] // MOE_BM).astype(I32).reshape(1)
    flat = lambda a: a.reshape(-1).astype(I32)
    off8, n8, base = flat(off8), flat(cnt8 // MOE_CHUNK), flat(base)

    rows = _scatter_rows(off8, n8, base, hs, nb * MOE_BM)
    y = _grouped_ffn(block_expert, n_used, rows, w1, w3, w2)
    return _combine(off8, n8, base, y, pos, gates, x2d, mod_l, seq, tm, lr)


def _seg_matrix(width, segments):
    idx = jnp.arange(width)
    m = jnp.zeros((width, width), F32)
    for start, length in segments:
        inside = (idx >= start) & (idx < start + length)
        m = m + jnp.where(inside[:, None] & inside[None, :], 1.0 / length, 0.0)
    return m.astype(BF16)


def _layout_w_in(w_in):
    depth, d, _ = w_in.shape
    sizes = [MLA_Q_RANK, MLA_KV_RANK, MLA_ROPE, N_DIL_COLS,
             (WIN_HEADS + 2 * WIN_KV_HEADS) * HEAD_DIM, 3 * D_MODEL]
    bounds = [sum(sizes[:k + 1]) for k in range(len(sizes) - 1)]
    c_q, c_kv, k_rope, dil, win, gate = jnp.split(w_in, bounds, axis=-1)
    win_q, win_kv = win[..., :WIN_HEADS * HEAD_DIM], win[..., WIN_HEADS * HEAD_DIM:]
    z = lambda w: jnp.zeros((depth, d, w), w_in.dtype)
    out = jnp.concatenate([gate, c_q, win_q, c_kv, win_kv, z(MLA_NOPE), k_rope,
                           z(N_COLS - OFF_KR - MLA_NOPE - MLA_ROPE)], axis=-1)
    assert out.shape[-1] == N_COLS
    return out.astype(BF16), dil.astype(BF16)


def _layout_mla(w_uq, w_ukv, g_q, g_k):
    qd = MLA_NOPE + MLA_ROPE
    wq = w_uq.reshape(MLA_Q_RANK, MLA_HEADS, qd)
    wq = jnp.pad(wq, ((0, 0), (0, 0), (0, SLOT - qd))).reshape(MLA_Q_RANK, MLA_HEADS * SLOT)
    wkv = w_ukv.reshape(MLA_KV_RANK, MLA_HEADS, MLA_NOPE + MLA_V)
    wk = jnp.pad(wkv[:, :, :MLA_NOPE], ((0, 0), (0, 0), (0, SLOT - MLA_NOPE))).reshape(MLA_KV_RANK, MLA_HEADS * SLOT)
    wv = wkv[:, :, MLA_NOPE:].reshape(MLA_KV_RANK, MLA_HEADS * MLA_V)
    scale = LOG2E * float(qd) ** -0.5
    gq_slot = (jnp.pad(g_q, (0, SLOT - qd)) * scale).reshape(1, SLOT)
    gkn_slot = jnp.pad(g_k[:MLA_NOPE], (0, SLOT - MLA_NOPE)).reshape(1, SLOT)
    gkr_slot = jnp.pad(g_k[MLA_NOPE:], (MLA_NOPE, SLOT - qd)).reshape(1, SLOT)
    return wq.astype(BF16), jnp.concatenate([wk, wv], axis=1).astype(BF16), gq_slot, gkn_slot, gkr_slot


def kernel(x, c, pos, w_ada, b_ada, g_norm1, w_in, g_cq, w_uq, g_ckv, w_ukv, g_q_mla, g_k_mla, g_q_dil, g_k_dil,
           g_q_win, g_k_win, sink_win, w_br_mla, w_br_dil, w_br_win, w_out, g_norm2, w_gr, b_gr, w_er, b_er,
           w1, w3, w2):
    batch, seq, d = x.shape
    depth = w_ada.shape[0]
    t = batch * seq
    half = MLA_ROPE // 2
    inv_freq = ROPE_THETA ** (-jnp.arange(half, dtype=F32) / half)
    invf_slot = jnp.concatenate([jnp.zeros((MLA_NOPE,), F32), inv_freq, inv_freq,
                                 jnp.zeros((SLOT - MLA_NOPE - MLA_ROPE,), F32)]).reshape(1, SLOT)
    cos_t, sin_t = _rope_tables(pos.reshape(t, 1), invf_slot)
    mod = _modulation(c, w_ada, b_ada)
    w_in_k, w_dil_k = _layout_w_in(w_in)
    mseg_slot = _seg_matrix(SLOT, ((0, MLA_NOPE), (MLA_NOPE, MLA_ROPE)))
    mseg128 = _seg_matrix(128, tuple((k * HEAD_DIM, HEAD_DIM) for k in range(2)))
    head_scale = LOG2E * float(HEAD_DIM) ** -0.5
    win_bq = min(128, seq)

    x2d = x.reshape(t, d)
    for l in range(depth):
        mod_l = mod[l].reshape(batch, 6, d)
        g1 = g_norm1[l].reshape(1, d)
        cols = _in_projection(x2d, mod_l, g1, w_in_k[l], seq, BF16, N_COLS // 2, "in_projection")
        dil_cols = _in_projection(x2d, mod_l, g1, w_dil_k[l], seq, F32, N_DIL_COLS // 3, "in_projection_dil")

        wuq, wukv, gq_slot, gkn_slot, gkr_slot = _layout_mla(w_uq[l], w_ukv[l], g_q_mla[l], g_k_mla[l])
        qm, km, vm = _mla_prep(cols, cos_t, sin_t, g_cq[l].reshape(1, -1), g_ckv[l].reshape(1, -1),
                               wuq, wukv, gq_slot, gkn_slot, gkr_slot, mseg_slot)
        o_mla = _mla_attention(qm, km, vm, batch, seq)

        gq_dil = (jnp.tile(g_q_dil[l], 2) * head_scale).reshape(1, -1)
        gk_dil = jnp.tile(g_k_dil[l], 2).reshape(1, -1)
        o_dil, lse_dil = [], []
        for gi in range(DIL_GROUPS):
            o_g, lse_g = _dilated_group(dil_cols, pos, gq_dil, gk_dil, mseg128, gi, batch, seq)
            o_dil.append(o_g)
            lse_dil.append(lse_g)

        gq_win = (jnp.tile(g_q_win[l], 2) * head_scale).reshape(1, -1)
        gk_win = jnp.tile(g_k_win[l], WIN_KV_HEADS).reshape(1, -1)
        sink_row = jnp.repeat(sink_win[l].astype(F32) * LOG2E, win_bq).reshape(1, WIN_HEADS * win_bq)
        o_win = _window_mixer(cols, pos, gq_win, gk_win, sink_row, mseg128, batch, seq)

        x2d = _merge(cols, o_mla, o_dil, lse_dil, o_win, x2d, mod_l, w_br_mla[l].astype(BF16),
                     w_br_dil[l].astype(BF16), w_br_win[l].astype(BF16), w_out[l].astype(BF16), seq)
        x2d = _moe(x2d, mod_l, g_norm2[l].reshape(1, d), w_gr[l], b_gr[l], w_er[l], b_er[l],
                   w1[l], w3[l], w2[l], seq)
    return x2d.reshape(batch, seq, d)
```

```python
import functools
import math

import jax
import jax.numpy as jnp
from jax import lax
from jax.experimental import pallas as pl
from jax.experimental.pallas import tpu as pltpu

F32 = jnp.float32
BF16 = jnp.bfloat16
I32 = jnp.int32

D_MODEL = 1024
HEAD_DIM = 64
NEG_INF = -1e30
EPS = 1e-6
LOG2E = math.log2(math.e)
LN2 = math.log(2.0)
MLA_HEADS = 8
MLA_Q_RANK = 512
MLA_KV_RANK = 256
MLA_NOPE = 64
MLA_ROPE = 32
MLA_V = 64
ROPE_THETA = 10000.0
DIL_PATTERNS = ((128, 1), (512, 4), (2048, 16))
DIL_GROUPS = 3
DIL_HPG = 4
DIL_HEADS = DIL_GROUPS * DIL_HPG
DIL_RADIUS = 64
WIN_HEADS = 8
WIN_KV_HEADS = 2
WIN_RADIUS = 128
N_EXPERT_GROUPS = 4
EXPERTS_PER_GROUP = 8
N_EXPERTS = N_EXPERT_GROUPS * EXPERTS_PER_GROUP
D_EXPERT = 384

LANES = 128
SLOT = 128
VMEM_LIMIT = 48 * 1024 * 1024

OFF_GATE = 0
OFF_CQ = 3072
OFF_WQ = 3584
OFF_CKV = 4096
OFF_WK = 4352
OFF_WV = 4480
OFF_KR = 4608
N_COLS = 4864
N_DIL_COLS = 3 * DIL_HEADS * HEAD_DIM

MOE_BM = 512
MOE_CHUNK = 8
MLA_TQ = 256
MLA_HPS = 8


def _cparams(sem, vmem=None, flags=None):
    return pltpu.CompilerParams(dimension_semantics=sem, vmem_limit_bytes=vmem, flags=flags)


def _tile(n, pref):
    t = min(n, pref)
    assert n % t == 0, (n, pref)
    return t


def _dot(a, b):
    return jnp.dot(a, b, preferred_element_type=F32)


def _dot_nt(a, b):
    return lax.dot_general(a, b, (((1,), (1,)), ((), ())), preferred_element_type=F32)


def _dot_tn(a, b):
    return lax.dot_general(a, b, (((0,), (0,)), ((), ())), preferred_element_type=F32)


def _seg_mean_sq(x, mseg):
    x2 = x * x
    hi = x2.astype(BF16)
    lo = (x2 - hi.astype(F32)).astype(BF16)
    return _dot(hi, mseg) + _dot(lo, mseg)


def _seg_norm(x, mseg, gain):
    return x * lax.rsqrt(_seg_mean_sq(x, mseg) + EPS) * gain


def _rms(x, gain):
    ms = jnp.mean(x * x, axis=-1, keepdims=True)
    return x * lax.rsqrt(ms + EPS) * gain


def _mod_kernel(c_ref, w_ref, b_ref, o_ref):
    c = c_ref[...]
    cond = (c * jax.nn.sigmoid(c)).astype(BF16)
    o_ref[0] = _dot(cond, w_ref[0].astype(BF16)) + b_ref[0]


def _modulation(c, w_ada, b_ada):
    depth, d, n = w_ada.shape
    b = c.shape[0]
    tn = _tile(n, 1536)
    return pl.pallas_call(
        _mod_kernel,
        out_shape=jax.ShapeDtypeStruct((depth, b, n), F32),
        grid=(depth, n // tn),
        in_specs=[
            pl.BlockSpec((b, d), lambda l, j: (0, 0)),
            pl.BlockSpec((1, d, tn), lambda l, j: (l, 0, j)),
            pl.BlockSpec((1, 1, tn), lambda l, j: (l, 0, j)),
        ],
        out_specs=pl.BlockSpec((1, b, tn), lambda l, j: (l, 0, j)),
        compiler_params=_cparams(("parallel", "parallel")),
        name="adaln_mod",
    )(c, w_ada, b_ada.reshape(depth, 1, n))


def _rope_table_kernel(pos_ref, invf_ref, cos_ref, sin_ref):
    ang = pos_ref[...].astype(F32) * invf_ref[...]
    cos_ref[...] = jnp.cos(ang)
    sin_ref[...] = jnp.sin(ang)


def _rope_tables(pos_col, invf_slot):
    t = pos_col.shape[0]
    tm = _tile(t, 1024)
    return pl.pallas_call(
        _rope_table_kernel,
        out_shape=(jax.ShapeDtypeStruct((t, SLOT), F32), jax.ShapeDtypeStruct((t, SLOT), F32)),
        grid=(t // tm,),
        in_specs=[pl.BlockSpec((tm, 1), lambda i: (i, 0)), pl.BlockSpec((1, SLOT), lambda i: (0, 0))],
        out_specs=(pl.BlockSpec((tm, SLOT), lambda i: (i, 0)), pl.BlockSpec((tm, SLOT), lambda i: (i, 0))),
        compiler_params=_cparams(("parallel",)),
        name="rope_tables",
    )(pos_col, invf_slot)


def _inproj_kernel(x_ref, mod_ref, g_ref, w_ref, o_ref, h_scr):
    @pl.when(pl.program_id(1) == 0)
    def _():
        m = mod_ref[0]
        h = _rms(x_ref[...], g_ref[...]) * (1.0 + m[1:2]) + m[0:1]
        h_scr[...] = h.astype(BF16)

    o_ref[...] = _dot(h_scr[...], w_ref[...]).astype(o_ref.dtype)


def _in_projection(x2d, mod_l, g_norm, w_in_l, seq, out_dtype, tn, name):
    t, d = x2d.shape
    nc = w_in_l.shape[1]
    tm = _tile(seq, 1024)
    assert nc % tn == 0
    per_b = seq // tm
    return pl.pallas_call(
        _inproj_kernel,
        out_shape=jax.ShapeDtypeStruct((t, nc), out_dtype),
        grid=(t // tm, nc // tn),
        in_specs=[
            pl.BlockSpec((tm, d), lambda i, j: (i, 0)),
            pl.BlockSpec((1, 6, d), lambda i, j: (i // per_b, 0, 0)),
            pl.BlockSpec((1, d), lambda i, j: (0, 0)),
            pl.BlockSpec((d, tn), lambda i, j: (0, j)),
        ],
        out_specs=pl.BlockSpec((tm, tn), lambda i, j: (i, j)),
        scratch_shapes=[pltpu.VMEM((tm, d), BF16)],
        compiler_params=_cparams(("parallel", "arbitrary"), VMEM_LIMIT),
        name=name,
    )(x2d, mod_l, g_norm, w_in_l)


def _mla_prep_kernel(cq_ref, ckv_ref, kr_ref, cos_ref, sin_ref, gcq_ref, gckv_ref, wuq_ref, wukv_ref,
                     gq_ref, gkn_ref, gkr_ref, mseg_ref, q_out, k_out, v_out):
    cos = cos_ref[...]
    sin = sin_ref[...]
    lane = lax.broadcasted_iota(I32, (1, SLOT), 1)
    s_neg = jnp.where((lane >= 64) & (lane < 80), -sin, 0.0)
    s_pos = jnp.where((lane >= 80) & (lane < 96), sin, 0.0)
    mseg = mseg_ref[...]

    def rope(xn):
        return xn * cos + pltpu.roll(xn, SLOT - 16, 1) * s_neg + pltpu.roll(xn, 16, 1) * s_pos

    cqn = _rms(cq_ref[...].astype(F32), gcq_ref[...]).astype(BF16)
    q = _dot(cqn, wuq_ref[...])
    for h in range(MLA_HEADS):
        sl = slice(h * SLOT, (h + 1) * SLOT)
        q_out[:, sl] = rope(_seg_norm(q[:, sl], mseg, gq_ref[...])).astype(q_out.dtype)

    ckvn = _rms(ckv_ref[...].astype(F32), gckv_ref[...]).astype(BF16)
    kv = _dot(ckvn, wukv_ref[...])
    kr = rope(_seg_norm(kr_ref[...].astype(F32), mseg, gkr_ref[...]))
    for h in range(MLA_HEADS):
        sl = slice(h * SLOT, (h + 1) * SLOT)
        k_out[:, sl] = (_seg_norm(kv[:, sl], mseg, gkn_ref[...]) + kr).astype(k_out.dtype)
    v_out[...] = kv[:, MLA_HEADS * SLOT:].astype(v_out.dtype)


def _mla_prep(cols, cos_t, sin_t, gcq, gckv, wuq, wukv, gq_slot, gkn_slot, gkr_slot, mseg):
    t = cols.shape[0]
    tm = _tile(t, 512)
    hs = MLA_HEADS * SLOT
    full = lambda shape: pl.BlockSpec(shape, lambda i: (0,) * len(shape))
    return pl.pallas_call(
        _mla_prep_kernel,
        out_shape=(jax.ShapeDtypeStruct((t, hs), BF16), jax.ShapeDtypeStruct((t, hs), BF16),
                   jax.ShapeDtypeStruct((t, MLA_HEADS * MLA_V), BF16)),
        grid=(t // tm,),
        in_specs=[
            pl.BlockSpec((tm, MLA_Q_RANK), lambda i: (i, OFF_CQ // MLA_Q_RANK)),
            pl.BlockSpec((tm, MLA_KV_RANK), lambda i: (i, OFF_CKV // MLA_KV_RANK)),
            pl.BlockSpec((tm, SLOT), lambda i: (i, OFF_KR // SLOT)),
            pl.BlockSpec((tm, SLOT), lambda i: (i, 0)),
            pl.BlockSpec((tm, SLOT), lambda i: (i, 0)),
            full((1, MLA_Q_RANK)), full((1, MLA_KV_RANK)),
            full((MLA_Q_RANK, hs)), full((MLA_KV_RANK, hs + MLA_HEADS * MLA_V)),
            full((1, SLOT)), full((1, SLOT)), full((1, SLOT)), full((SLOT, SLOT)),
        ],
        out_specs=(pl.BlockSpec((tm, hs), lambda i: (i, 0)), pl.BlockSpec((tm, hs), lambda i: (i, 0)),
                   pl.BlockSpec((tm, MLA_HEADS * MLA_V), lambda i: (i, 0))),
        compiler_params=_cparams(("parallel",), VMEM_LIMIT),
        name="mla_prep",
    )(cols, cols, cols, cos_t, sin_t, gcq, gckv, wuq, wukv, gq_slot, gkn_slot, gkr_slot, mseg)


def _mla_attn_kernel(q_ref, k_ref, v_ref, o_ref):
    lane = lax.broadcasted_iota(I32, (1, 2 * MLA_V), 1)
    for pair in range(MLA_HPS // 2):
        outs = []
        vp = v_ref[0, :, pair * 2 * MLA_V:(pair + 1) * 2 * MLA_V]
        for a in range(2):
            sl = slice((2 * pair + a) * SLOT, (2 * pair + a + 1) * SLOT)
            s = _dot_nt(q_ref[0, :, sl], k_ref[0, :, sl])
            m = jnp.max(s, axis=-1, keepdims=True)
            p = jnp.exp2(s - m)
            l = jnp.sum(p, axis=-1, keepdims=True)
            outs.append(_dot(p.astype(BF16), vp) / l)
        o_ref[0, :, pair * 2 * MLA_V:(pair + 1) * 2 * MLA_V] = jnp.where(lane < MLA_V, outs[0], outs[1]).astype(o_ref.dtype)


def _mla_attention(qm, km, vm, batch, seq):
    hs = MLA_HEADS * SLOT
    q3 = qm.reshape(batch, seq, hs)
    k3 = km.reshape(batch, seq, hs)
    v3 = vm.reshape(batch, seq, MLA_HEADS * MLA_V)
    tq = _tile(seq, MLA_TQ)
    out = pl.pallas_call(
        _mla_attn_kernel,
        out_shape=jax.ShapeDtypeStruct((batch, seq, MLA_HEADS * MLA_V), BF16),
        grid=(batch, MLA_HEADS // MLA_HPS, seq // tq),
        in_specs=[
            pl.BlockSpec((1, tq, MLA_HPS * SLOT), lambda b, p, i: (b, i, p)),
            pl.BlockSpec((1, seq, MLA_HPS * SLOT), lambda b, p, i: (b, 0, p)),
            pl.BlockSpec((1, seq, MLA_HPS * MLA_V), lambda b, p, i: (b, 0, p)),
        ],
        out_specs=pl.BlockSpec((1, tq, MLA_HPS * MLA_V), lambda b, p, i: (b, i, p)),
        compiler_params=_cparams(("parallel", "parallel", "arbitrary"), VMEM_LIMIT),
        name="mla_attention",
    )(q3, k3, v3)
    return out.reshape(batch * seq, MLA_HEADS * MLA_V)


BAND_UNROLL = 16


def _window_start(i, bq, radius, n, kw):
    ws = jnp.clip(i * bq - radius, 0, n - kw)
    return pl.multiple_of(ws, 16)


MASKED_DIST = 1e30


def _masked_dist_t(i, bq, ws, kw, radius, pos_keys, pos_queries):
    keys = lax.broadcasted_iota(I32, (kw, bq), 0)
    queries = lax.broadcasted_iota(I32, (kw, bq), 1)
    rel = queries - keys + (i * bq - ws)
    d = pos_keys - pos_queries
    return jnp.where(jnp.maximum(rel, -rel) <= radius, jnp.maximum(d, -d).astype(F32), MASKED_DIST)


def _sub_positions(pos, batch, n, dil, bq):
    pos_sub = jnp.transpose(pos.reshape(batch, n, dil), (0, 2, 1))
    return pos_sub.reshape(batch, dil * n, 1), pos_sub.reshape(batch, dil * (n // bq), 1, bq)


def _dil_kernel(q_ref, k_ref, v_ref, pcol_ref, prow_ref, gq_ref, gk_ref, slope_ref, mseg_ref, o_ref, lse_ref,
                kn_scr, vn_scr, *, n, bq, kw, dil):
    half = pl.program_id(1)
    nblk = n // bq
    mseg = mseg_ref[...]
    gq = gq_ref[...]
    low_lane = lax.broadcasted_iota(I32, (1, 2 * HEAD_DIM), 1) < HEAD_DIM
    slope0 = slope_ref[pl.ds(2 * half, 1), 0:1]
    slope1 = slope_ref[pl.ds(2 * half + 1, 1), 0:1]

    def rows(start, size):
        return pl.ds(start, size) if dil == 1 else pl.ds(start, size, stride=dil)

    def normed_kv(r):
        kn = _seg_norm(k_ref[0, rows(r, n), :], mseg, gk_ref[...]).astype(BF16)
        return kn, v_ref[0, rows(r, n), :].astype(BF16)

    def block(r, i, ws, kwin, vwin):
        q_rows = rows(r + i * (bq * dil), bq)
        qn = _seg_norm(q_ref[0, q_rows, :], mseg, gq)
        qs = jnp.concatenate([jnp.where(low_lane, qn, 0.0), jnp.where(low_lane, 0.0, qn)], axis=0).astype(BF16)
        dist = _masked_dist_t(i, bq, ws, kw, DIL_RADIUS,
                              pcol_ref[0, pl.ds(r * n + ws, kw), :], prow_ref[0, r * nblk + i])
        s = _dot_nt(kwin, qs) - jnp.concatenate([slope0 * dist, slope1 * dist], axis=1)
        m = jnp.max(s, axis=0, keepdims=True)
        p = jnp.exp2(s - m)
        l = jnp.sum(p, axis=0, keepdims=True)
        ot = _dot_tn(vwin, p.astype(BF16)) / l
        lse = (m + jnp.log2(l)) * LN2
        o_ref[0, q_rows, :] = jnp.concatenate([ot[:HEAD_DIM, :bq], ot[HEAD_DIM:, bq:]], axis=0).T
        lse_ref[0, q_rows, :] = jnp.concatenate([jnp.broadcast_to(lse[:, :bq], (HEAD_DIM, bq)),
                                                 jnp.broadcast_to(lse[:, bq:], (HEAD_DIM, bq))], axis=0).T

    def fill(r, carry):
        base = pl.multiple_of(r * n, 16)
        kn_scr[pl.ds(base, n), :], vn_scr[pl.ds(base, n), :] = normed_kv(r)
        return carry

    lax.fori_loop(0, dil, fill, 0)

    def query_block(b, carry):
        r = b >> (nblk.bit_length() - 1)
        i = b & (nblk - 1)
        ws = _window_start(i, bq, DIL_RADIUS, n, kw)
        start = pl.multiple_of(r * n + ws, 16)
        block(r, i, ws, kn_scr[pl.ds(start, kw), :], vn_scr[pl.ds(start, kw), :])
        return carry

    lax.fori_loop(0, dil * nblk, query_block, 0, unroll=min(dil * nblk, BAND_UNROLL))


def _dilated_group(dil_cols, pos, gq_row, gk_row, mseg, gi, batch, seq):
    _, dil = DIL_PATTERNS[gi]
    n = seq // dil
    bq = min(128, n)
    kw = min(bq + 2 * DIL_RADIUS, n)
    nblk = n // bq
    width = DIL_HPG * HEAD_DIM
    hw = 2 * HEAD_DIM
    cols3 = dil_cols.reshape(batch, seq, N_DIL_COLS)
    pcol, prow = _sub_positions(pos, batch, n, dil, bq)
    slopes = jnp.asarray([LOG2E * 2.0 ** (-8.0 * (gi * DIL_HPG + hh + 1) / DIL_HEADS) for hh in range(DIL_HPG)], F32)
    slope_rows = jnp.broadcast_to(jnp.pad(slopes, (0, 8 - DIL_HPG))[:, None], (8, LANES))
    kern = functools.partial(_dil_kernel, n=n, bq=bq, kw=kw, dil=dil)
    col = lambda which: (lambda b, h: (b, 0, (which * DIL_GROUPS + gi) * 2 + h))
    small = lambda shape: pl.BlockSpec(shape, lambda b, h: (0,) * len(shape))
    o, lse = pl.pallas_call(
        kern,
        out_shape=(jax.ShapeDtypeStruct((batch, seq, width), F32), jax.ShapeDtypeStruct((batch, seq, width), F32)),
        grid=(batch, 2),
        in_specs=[
            pl.BlockSpec((1, seq, hw), col(0)),
            pl.BlockSpec((1, seq, hw), col(1)),
            pl.BlockSpec((1, seq, hw), col(2)),
            pl.BlockSpec((1, seq, 1), lambda b, h: (b, 0, 0)),
            pl.BlockSpec((1, dil * nblk, 1, bq), lambda b, h: (b, 0, 0, 0)),
            small((1, hw)), small((1, hw)), small((8, LANES)), small((hw, hw)),
        ],
        out_specs=(pl.BlockSpec((1, seq, hw), lambda b, h: (b, 0, h)),
                   pl.BlockSpec((1, seq, hw), lambda b, h: (b, 0, h))),
        scratch_shapes=[pltpu.VMEM((seq, hw), BF16), pltpu.VMEM((seq, hw), BF16)],
        compiler_params=_cparams(("parallel", "parallel"), VMEM_LIMIT),
        name=f"dilated_group{gi}",
    )(cols3, cols3, cols3, pcol, prow, gq_row, gk_row, slope_rows, mseg)
    return o.reshape(batch * seq, width), lse.reshape(batch * seq, width)


def _win_kernel(q_ref, k_ref, v_ref, pcol_ref, prow_ref, gq_ref, gk_ref, sink_ref, mseg_ref, o_ref, kn_scr,
                *, n, bq, kw, slopes):
    i = pl.program_id(1)
    mseg = mseg_ref[...]

    @pl.when(i == 0)
    def _():
        kn_scr[...] = _seg_norm(k_ref[0].astype(F32), mseg, gk_ref[...]).astype(BF16)

    ws = _window_start(i, bq, WIN_RADIUS, n, kw)
    kwin = kn_scr[pl.ds(ws, kw), :]
    vwin = v_ref[0, pl.ds(ws, kw), :]
    dist = _masked_dist_t(i, bq, ws, kw, WIN_RADIUS, pcol_ref[0, pl.ds(ws, kw), :], prow_ref[0, 0])
    low_lane = lax.broadcasted_iota(I32, (1, 2 * HEAD_DIM), 1) < HEAD_DIM
    rep = WIN_HEADS // WIN_KV_HEADS

    qn_pairs = [_seg_norm(q_ref[0, :, pair * 128:(pair + 1) * 128].astype(F32), mseg, gq_ref[...])
                for pair in range(WIN_HEADS // 2)]
    q_parts = []
    for h in range(WIN_HEADS):
        pair, upper = divmod(h, 2)
        qm = jnp.where(low_lane, 0.0, qn_pairs[pair]) if upper else jnp.where(low_lane, qn_pairs[pair], 0.0)
        if upper != h // rep:
            qm = pltpu.roll(qm, HEAD_DIM, 1)
        q_parts.append(qm.astype(BF16))
    qs = jnp.concatenate(q_parts, axis=0)
    bias = jnp.concatenate([slopes[h] * dist for h in range(WIN_HEADS)], axis=1)
    sink = sink_ref[...]
    s = _dot_nt(kwin, qs) - bias
    m = jnp.maximum(jnp.max(s, axis=0, keepdims=True), sink)
    e = jnp.exp2(s - m)
    den = jnp.sum(e, axis=0, keepdims=True) + jnp.exp2(sink - m)
    ot = _dot_tn(vwin, e.astype(BF16)) / den
    for pair in range(WIN_HEADS // 2):
        slabs = []
        for h in (2 * pair, 2 * pair + 1):
            g = h // rep
            slabs.append(ot[g * HEAD_DIM:(g + 1) * HEAD_DIM, h * bq:(h + 1) * bq])
        o_ref[0, :, pair * 128:(pair + 1) * 128] = jnp.concatenate(slabs, axis=0).T.astype(o_ref.dtype)


def _window_mixer(cols, pos, gq_row, gk_row, sink_row, mseg128, batch, seq):
    n = seq
    bq = min(128, n)
    kw = min(bq + 2 * WIN_RADIUS, n)
    nblk = n // bq
    qw = WIN_HEADS * HEAD_DIM
    kvw = WIN_KV_HEADS * HEAD_DIM
    cols_v = cols.reshape(batch, seq, N_COLS)
    pcol = pos.reshape(batch, seq, 1)
    prow = pos.reshape(batch, nblk, 1, bq)
    slopes = tuple(float(LOG2E * 2.0 ** (-8.0 * (h + 1) / WIN_HEADS)) for h in range(WIN_HEADS))
    kern = functools.partial(_win_kernel, n=n, bq=bq, kw=kw, slopes=slopes)
    small = lambda shape: pl.BlockSpec(shape, lambda b, i: (0,) * len(shape))
    o = pl.pallas_call(
        kern,
        out_shape=jax.ShapeDtypeStruct((batch, seq, qw), BF16),
        grid=(batch, nblk),
        in_specs=[
            pl.BlockSpec((1, bq, qw), lambda b, i: (b, i, OFF_WQ // qw)),
            pl.BlockSpec((1, n, kvw), lambda b, i: (b, 0, OFF_WK // kvw)),
            pl.BlockSpec((1, n, kvw), lambda b, i: (b, 0, OFF_WV // kvw)),
            pl.BlockSpec((1, n, 1), lambda b, i: (b, 0, 0)),
            pl.BlockSpec((1, 1, 1, bq), lambda b, i: (b, i, 0, 0)),
            small((1, kvw)), small((1, kvw)), small((1, WIN_HEADS * bq)), small((kvw, kvw)),
        ],
        out_specs=pl.BlockSpec((1, bq, qw), lambda b, i: (b, i, 0)),
        scratch_shapes=[pltpu.VMEM((n, kvw), BF16)],
        compiler_params=_cparams(("parallel", "arbitrary"), VMEM_LIMIT),
        name="window_mixer",
    )(cols_v, cols_v, cols_v, pcol, prow, gq_row, gk_row, sink_row, mseg128)
    return o.reshape(batch * seq, qw)


def _merge_kernel(gm_ref, gd_ref, gw_ref, om_ref, od0_ref, od1_ref, od2_ref, l0_ref, l1_ref, l2_ref, ow_ref,
                  x_ref, mod_ref, wm_ref, wd_ref, ww_ref, wo_ref, o_ref):
    l0, l1, l2 = l0_ref[...], l1_ref[...], l2_ref[...]
    m = jnp.maximum(jnp.maximum(l0, l1), l2)
    e0, e1, e2 = jnp.exp(l0 - m), jnp.exp(l1 - m), jnp.exp(l2 - m)
    od = (e0 * od0_ref[...] + e1 * od1_ref[...] + e2 * od2_ref[...]) / (e0 + e1 + e2)
    y = jax.nn.sigmoid(gm_ref[...].astype(F32)) * _dot(om_ref[...], wm_ref[...])
    y += jax.nn.sigmoid(gd_ref[...].astype(F32)) * _dot(od.astype(BF16), wd_ref[...])
    y += jax.nn.sigmoid(gw_ref[...].astype(F32)) * _dot(ow_ref[...], ww_ref[...])
    z = _dot(y.astype(BF16), wo_ref[...])
    o_ref[...] = x_ref[...] + mod_ref[0][2:3] * z


def _merge(cols, o_mla, o_dil, lse_dil, o_win, x2d, mod_l, wm, wd, ww, wo, seq):
    t, d = x2d.shape
    tm = _tile(seq, 512)
    per_b = seq // tm
    row = lambda w, j=0: pl.BlockSpec((tm, w), lambda i: (i, j))
    full = lambda a: pl.BlockSpec(a.shape, lambda i: (0, 0))
    dw = DIL_HPG * HEAD_DIM
    return pl.pallas_call(
        _merge_kernel,
        out_shape=jax.ShapeDtypeStruct((t, d), F32),
        grid=(t // tm,),
        in_specs=[row(d, 0), row(d, 1), row(d, 2), row(o_mla.shape[1]),
                  row(dw), row(dw), row(dw), row(dw), row(dw), row(dw), row(o_win.shape[1]),
                  row(d), pl.BlockSpec((1, 6, d), lambda i: (i // per_b, 0, 0)),
                  full(wm), full(wd), full(ww), full(wo)],
        out_specs=row(d),
        compiler_params=_cparams(("parallel",), VMEM_LIMIT),
        name="merge_out_proj",
    )(cols, cols, cols, o_mla, o_dil[0], o_dil[1], o_dil[2], lse_dil[0], lse_dil[1], lse_dil[2], o_win,
      x2d, mod_l, wm, wd, ww, wo)


def _pack_halves(y):
    n = y.shape[1] // 2
    hi = pltpu.bitcast(y[:, :n].astype(BF16).astype(F32), I32)
    lo = pltpu.bitcast(y[:, n:].astype(BF16).astype(F32), I32)
    return hi | lax.shift_right_logical(lo, jnp.int32(16))


def _unpack_halves(u):
    hi = pltpu.bitcast(u & jnp.int32(-65536), F32).astype(BF16)
    lo = pltpu.bitcast(u << 16, F32).astype(BF16)
    return jnp.concatenate([hi, lo], axis=1)


def _router_kernel(x_ref, mod_ref, g_ref, whi_ref, wlo_ref, br_ref, upper_ref, hs_ref, pos_ref, gate_ref, tab_ref,
                   *, lr):
    m = mod_ref[0]
    h = _rms(x_ref[...], g_ref[...]) * (1.0 + m[4:5]) + m[3:4]
    tm = h.shape[0]
    h_hi = h.astype(BF16)
    h_lo = (h - h_hi.astype(F32)).astype(BF16)
    logits = _dot(h_hi, whi_ref[...]) + (_dot(h_hi, wlo_ref[...]) + _dot(h_lo, whi_ref[...])) + br_ref[...]
    lane = lax.broadcasted_iota(I32, (tm, LANES), 1)
    lane_f = lane.astype(F32)

    def first_argmax(vals, mx):
        return jnp.min(jnp.where(vals == mx, lane_f, float(LANES)), axis=-1, keepdims=True).astype(I32)

    lg = jnp.where(lane < N_EXPERT_GROUPS, logits, NEG_INF)
    mg = jnp.max(lg, axis=-1, keepdims=True)
    g_w = 1.0 / jnp.sum(jnp.exp(lg - mg), axis=-1, keepdims=True)
    g_idx = first_argmax(lg, mg)
    eid = lane - N_EXPERT_GROUPS
    in_grp = (eid >= 0) & (eid < N_EXPERTS) & ((eid >> 3) == g_idx)
    le = jnp.where(in_grp, logits, NEG_INF)
    m1 = jnp.max(le, axis=-1, keepdims=True)
    i1 = first_argmax(le, m1)
    le2 = jnp.where(lane == i1, NEG_INF, le)
    m2 = jnp.max(le2, axis=-1, keepdims=True)
    i2 = first_argmax(le2, m2)
    r = jnp.exp(m2 - m1)
    gate1 = g_w / (1.0 + r)
    gate2 = g_w * r / (1.0 + r)
    e1 = i1 - N_EXPERT_GROUPS
    e2 = i2 - N_EXPERT_GROUPS
    hit1 = lane == e1
    hit2 = lane == e2
    onehot = jnp.where(hit1 | hit2, 1.0, 0.0)
    rows = lax.broadcasted_iota(I32, (tm, tm), 0)
    cols = lax.broadcasted_iota(I32, (tm, tm), 1)
    before = jnp.where(rows > cols, 1.0, 0.0).astype(BF16)
    rank = _dot(before, onehot.astype(BF16))
    cnt8 = (jnp.sum(onehot, axis=0, keepdims=True).astype(I32) + (MOE_CHUNK - 1)) & jnp.int32(-MOE_CHUNK)
    off8 = _dot(jnp.broadcast_to(cnt8.astype(F32), (8, LANES)).astype(BF16), upper_ref[...])[0:1]
    pos1 = jnp.sum(jnp.where(hit1, rank + off8, 0.0), axis=-1, keepdims=True).astype(I32)
    pos2 = jnp.sum(jnp.where(hit2, rank + off8, 0.0), axis=-1, keepdims=True).astype(I32)
    pos_lanes = jnp.where(lane == 0, pos1, jnp.where(lane == 1, pos2, 0))
    pos_rows = pos_lanes.astype(F32).T.astype(I32)
    local = lax.broadcasted_iota(I32, (lr, tm), 0)
    place = jnp.where((local == pos_rows[0:1]) | (local == pos_rows[1:2]), 1.0, 0.0).astype(BF16)
    hs_ref[0] = _pack_halves(_dot(place, h_hi))
    pos_ref[...] = pos_lanes
    gate_ref[...] = jnp.where(lane == 0, gate1, jnp.where(lane == 1, gate2, 0.0))
    sub = lax.broadcasted_iota(I32, (8, LANES), 0)
    tab_ref[0] = jnp.where(sub == 0, cnt8, jnp.where(sub == 1, off8.astype(I32), 0))


def _router(x2d, mod_l, g_norm, w_router, b_router, seq, tm, lr):
    t, d = x2d.shape
    per_b = seq // tm
    nt = t // tm
    row = lambda w: pl.BlockSpec((tm, w), lambda i: (i, 0))
    idx = jnp.arange(LANES)
    upper = jnp.where(idx[:, None] < idx[None, :], 1.0, 0.0).astype(BF16)
    w_hi = w_router.astype(BF16)
    w_lo = (w_router - w_hi.astype(F32)).astype(BF16)
    return pl.pallas_call(
        functools.partial(_router_kernel, lr=lr),
        out_shape=(jax.ShapeDtypeStruct((nt, lr, d // 2), I32), jax.ShapeDtypeStruct((t, LANES), I32),
                   jax.ShapeDtypeStruct((t, LANES), F32), jax.ShapeDtypeStruct((nt, 8, LANES), I32)),
        grid=(nt,),
        in_specs=[row(d), pl.BlockSpec((1, 6, d), lambda i: (i // per_b, 0, 0)),
                  pl.BlockSpec((1, d), lambda i: (0, 0)), pl.BlockSpec((d, LANES), lambda i: (0, 0)),
                  pl.BlockSpec((d, LANES), lambda i: (0, 0)),
                  pl.BlockSpec((1, LANES), lambda i: (0, 0)), pl.BlockSpec((LANES, LANES), lambda i: (0, 0))],
        out_specs=(pl.BlockSpec((1, lr, d // 2), lambda i: (i, 0, 0)), row(LANES), row(LANES),
                   pl.BlockSpec((1, 8, LANES), lambda i: (i, 0, 0))),
        compiler_params=_cparams(("parallel",), VMEM_LIMIT),
        name="moe_router",
    )(x2d, mod_l, g_norm, w_hi, w_lo, b_router, upper)


def _for_each_chunk(off_ref, n_ref, base_ref, tile, fn):
    def per_expert(e, carry):
        k = tile * N_EXPERTS + e
        off, base = off_ref[k], base_ref[k]

        def per_chunk(c, carry2):
            fn(pl.multiple_of(off + c * MOE_CHUNK, MOE_CHUNK), pl.multiple_of(base + c * MOE_CHUNK, MOE_CHUNK))
            return carry2

        return lax.fori_loop(0, n_ref[k], per_chunk, carry)

    lax.fori_loop(0, N_EXPERTS, per_expert, 0)


def _scatter_kernel(off_ref, n_ref, base_ref, hs_ref, rows_in_ref, rows_ref, sem):
    del rows_in_ref
    tile = pl.program_id(0)

    def chunk_copy(local, glob):
        return pltpu.make_async_copy(hs_ref.at[0, pl.ds(local, MOE_CHUNK)], rows_ref.at[pl.ds(glob, MOE_CHUNK)], sem)

    _for_each_chunk(off_ref, n_ref, base_ref, tile, lambda a, b: chunk_copy(a, b).start())
    _for_each_chunk(off_ref, n_ref, base_ref, tile, lambda a, b: chunk_copy(a, b).wait())


def _scatter_rows(off8, n8, base, hs, n_rows):
    nt, lr, w = hs.shape
    zeros = jnp.zeros((n_rows, w), hs.dtype)
    grid_spec = pltpu.PrefetchScalarGridSpec(
        num_scalar_prefetch=3,
        grid=(nt,),
        in_specs=[pl.BlockSpec((1, lr, w), lambda i, *_: (i, 0, 0)), pl.BlockSpec(memory_space=pl.ANY)],
        out_specs=pl.BlockSpec(memory_space=pl.ANY),
        scratch_shapes=[pltpu.SemaphoreType.DMA],
    )
    return pl.pallas_call(
        _scatter_kernel,
        out_shape=jax.ShapeDtypeStruct((n_rows, w), hs.dtype),
        grid_spec=grid_spec,
        input_output_aliases={4: 0},
        compiler_params=_cparams(("arbitrary",)),
        name="moe_scatter",
    )(off8, n8, base, hs, zeros)


def _ffn_kernel(be_ref, nu_ref, rows_ref, w1_ref, w3_ref, w2_ref, y_ref, w13_scr, w2_scr):
    j = pl.program_id(0)
    used = j < nu_ref[0]
    new_expert = (j == 0) | (be_ref[j] != be_ref[jnp.maximum(j - 1, 0)])

    @pl.when(used & new_expert)
    def _():
        w13_scr[:, :D_EXPERT] = w1_ref[0].astype(BF16)
        w13_scr[:, D_EXPERT:] = w3_ref[0].astype(BF16)
        w2_scr[...] = w2_ref[0].astype(BF16)

    @pl.when(used)
    def _():
        h = _dot(_unpack_halves(rows_ref[...]), w13_scr[...])
        a = h[:, :D_EXPERT]
        act = a * jax.nn.sigmoid(a) * h[:, D_EXPERT:]
        y_ref[...] = _pack_halves(_dot(act.astype(BF16), w2_scr[...]))

    @pl.when(jnp.logical_not(used))
    def _():
        y_ref[...] = jnp.zeros_like(y_ref)


def _grouped_ffn(block_expert, n_used, rows, w1, w3, w2):
    n_rows, w = rows.shape
    d = w1.shape[1]
    nb = n_rows // MOE_BM
    grid_spec = pltpu.PrefetchScalarGridSpec(
        num_scalar_prefetch=2,
        grid=(nb,),
        in_specs=[
            pl.BlockSpec((MOE_BM, w), lambda j, be, nu: (j, 0)),
            pl.BlockSpec((1, d, D_EXPERT), lambda j, be, nu: (be[j], 0, 0)),
            pl.BlockSpec((1, d, D_EXPERT), lambda j, be, nu: (be[j], 0, 0)),
            pl.BlockSpec((1, D_EXPERT, d), lambda j, be, nu: (be[j], 0, 0)),
        ],
        out_specs=pl.BlockSpec((MOE_BM, w), lambda j, be, nu: (j, 0)),
        scratch_shapes=[pltpu.VMEM((d, 2 * D_EXPERT), BF16), pltpu.VMEM((D_EXPERT, d), BF16)],
    )
    return pl.pallas_call(
        _ffn_kernel,
        out_shape=jax.ShapeDtypeStruct((n_rows, w), I32),
        grid_spec=grid_spec,
        compiler_params=_cparams(("arbitrary",), VMEM_LIMIT),
        name="moe_grouped_ffn",
    )(block_expert, n_used, rows, w1, w3, w2)


def _combine_kernel(off_ref, n_ref, base_ref, y_ref, pos_ref, gate_ref, x_ref, mod_ref, o_ref, ybuf, sem):
    tile = pl.program_id(0)
    ybuf[...] = jnp.zeros_like(ybuf)

    def chunk_copy(local, glob):
        return pltpu.make_async_copy(y_ref.at[pl.ds(glob, MOE_CHUNK)], ybuf.at[pl.ds(local, MOE_CHUNK)], sem)

    _for_each_chunk(off_ref, n_ref, base_ref, tile, lambda a, b: chunk_copy(a, b).start())
    _for_each_chunk(off_ref, n_ref, base_ref, tile, lambda a, b: chunk_copy(a, b).wait())
    tm = x_ref.shape[0]
    pos, g = pos_ref[...], gate_ref[...]
    local = lax.broadcasted_iota(I32, (tm, ybuf.shape[0]), 1)
    pick = (jnp.where(local == pos[:, 0:1], g[:, 0:1], 0.0)
            + jnp.where(local == pos[:, 1:2], g[:, 1:2], 0.0)).astype(BF16)
    moe = _dot(pick, _unpack_halves(ybuf[...]))
    o_ref[...] = x_ref[...] + mod_ref[0][5:6] * moe


def _combine(off8, n8, base, y, pos, gates, x2d, mod_l, seq, tm, lr):
    t, d = x2d.shape
    per_b = seq // tm
    grid_spec = pltpu.PrefetchScalarGridSpec(
        num_scalar_prefetch=3,
        grid=(t // tm,),
        in_specs=[pl.BlockSpec(memory_space=pl.ANY),
                  pl.BlockSpec((tm, LANES), lambda i, *_: (i, 0)), pl.BlockSpec((tm, LANES), lambda i, *_: (i, 0)),
                  pl.BlockSpec((tm, d), lambda i, *_: (i, 0)),
                  pl.BlockSpec((1, 6, d), lambda i, *_: (i // per_b, 0, 0))],
        out_specs=pl.BlockSpec((tm, d), lambda i, *_: (i, 0)),
        scratch_shapes=[pltpu.VMEM((lr, y.shape[1]), I32), pltpu.SemaphoreType.DMA],
    )
    return pl.pallas_call(
        _combine_kernel,
        out_shape=jax.ShapeDtypeStruct((t, d), F32),
        grid_spec=grid_spec,
        compiler_params=_cparams(("arbitrary",), VMEM_LIMIT),
        name="moe_combine",
    )(off8, n8, base, y, pos, gates, x2d, mod_l)


def _moe(x2d, mod_l, g_norm2, w_gr, b_gr, w_er, b_er, w1, w3, w2, seq):
    t, d = x2d.shape
    pad = LANES - N_EXPERT_GROUPS - N_EXPERTS
    w_router = jnp.concatenate([w_gr, w_er, jnp.zeros((d, pad), F32)], axis=1)
    b_router = jnp.concatenate([b_gr, b_er, jnp.zeros((pad,), F32)]).reshape(1, LANES)
    tm = _tile(seq, 512)
    nt = t // tm
    lr = 2 * tm + N_EXPERTS * MOE_CHUNK
    hs, pos, gates, tab = _router(x2d, mod_l, g_norm2, w_router, b_router, seq, tm, lr)

    cnt8 = tab[:, 0, :N_EXPERTS]
    off8 = tab[:, 1, :N_EXPERTS]
    total = jnp.sum(cnt8, axis=0)
    padded = (total + MOE_BM - 1) // MOE_BM * MOE_BM
    ends = jnp.cumsum(padded)
    base = (ends - padded)[None, :] + jnp.cumsum(cnt8, axis=0) - cnt8
    nb = (2 * t + nt * N_EXPERTS * MOE_CHUNK) // MOE_BM + N_EXPERTS
    block_start = jnp.arange(nb, dtype=I32) * MOE_BM
    block_expert = jnp.minimum(jnp.sum(block_start[:, None] >= ends[None, :], axis=1), N_EXPERTS - 1).astype(I32)
    n_used = (ends[-1] // MOE_BM).astype(I32).reshape(1)
    flat = lambda a: a.reshape(-1).astype(I32)
    off8, n8, base = flat(off8), flat(cnt8 // MOE_CHUNK), flat(base)

    rows = _scatter_rows(off8, n8, base, hs, nb * MOE_BM)
    y = _grouped_ffn(block_expert, n_used, rows, w1, w3, w2)
    return _combine(off8, n8, base, y, pos, gates, x2d, mod_l, seq, tm, lr)


def _seg_matrix(width, segments):
    idx = jnp.arange(width)
    m = jnp.zeros((width, width), F32)
    for start, length in segments:
        inside = (idx >= start) & (idx < start + length)
        m = m + jnp.where(inside[:, None] & inside[None, :], 1.0 / length, 0.0)
    return m.astype(BF16)


def _layout_w_in(w_in):
    depth, d, _ = w_in.shape
    sizes = [MLA_Q_RANK, MLA_KV_RANK, MLA_ROPE, N_DIL_COLS,
             (WIN_HEADS + 2 * WIN_KV_HEADS) * HEAD_DIM, 3 * D_MODEL]
    bounds = [sum(sizes[:k + 1]) for k in range(len(sizes) - 1)]
    c_q, c_kv, k_rope, dil, win, gate = jnp.split(w_in, bounds, axis=-1)
    win_q, win_kv = win[..., :WIN_HEADS * HEAD_DIM], win[..., WIN_HEADS * HEAD_DIM:]
    z = lambda w: jnp.zeros((depth, d, w), w_in.dtype)
    out = jnp.concatenate([gate, c_q, win_q, c_kv, win_kv, z(MLA_NOPE), k_rope,
                           z(N_COLS - OFF_KR - MLA_NOPE - MLA_ROPE)], axis=-1)
    assert out.shape[-1] == N_COLS
    return out.astype(BF16), dil.astype(BF16)


def _layout_mla(w_uq, w_ukv, g_q, g_k):
    qd = MLA_NOPE + MLA_ROPE
    wq = w_uq.reshape(MLA_Q_RANK, MLA_HEADS, qd)
    wq = jnp.pad(wq, ((0, 0), (0, 0), (0, SLOT - qd))).reshape(MLA_Q_RANK, MLA_HEADS * SLOT)
    wkv = w_ukv.reshape(MLA_KV_RANK, MLA_HEADS, MLA_NOPE + MLA_V)
    wk = jnp.pad(wkv[:, :, :MLA_NOPE], ((0, 0), (0, 0), (0, SLOT - MLA_NOPE))).reshape(MLA_KV_RANK, MLA_HEADS * SLOT)
    wv = wkv[:, :, MLA_NOPE:].reshape(MLA_KV_RANK, MLA_HEADS * MLA_V)
    scale = LOG2E * float(qd) ** -0.5
    gq_slot = (jnp.pad(g_q, (0, SLOT - qd)) * scale).reshape(1, SLOT)
    gkn_slot = jnp.pad(g_k[:MLA_NOPE], (0, SLOT - MLA_NOPE)).reshape(1, SLOT)
    gkr_slot = jnp.pad(g_k[MLA_NOPE:], (MLA_NOPE, SLOT - qd)).reshape(1, SLOT)
    return wq.astype(BF16), jnp.concatenate([wk, wv], axis=1).astype(BF16), gq_slot, gkn_slot, gkr_slot


def kernel(x, c, pos, w_ada, b_ada, g_norm1, w_in, g_cq, w_uq, g_ckv, w_ukv, g_q_mla, g_k_mla, g_q_dil, g_k_dil,
           g_q_win, g_k_win, sink_win, w_br_mla, w_br_dil, w_br_win, w_out, g_norm2, w_gr, b_gr, w_er, b_er,
           w1, w3, w2):
    batch, seq, d = x.shape
    depth = w_ada.shape[0]
    t = batch * seq
    half = MLA_ROPE // 2
    inv_freq = ROPE_THETA ** (-jnp.arange(half, dtype=F32) / half)
    invf_slot = jnp.concatenate([jnp.zeros((MLA_NOPE,), F32), inv_freq, inv_freq,
                                 jnp.zeros((SLOT - MLA_NOPE - MLA_ROPE,), F32)]).reshape(1, SLOT)
    cos_t, sin_t = _rope_tables(pos.reshape(t, 1), invf_slot)
    mod = _modulation(c, w_ada, b_ada)
    w_in_k, w_dil_k = _layout_w_in(w_in)
    mseg_slot = _seg_matrix(SLOT, ((0, MLA_NOPE), (MLA_NOPE, MLA_ROPE)))
    mseg128 = _seg_matrix(128, tuple((k * HEAD_DIM, HEAD_DIM) for k in range(2)))
    head_scale = LOG2E * float(HEAD_DIM) ** -0.5
    win_bq = min(128, seq)

    x2d = x.reshape(t, d)
    for l in range(depth):
        mod_l = mod[l].reshape(batch, 6, d)
        g1 = g_norm1[l].reshape(1, d)
        cols = _in_projection(x2d, mod_l, g1, w_in_k[l], seq, BF16, N_COLS // 2, "in_projection")
        dil_cols = _in_projection(x2d, mod_l, g1, w_dil_k[l], seq, F32, N_DIL_COLS // 3, "in_projection_dil")

        wuq, wukv, gq_slot, gkn_slot, gkr_slot = _layout_mla(w_uq[l], w_ukv[l], g_q_mla[l], g_k_mla[l])
        qm, km, vm = _mla_prep(cols, cos_t, sin_t, g_cq[l].reshape(1, -1), g_ckv[l].reshape(1, -1),
                               wuq, wukv, gq_slot, gkn_slot, gkr_slot, mseg_slot)
        o_mla = _mla_attention(qm, km, vm, batch, seq)

        gq_dil = (jnp.tile(g_q_dil[l], 2) * head_scale).reshape(1, -1)
        gk_dil = jnp.tile(g_k_dil[l], 2).reshape(1, -1)
        o_dil, lse_dil = [], []
        for gi in range(DIL_GROUPS):
            o_g, lse_g = _dilated_group(dil_cols, pos, gq_dil, gk_dil, mseg128, gi, batch, seq)
            o_dil.append(o_g)
            lse_dil.append(lse_g)

        gq_win = (jnp.tile(g_q_win[l], 2) * head_scale).reshape(1, -1)
        gk_win = jnp.tile(g_k_win[l], WIN_KV_HEADS).reshape(1, -1)
        sink_row = jnp.repeat(sink_win[l].astype(F32) * LOG2E, win_bq).reshape(1, WIN_HEADS * win_bq)
        o_win = _window_mixer(cols, pos, gq_win, gk_win, sink_row, mseg128, batch, seq)

        x2d = _merge(cols, o_mla, o_dil, lse_dil, o_win, x2d, mod_l, w_br_mla[l].astype(BF16),
                     w_br_dil[l].astype(BF16), w_br_win[l].astype(BF16), w_out[l].astype(BF16), seq)
        x2d = _moe(x2d, mod_l, g_norm2[l].reshape(1, d), w_gr[l], b_gr[l], w_er[l], b_er[l],
                   w1[l], w3[l], w2[l], seq)
    return x2d.reshape(batch, seq, d)
```

```python
import functools
import math

import jax
import jax.numpy as jnp
from jax import lax
from jax.experimental import pallas as pl
from jax.experimental.pallas import tpu as pltpu

F32 = jnp.float32
BF16 = jnp.bfloat16
I32 = jnp.int32

D_MODEL = 1024
HEAD_DIM = 64
NEG_INF = -1e30
EPS = 1e-6
LOG2E = math.log2(math.e)
LN2 = math.log(2.0)
MLA_HEADS = 8
MLA_Q_RANK = 512
MLA_KV_RANK = 256
MLA_NOPE = 64
MLA_ROPE = 32
MLA_V = 64
ROPE_THETA = 10000.0
DIL_PATTERNS = ((128, 1), (512, 4), (2048, 16))
DIL_GROUPS = 3
DIL_HPG = 4
DIL_HEADS = DIL_GROUPS * DIL_HPG
DIL_RADIUS = 64
WIN_HEADS = 8
WIN_KV_HEADS = 2
WIN_RADIUS = 128
N_EXPERT_GROUPS = 4
EXPERTS_PER_GROUP = 8
N_EXPERTS = N_EXPERT_GROUPS * EXPERTS_PER_GROUP
D_EXPERT = 384

LANES = 128
SLOT = 128
VMEM_LIMIT = 48 * 1024 * 1024

OFF_GATE = 0
OFF_CQ = 3072
OFF_WQ = 3584
OFF_CKV = 4096
OFF_WK = 4352
OFF_WV = 4480
OFF_KR = 4608
N_COLS = 4864
N_DIL_COLS = 3 * DIL_HEADS * HEAD_DIM

MOE_BM = 512
MOE_CHUNK = 16
MLA_TQ = 256
MLA_HPS = 8


def _cparams(sem, vmem=None, flags=None):
    return pltpu.CompilerParams(dimension_semantics=sem, vmem_limit_bytes=vmem, flags=flags)


def _tile(n, pref):
    t = min(n, pref)
    assert n % t == 0, (n, pref)
    return t


def _dot(a, b):
    return jnp.dot(a, b, preferred_element_type=F32)


def _dot_nt(a, b):
    return lax.dot_general(a, b, (((1,), (1,)), ((), ())), preferred_element_type=F32)


def _dot_tn(a, b):
    return lax.dot_general(a, b, (((0,), (0,)), ((), ())), preferred_element_type=F32)


def _seg_mean_sq(x, mseg):
    x2 = x * x
    hi = x2.astype(BF16)
    lo = (x2 - hi.astype(F32)).astype(BF16)
    return _dot(hi, mseg) + _dot(lo, mseg)


def _seg_norm(x, mseg, gain):
    return x * lax.rsqrt(_seg_mean_sq(x, mseg) + EPS) * gain


def _rms(x, gain):
    ms = jnp.mean(x * x, axis=-1, keepdims=True)
    return x * lax.rsqrt(ms + EPS) * gain


def _mod_kernel(c_ref, w_ref, b_ref, o_ref):
    c = c_ref[...]
    cond = (c * jax.nn.sigmoid(c)).astype(BF16)
    o_ref[0] = _dot(cond, w_ref[0].astype(BF16)) + b_ref[0]


def _modulation(c, w_ada, b_ada):
    depth, d, n = w_ada.shape
    b = c.shape[0]
    tn = _tile(n, 1536)
    return pl.pallas_call(
        _mod_kernel,
        out_shape=jax.ShapeDtypeStruct((depth, b, n), F32),
        grid=(depth, n // tn),
        in_specs=[
            pl.BlockSpec((b, d), lambda l, j: (0, 0)),
            pl.BlockSpec((1, d, tn), lambda l, j: (l, 0, j)),
            pl.BlockSpec((1, 1, tn), lambda l, j: (l, 0, j)),
        ],
        out_specs=pl.BlockSpec((1, b, tn), lambda l, j: (l, 0, j)),
        compiler_params=_cparams(("parallel", "parallel")),
        name="adaln_mod",
    )(c, w_ada, b_ada.reshape(depth, 1, n))


def _rope_table_kernel(pos_ref, invf_ref, cos_ref, sin_ref):
    ang = pos_ref[...].astype(F32) * invf_ref[...]
    cos_ref[...] = jnp.cos(ang)
    sin_ref[...] = jnp.sin(ang)


def _rope_tables(pos_col, invf_slot):
    t = pos_col.shape[0]
    tm = _tile(t, 1024)
    return pl.pallas_call(
        _rope_table_kernel,
        out_shape=(jax.ShapeDtypeStruct((t, SLOT), F32), jax.ShapeDtypeStruct((t, SLOT), F32)),
        grid=(t // tm,),
        in_specs=[pl.BlockSpec((tm, 1), lambda i: (i, 0)), pl.BlockSpec((1, SLOT), lambda i: (0, 0))],
        out_specs=(pl.BlockSpec((tm, SLOT), lambda i: (i, 0)), pl.BlockSpec((tm, SLOT), lambda i: (i, 0))),
        compiler_params=_cparams(("parallel",)),
        name="rope_tables",
    )(pos_col, invf_slot)


def _inproj_kernel(x_ref, mod_ref, g_ref, w_ref, o_ref, h_scr):
    @pl.when(pl.program_id(1) == 0)
    def _():
        m = mod_ref[0]
        h = _rms(x_ref[...], g_ref[...]) * (1.0 + m[1:2]) + m[0:1]
        h_scr[...] = h.astype(BF16)

    o_ref[...] = _dot(h_scr[...], w_ref[...]).astype(o_ref.dtype)


def _in_projection(x2d, mod_l, g_norm, w_in_l, seq, out_dtype, tn, name):
    t, d = x2d.shape
    nc = w_in_l.shape[1]
    tm = _tile(seq, 1024)
    assert nc % tn == 0
    per_b = seq // tm
    return pl.pallas_call(
        _inproj_kernel,
        out_shape=jax.ShapeDtypeStruct((t, nc), out_dtype),
        grid=(t // tm, nc // tn),
        in_specs=[
            pl.BlockSpec((tm, d), lambda i, j: (i, 0)),
            pl.BlockSpec((1, 6, d), lambda i, j: (i // per_b, 0, 0)),
            pl.BlockSpec((1, d), lambda i, j: (0, 0)),
            pl.BlockSpec((d, tn), lambda i, j: (0, j)),
        ],
        out_specs=pl.BlockSpec((tm, tn), lambda i, j: (i, j)),
        scratch_shapes=[pltpu.VMEM((tm, d), BF16)],
        compiler_params=_cparams(("parallel", "arbitrary"), VMEM_LIMIT),
        name=name,
    )(x2d, mod_l, g_norm, w_in_l)


def _mla_prep_kernel(cq_ref, ckv_ref, kr_ref, cos_ref, sin_ref, gcq_ref, gckv_ref, wuq_ref, wukv_ref,
                     gq_ref, gkn_ref, gkr_ref, mseg_ref, q_out, k_out, v_out):
    cos = cos_ref[...]
    sin = sin_ref[...]
    lane = lax.broadcasted_iota(I32, (1, SLOT), 1)
    s_neg = jnp.where((lane >= 64) & (lane < 80), -sin, 0.0)
    s_pos = jnp.where((lane >= 80) & (lane < 96), sin, 0.0)
    mseg = mseg_ref[...]

    def rope(xn):
        return xn * cos + pltpu.roll(xn, SLOT - 16, 1) * s_neg + pltpu.roll(xn, 16, 1) * s_pos

    cqn = _rms(cq_ref[...].astype(F32), gcq_ref[...]).astype(BF16)
    q = _dot(cqn, wuq_ref[...])
    for h in range(MLA_HEADS):
        sl = slice(h * SLOT, (h + 1) * SLOT)
        q_out[:, sl] = rope(_seg_norm(q[:, sl], mseg, gq_ref[...])).astype(q_out.dtype)

    ckvn = _rms(ckv_ref[...].astype(F32), gckv_ref[...]).astype(BF16)
    kv = _dot(ckvn, wukv_ref[...])
    kr = rope(_seg_norm(kr_ref[...].astype(F32), mseg, gkr_ref[...]))
    for h in range(MLA_HEADS):
        sl = slice(h * SLOT, (h + 1) * SLOT)
        k_out[:, sl] = (_seg_norm(kv[:, sl], mseg, gkn_ref[...]) + kr).astype(k_out.dtype)
    v_out[...] = kv[:, MLA_HEADS * SLOT:].astype(v_out.dtype)


def _mla_prep(cols, cos_t, sin_t, gcq, gckv, wuq, wukv, gq_slot, gkn_slot, gkr_slot, mseg):
    t = cols.shape[0]
    tm = _tile(t, 512)
    hs = MLA_HEADS * SLOT
    full = lambda shape: pl.BlockSpec(shape, lambda i: (0,) * len(shape))
    return pl.pallas_call(
        _mla_prep_kernel,
        out_shape=(jax.ShapeDtypeStruct((t, hs), BF16), jax.ShapeDtypeStruct((t, hs), BF16),
                   jax.ShapeDtypeStruct((t, MLA_HEADS * MLA_V), BF16)),
        grid=(t // tm,),
        in_specs=[
            pl.BlockSpec((tm, MLA_Q_RANK), lambda i: (i, OFF_CQ // MLA_Q_RANK)),
            pl.BlockSpec((tm, MLA_KV_RANK), lambda i: (i, OFF_CKV // MLA_KV_RANK)),
            pl.BlockSpec((tm, SLOT), lambda i: (i, OFF_KR // SLOT)),
            pl.BlockSpec((tm, SLOT), lambda i: (i, 0)),
            pl.BlockSpec((tm, SLOT), lambda i: (i, 0)),
            full((1, MLA_Q_RANK)), full((1, MLA_KV_RANK)),
            full((MLA_Q_RANK, hs)), full((MLA_KV_RANK, hs + MLA_HEADS * MLA_V)),
            full((1, SLOT)), full((1, SLOT)), full((1, SLOT)), full((SLOT, SLOT)),
        ],
        out_specs=(pl.BlockSpec((tm, hs), lambda i: (i, 0)), pl.BlockSpec((tm, hs), lambda i: (i, 0)),
                   pl.BlockSpec((tm, MLA_HEADS * MLA_V), lambda i: (i, 0))),
        compiler_params=_cparams(("parallel",), VMEM_LIMIT),
        name="mla_prep",
    )(cols, cols, cols, cos_t, sin_t, gcq, gckv, wuq, wukv, gq_slot, gkn_slot, gkr_slot, mseg)


def _mla_attn_kernel(q_ref, k_ref, v_ref, o_ref):
    lane = lax.broadcasted_iota(I32, (1, 2 * MLA_V), 1)
    for pair in range(MLA_HPS // 2):
        outs = []
        vp = v_ref[0, :, pair * 2 * MLA_V:(pair + 1) * 2 * MLA_V]
        for a in range(2):
            sl = slice((2 * pair + a) * SLOT, (2 * pair + a + 1) * SLOT)
            s = _dot_nt(q_ref[0, :, sl], k_ref[0, :, sl])
            m = jnp.max(s, axis=-1, keepdims=True)
            p = jnp.exp2(s - m)
            l = jnp.sum(p, axis=-1, keepdims=True)
            outs.append(_dot(p.astype(BF16), vp) / l)
        o_ref[0, :, pair * 2 * MLA_V:(pair + 1) * 2 * MLA_V] = jnp.where(lane < MLA_V, outs[0], outs[1]).astype(o_ref.dtype)


def _mla_attention(qm, km, vm, batch, seq):
    hs = MLA_HEADS * SLOT
    q3 = qm.reshape(batch, seq, hs)
    k3 = km.reshape(batch, seq, hs)
    v3 = vm.reshape(batch, seq, MLA_HEADS * MLA_V)
    tq = _tile(seq, MLA_TQ)
    out = pl.pallas_call(
        _mla_attn_kernel,
        out_shape=jax.ShapeDtypeStruct((batch, seq, MLA_HEADS * MLA_V), BF16),
        grid=(batch, MLA_HEADS // MLA_HPS, seq // tq),
        in_specs=[
            pl.BlockSpec((1, tq, MLA_HPS * SLOT), lambda b, p, i: (b, i, p)),
            pl.BlockSpec((1, seq, MLA_HPS * SLOT), lambda b, p, i: (b, 0, p)),
            pl.BlockSpec((1, seq, MLA_HPS * MLA_V), lambda b, p, i: (b, 0, p)),
        ],
        out_specs=pl.BlockSpec((1, tq, MLA_HPS * MLA_V), lambda b, p, i: (b, i, p)),
        compiler_params=_cparams(("parallel", "parallel", "arbitrary"), VMEM_LIMIT),
        name="mla_attention",
    )(q3, k3, v3)
    return out.reshape(batch * seq, MLA_HEADS * MLA_V)


WIN_BLOCKS_PER_STEP = 4
BAND_UNROLL = 16


def _window_start(i, bq, radius, n, kw):
    ws = jnp.clip(i * bq - radius, 0, n - kw)
    return pl.multiple_of(ws, 16)


MASKED_DIST = 1e30


def _masked_dist_t(i, bq, ws, kw, radius, pos_keys, pos_queries):
    keys = lax.broadcasted_iota(I32, (kw, bq), 0)
    queries = lax.broadcasted_iota(I32, (kw, bq), 1)
    rel = queries - keys + (i * bq - ws)
    d = pos_keys - pos_queries
    return jnp.where(jnp.maximum(rel, -rel) <= radius, jnp.maximum(d, -d).astype(F32), MASKED_DIST)


def _sub_positions(pos, batch, n, dil, bq):
    pos_sub = jnp.transpose(pos.reshape(batch, n, dil), (0, 2, 1))
    return pos_sub.reshape(batch, dil * n, 1), pos_sub.reshape(batch, dil * (n // bq), 1, bq)


def _dil_kernel(q_ref, k_ref, v_ref, pcol_ref, prow_ref, gq_ref, gk_ref, slope_ref, mseg_ref, o_ref, lse_ref,
                kn_scr, vn_scr, *, n, bq, kw, dil):
    half = pl.program_id(1)
    nblk = n // bq
    mseg = mseg_ref[...]
    gq = gq_ref[...]
    low_lane = lax.broadcasted_iota(I32, (1, 2 * HEAD_DIM), 1) < HEAD_DIM
    slope0 = slope_ref[pl.ds(2 * half, 1), 0:1]
    slope1 = slope_ref[pl.ds(2 * half + 1, 1), 0:1]

    def rows(start, size):
        return pl.ds(start, size) if dil == 1 else pl.ds(start, size, stride=dil)

    def normed_kv(r):
        kn = _seg_norm(k_ref[0, rows(r, n), :], mseg, gk_ref[...]).astype(BF16)
        return kn, v_ref[0, rows(r, n), :].astype(BF16)

    def block(r, i, ws, kwin, vwin):
        q_rows = rows(r + i * (bq * dil), bq)
        qn = _seg_norm(q_ref[0, q_rows, :], mseg, gq)
        qs = jnp.concatenate([jnp.where(low_lane, qn, 0.0), jnp.where(low_lane, 0.0, qn)], axis=0).astype(BF16)
        dist = _masked_dist_t(i, bq, ws, kw, DIL_RADIUS,
                              pcol_ref[0, pl.ds(r * n + ws, kw), :], prow_ref[0, r * nblk + i])
        s = _dot_nt(kwin, qs) - jnp.concatenate([slope0 * dist, slope1 * dist], axis=1)
        m = jnp.max(s, axis=0, keepdims=True)
        p = jnp.exp2(s - m)
        l = jnp.sum(p, axis=0, keepdims=True)
        ot = _dot_tn(vwin, p.astype(BF16)) / l
        lse = (m + jnp.log2(l)) * LN2
        o_ref[0, q_rows, :] = jnp.concatenate([ot[:HEAD_DIM, :bq], ot[HEAD_DIM:, bq:]], axis=0).T
        lse_ref[0, q_rows, :] = jnp.concatenate([jnp.broadcast_to(lse[:, :bq], (HEAD_DIM, bq)),
                                                 jnp.broadcast_to(lse[:, bq:], (HEAD_DIM, bq))], axis=0).T

    def fill(r, carry):
        base = pl.multiple_of(r * n, 16)
        kn_scr[pl.ds(base, n), :], vn_scr[pl.ds(base, n), :] = normed_kv(r)
        return carry

    lax.fori_loop(0, dil, fill, 0)

    def query_block(b, carry):
        r = b >> (nblk.bit_length() - 1)
        i = b & (nblk - 1)
        ws = _window_start(i, bq, DIL_RADIUS, n, kw)
        start = pl.multiple_of(r * n + ws, 16)
        block(r, i, ws, kn_scr[pl.ds(start, kw), :], vn_scr[pl.ds(start, kw), :])
        return carry

    lax.fori_loop(0, dil * nblk, query_block, 0, unroll=min(dil * nblk, BAND_UNROLL))


def _dilated_group(dil_cols, pos, gq_row, gk_row, mseg, gi, batch, seq):
    _, dil = DIL_PATTERNS[gi]
    n = seq // dil
    bq = min(128, n)
    kw = min(bq + 2 * DIL_RADIUS, n)
    nblk = n // bq
    width = DIL_HPG * HEAD_DIM
    hw = 2 * HEAD_DIM
    cols3 = dil_cols.reshape(batch, seq, N_DIL_COLS)
    pcol, prow = _sub_positions(pos, batch, n, dil, bq)
    slopes = jnp.asarray([LOG2E * 2.0 ** (-8.0 * (gi * DIL_HPG + hh + 1) / DIL_HEADS) for hh in range(DIL_HPG)], F32)
    slope_rows = jnp.broadcast_to(jnp.pad(slopes, (0, 8 - DIL_HPG))[:, None], (8, LANES))
    kern = functools.partial(_dil_kernel, n=n, bq=bq, kw=kw, dil=dil)
    col = lambda which: (lambda b, h: (b, 0, (which * DIL_GROUPS + gi) * 2 + h))
    small = lambda shape: pl.BlockSpec(shape, lambda b, h: (0,) * len(shape))
    o, lse = pl.pallas_call(
        kern,
        out_shape=(jax.ShapeDtypeStruct((batch, seq, width), F32), jax.ShapeDtypeStruct((batch, seq, width), F32)),
        grid=(batch, 2),
        in_specs=[
            pl.BlockSpec((1, seq, hw), col(0)),
            pl.BlockSpec((1, seq, hw), col(1)),
            pl.BlockSpec((1, seq, hw), col(2)),
            pl.BlockSpec((1, seq, 1), lambda b, h: (b, 0, 0)),
            pl.BlockSpec((1, dil * nblk, 1, bq), lambda b, h: (b, 0, 0, 0)),
            small((1, hw)), small((1, hw)), small((8, LANES)), small((hw, hw)),
        ],
        out_specs=(pl.BlockSpec((1, seq, hw), lambda b, h: (b, 0, h)),
                   pl.BlockSpec((1, seq, hw), lambda b, h: (b, 0, h))),
        scratch_shapes=[pltpu.VMEM((seq, hw), BF16), pltpu.VMEM((seq, hw), BF16)],
        compiler_params=_cparams(("parallel", "parallel"), VMEM_LIMIT),
        name=f"dilated_group{gi}",
    )(cols3, cols3, cols3, pcol, prow, gq_row, gk_row, slope_rows, mseg)
    return o.reshape(batch * seq, width), lse.reshape(batch * seq, width)


def _win_kernel(q_ref, k_ref, v_ref, pcol_ref, prow_ref, gq_ref, gk_ref, sink_ref, mseg_ref, o_ref, kn_scr,
                *, n, bq, kw, slopes):
    step = pl.program_id(1)
    mseg = mseg_ref[...]

    @pl.when(step == 0)
    def _():
        kn_scr[...] = _seg_norm(k_ref[0].astype(F32), mseg, gk_ref[...]).astype(BF16)

    low_lane = lax.broadcasted_iota(I32, (1, 2 * HEAD_DIM), 1) < HEAD_DIM
    rep = WIN_HEADS // WIN_KV_HEADS
    sink = sink_ref[...]
    per_step = q_ref.shape[1] // bq
    for j in range(per_step):
        i = step * per_step + j
        rows = slice(j * bq, (j + 1) * bq)
        ws = _window_start(i, bq, WIN_RADIUS, n, kw)
        kwin = kn_scr[pl.ds(ws, kw), :]
        vwin = v_ref[0, pl.ds(ws, kw), :]
        dist = _masked_dist_t(i, bq, ws, kw, WIN_RADIUS, pcol_ref[0, pl.ds(ws, kw), :], prow_ref[0, j])

        qn_pairs = [_seg_norm(q_ref[0, rows, pair * 128:(pair + 1) * 128].astype(F32), mseg, gq_ref[...])
                    for pair in range(WIN_HEADS // 2)]
        q_parts = []
        for h in range(WIN_HEADS):
            pair, upper = divmod(h, 2)
            qm = jnp.where(low_lane, 0.0, qn_pairs[pair]) if upper else jnp.where(low_lane, qn_pairs[pair], 0.0)
            if upper != h // rep:
                qm = pltpu.roll(qm, HEAD_DIM, 1)
            q_parts.append(qm.astype(BF16))
        qs = jnp.concatenate(q_parts, axis=0)
        bias = jnp.concatenate([slopes[h] * dist for h in range(WIN_HEADS)], axis=1)
        s = _dot_nt(kwin, qs) - bias
        m = jnp.maximum(jnp.max(s, axis=0, keepdims=True), sink)
        e = jnp.exp2(s - m)
        den = jnp.sum(e, axis=0, keepdims=True) + jnp.exp2(sink - m)
        ot = _dot_tn(vwin, e.astype(BF16)) / den
        for pair in range(WIN_HEADS // 2):
            slabs = []
            for h in (2 * pair, 2 * pair + 1):
                g = h // rep
                slabs.append(ot[g * HEAD_DIM:(g + 1) * HEAD_DIM, h * bq:(h + 1) * bq])
            o_ref[0, rows, pair * 128:(pair + 1) * 128] = jnp.concatenate(slabs, axis=0).T.astype(o_ref.dtype)


def _window_mixer(cols, pos, gq_row, gk_row, sink_row, mseg128, batch, seq):
    n = seq
    bq = min(128, n)
    kw = min(bq + 2 * WIN_RADIUS, n)
    nblk = n // bq
    qw = WIN_HEADS * HEAD_DIM
    kvw = WIN_KV_HEADS * HEAD_DIM
    cols_v = cols.reshape(batch, seq, N_COLS)
    pcol = pos.reshape(batch, seq, 1)
    prow = pos.reshape(batch, nblk, 1, bq)
    slopes = tuple(float(LOG2E * 2.0 ** (-8.0 * (h + 1) / WIN_HEADS)) for h in range(WIN_HEADS))
    kern = functools.partial(_win_kernel, n=n, bq=bq, kw=kw, slopes=slopes)
    small = lambda shape: pl.BlockSpec(shape, lambda b, i: (0,) * len(shape))
    per_step = min(WIN_BLOCKS_PER_STEP, nblk)
    o = pl.pallas_call(
        kern,
        out_shape=jax.ShapeDtypeStruct((batch, seq, qw), BF16),
        grid=(batch, nblk // per_step),
        in_specs=[
            pl.BlockSpec((1, per_step * bq, qw), lambda b, i: (b, i, OFF_WQ // qw)),
            pl.BlockSpec((1, n, kvw), lambda b, i: (b, 0, OFF_WK // kvw)),
            pl.BlockSpec((1, n, kvw), lambda b, i: (b, 0, OFF_WV // kvw)),
            pl.BlockSpec((1, n, 1), lambda b, i: (b, 0, 0)),
            pl.BlockSpec((1, per_step, 1, bq), lambda b, i: (b, i, 0, 0)),
            small((1, kvw)), small((1, kvw)), small((1, WIN_HEADS * bq)), small((kvw, kvw)),
        ],
        out_specs=pl.BlockSpec((1, per_step * bq, qw), lambda b, i: (b, i, 0)),
        scratch_shapes=[pltpu.VMEM((n, kvw), BF16)],
        compiler_params=_cparams(("parallel", "arbitrary"), VMEM_LIMIT),
        name="window_mixer",
    )(cols_v, cols_v, cols_v, pcol, prow, gq_row, gk_row, sink_row, mseg128)
    return o.reshape(batch * seq, qw)


def _merge_kernel(gm_ref, gd_ref, gw_ref, om_ref, od0_ref, od1_ref, od2_ref, l0_ref, l1_ref, l2_ref, ow_ref,
                  x_ref, mod_ref, wm_ref, wd_ref, ww_ref, wo_ref, o_ref):
    l0, l1, l2 = l0_ref[...], l1_ref[...], l2_ref[...]
    m = jnp.maximum(jnp.maximum(l0, l1), l2)
    e0, e1, e2 = jnp.exp(l0 - m), jnp.exp(l1 - m), jnp.exp(l2 - m)
    od = (e0 * od0_ref[...] + e1 * od1_ref[...] + e2 * od2_ref[...]) / (e0 + e1 + e2)
    y = jax.nn.sigmoid(gm_ref[...].astype(F32)) * _dot(om_ref[...], wm_ref[...])
    y += jax.nn.sigmoid(gd_ref[...].astype(F32)) * _dot(od.astype(BF16), wd_ref[...])
    y += jax.nn.sigmoid(gw_ref[...].astype(F32)) * _dot(ow_ref[...], ww_ref[...])
    z = _dot(y.astype(BF16), wo_ref[...])
    o_ref[...] = x_ref[...] + mod_ref[0][2:3] * z


def _merge(cols, o_mla, o_dil, lse_dil, o_win, x2d, mod_l, wm, wd, ww, wo, seq):
    t, d = x2d.shape
    tm = _tile(seq, 512)
    per_b = seq // tm
    row = lambda w, j=0: pl.BlockSpec((tm, w), lambda i: (i, j))
    full = lambda a: pl.BlockSpec(a.shape, lambda i: (0, 0))
    dw = DIL_HPG * HEAD_DIM
    return pl.pallas_call(
        _merge_kernel,
        out_shape=jax.ShapeDtypeStruct((t, d), F32),
        grid=(t // tm,),
        in_specs=[row(d, 0), row(d, 1), row(d, 2), row(o_mla.shape[1]),
                  row(dw), row(dw), row(dw), row(dw), row(dw), row(dw), row(o_win.shape[1]),
                  row(d), pl.BlockSpec((1, 6, d), lambda i: (i // per_b, 0, 0)),
                  full(wm), full(wd), full(ww), full(wo)],
        out_specs=row(d),
        compiler_params=_cparams(("parallel",), VMEM_LIMIT),
        name="merge_out_proj",
    )(cols, cols, cols, o_mla, o_dil[0], o_dil[1], o_dil[2], lse_dil[0], lse_dil[1], lse_dil[2], o_win,
      x2d, mod_l, wm, wd, ww, wo)


def _router_kernel(x_ref, mod_ref, g_ref, whi_ref, wlo_ref, br_ref, upper_ref, hs_ref, pos_ref, gate_ref, tab_ref,
                   *, lr):
    m = mod_ref[0]
    h = _rms(x_ref[...], g_ref[...]) * (1.0 + m[4:5]) + m[3:4]
    tm = h.shape[0]
    h_hi = h.astype(BF16)
    h_lo = (h - h_hi.astype(F32)).astype(BF16)
    logits = _dot(h_hi, whi_ref[...]) + (_dot(h_hi, wlo_ref[...]) + _dot(h_lo, whi_ref[...])) + br_ref[...]
    lane = lax.broadcasted_iota(I32, (tm, LANES), 1)
    lane_f = lane.astype(F32)

    def first_argmax(vals, mx):
        return jnp.min(jnp.where(vals == mx, lane_f, float(LANES)), axis=-1, keepdims=True).astype(I32)

    lg = jnp.where(lane < N_EXPERT_GROUPS, logits, NEG_INF)
    mg = jnp.max(lg, axis=-1, keepdims=True)
    g_w = 1.0 / jnp.sum(jnp.exp(lg - mg), axis=-1, keepdims=True)
    g_idx = first_argmax(lg, mg)
    eid = lane - N_EXPERT_GROUPS
    in_grp = (eid >= 0) & (eid < N_EXPERTS) & ((eid >> 3) == g_idx)
    le = jnp.where(in_grp, logits, NEG_INF)
    m1 = jnp.max(le, axis=-1, keepdims=True)
    i1 = first_argmax(le, m1)
    le2 = jnp.where(lane == i1, NEG_INF, le)
    m2 = jnp.max(le2, axis=-1, keepdims=True)
    i2 = first_argmax(le2, m2)
    r = jnp.exp(m2 - m1)
    gate1 = g_w / (1.0 + r)
    gate2 = g_w * r / (1.0 + r)
    e1 = i1 - N_EXPERT_GROUPS
    e2 = i2 - N_EXPERT_GROUPS
    hit1 = lane == e1
    hit2 = lane == e2
    onehot = jnp.where(hit1 | hit2, 1.0, 0.0)
    rows = lax.broadcasted_iota(I32, (tm, tm), 0)
    cols = lax.broadcasted_iota(I32, (tm, tm), 1)
    before = jnp.where(rows > cols, 1.0, 0.0).astype(BF16)
    rank = _dot(before, onehot.astype(BF16))
    cnt8 = (jnp.sum(onehot, axis=0, keepdims=True).astype(I32) + (MOE_CHUNK - 1)) & jnp.int32(-MOE_CHUNK)
    off8 = _dot(jnp.broadcast_to(cnt8.astype(F32), (8, LANES)).astype(BF16), upper_ref[...])[0:1]
    pos1 = jnp.sum(jnp.where(hit1, rank + off8, 0.0), axis=-1, keepdims=True).astype(I32)
    pos2 = jnp.sum(jnp.where(hit2, rank + off8, 0.0), axis=-1, keepdims=True).astype(I32)
    pos_lanes = jnp.where(lane == 0, pos1, jnp.where(lane == 1, pos2, 0))
    pos_rows = pos_lanes.astype(F32).T.astype(I32)
    local = lax.broadcasted_iota(I32, (lr, tm), 0)
    place = jnp.where((local == pos_rows[0:1]) | (local == pos_rows[1:2]), 1.0, 0.0).astype(BF16)
    hs_ref[0] = _dot(place, h_hi).astype(BF16)
    pos_ref[...] = pos_lanes
    gate_ref[...] = jnp.where(lane == 0, gate1, jnp.where(lane == 1, gate2, 0.0))
    sub = lax.broadcasted_iota(I32, (8, LANES), 0)
    tab_ref[0] = jnp.where(sub == 0, cnt8, jnp.where(sub == 1, off8.astype(I32), 0))


def _router(x2d, mod_l, g_norm, w_router, b_router, seq, tm, lr):
    t, d = x2d.shape
    per_b = seq // tm
    nt = t // tm
    row = lambda w: pl.BlockSpec((tm, w), lambda i: (i, 0))
    idx = jnp.arange(LANES)
    upper = jnp.where(idx[:, None] < idx[None, :], 1.0, 0.0).astype(BF16)
    w_hi = w_router.astype(BF16)
    w_lo = (w_router - w_hi.astype(F32)).astype(BF16)
    return pl.pallas_call(
        functools.partial(_router_kernel, lr=lr),
        out_shape=(jax.ShapeDtypeStruct((nt, lr, d), BF16), jax.ShapeDtypeStruct((t, LANES), I32),
                   jax.ShapeDtypeStruct((t, LANES), F32), jax.ShapeDtypeStruct((nt, 8, LANES), I32)),
        grid=(nt,),
        in_specs=[row(d), pl.BlockSpec((1, 6, d), lambda i: (i // per_b, 0, 0)),
                  pl.BlockSpec((1, d), lambda i: (0, 0)), pl.BlockSpec((d, LANES), lambda i: (0, 0)),
                  pl.BlockSpec((d, LANES), lambda i: (0, 0)),
                  pl.BlockSpec((1, LANES), lambda i: (0, 0)), pl.BlockSpec((LANES, LANES), lambda i: (0, 0))],
        out_specs=(pl.BlockSpec((1, lr, d), lambda i: (i, 0, 0)), row(LANES), row(LANES),
                   pl.BlockSpec((1, 8, LANES), lambda i: (i, 0, 0))),
        compiler_params=_cparams(("parallel",), VMEM_LIMIT),
        name="moe_router",
    )(x2d, mod_l, g_norm, w_hi, w_lo, b_router, upper)


def _for_each_chunk(off_ref, n_ref, base_ref, tile, fn):
    def per_expert(e, carry):
        k = tile * N_EXPERTS + e
        off, base = off_ref[k], base_ref[k]

        def per_chunk(c, carry2):
            fn(pl.multiple_of(off + c * MOE_CHUNK, MOE_CHUNK), pl.multiple_of(base + c * MOE_CHUNK, MOE_CHUNK))
            return carry2

        return lax.fori_loop(0, n_ref[k], per_chunk, carry)

    lax.fori_loop(0, N_EXPERTS, per_expert, 0)


def _scatter_kernel(off_ref, n_ref, base_ref, hs_ref, rows_in_ref, rows_ref, sem):
    del rows_in_ref
    tile = pl.program_id(0)

    def chunk_copy(local, glob):
        return pltpu.make_async_copy(hs_ref.at[0, pl.ds(local, MOE_CHUNK)], rows_ref.at[pl.ds(glob, MOE_CHUNK)], sem)

    _for_each_chunk(off_ref, n_ref, base_ref, tile, lambda a, b: chunk_copy(a, b).start())
    _for_each_chunk(off_ref, n_ref, base_ref, tile, lambda a, b: chunk_copy(a, b).wait())


def _scatter_rows(off8, n8, base, hs, n_rows):
    nt, lr, w = hs.shape
    zeros = jnp.zeros((n_rows, w), hs.dtype)
    grid_spec = pltpu.PrefetchScalarGridSpec(
        num_scalar_prefetch=3,
        grid=(nt,),
        in_specs=[pl.BlockSpec((1, lr, w), lambda i, *_: (i, 0, 0)), pl.BlockSpec(memory_space=pl.ANY)],
        out_specs=pl.BlockSpec(memory_space=pl.ANY),
        scratch_shapes=[pltpu.SemaphoreType.DMA],
    )
    return pl.pallas_call(
        _scatter_kernel,
        out_shape=jax.ShapeDtypeStruct((n_rows, w), hs.dtype),
        grid_spec=grid_spec,
        input_output_aliases={4: 0},
        compiler_params=_cparams(("arbitrary",)),
        name="moe_scatter",
    )(off8, n8, base, hs, zeros)


def _ffn_kernel(be_ref, nu_ref, rows_ref, w1_ref, w3_ref, w2_ref, y_ref, w13_scr, w2_scr):
    j = pl.program_id(0)
    used = j < nu_ref[0]
    new_expert = (j == 0) | (be_ref[j] != be_ref[jnp.maximum(j - 1, 0)])

    @pl.when(used & new_expert)
    def _():
        w13_scr[:, :D_EXPERT] = w1_ref[0].astype(BF16)
        w13_scr[:, D_EXPERT:] = w3_ref[0].astype(BF16)
        w2_scr[...] = w2_ref[0].astype(BF16)

    @pl.when(used)
    def _():
        h = _dot(rows_ref[...], w13_scr[...])
        a = h[:, :D_EXPERT]
        act = a * jax.nn.sigmoid(a) * h[:, D_EXPERT:]
        y_ref[...] = _dot(act.astype(BF16), w2_scr[...]).astype(y_ref.dtype)

    @pl.when(jnp.logical_not(used))
    def _():
        y_ref[...] = jnp.zeros_like(y_ref)


def _grouped_ffn(block_expert, n_used, rows, w1, w3, w2):
    n_rows, w = rows.shape
    d = w1.shape[1]
    nb = n_rows // MOE_BM
    grid_spec = pltpu.PrefetchScalarGridSpec(
        num_scalar_prefetch=2,
        grid=(nb,),
        in_specs=[
            pl.BlockSpec((MOE_BM, w), lambda j, be, nu: (j, 0)),
            pl.BlockSpec((1, d, D_EXPERT), lambda j, be, nu: (be[j], 0, 0)),
            pl.BlockSpec((1, d, D_EXPERT), lambda j, be, nu: (be[j], 0, 0)),
            pl.BlockSpec((1, D_EXPERT, d), lambda j, be, nu: (be[j], 0, 0)),
        ],
        out_specs=pl.BlockSpec((MOE_BM, w), lambda j, be, nu: (j, 0)),
        scratch_shapes=[pltpu.VMEM((d, 2 * D_EXPERT), BF16), pltpu.VMEM((D_EXPERT, d), BF16)],
    )
    return pl.pallas_call(
        _ffn_kernel,
        out_shape=jax.ShapeDtypeStruct((n_rows, w), BF16),
        grid_spec=grid_spec,
        compiler_params=_cparams(("arbitrary",), VMEM_LIMIT),
        name="moe_grouped_ffn",
    )(block_expert, n_used, rows, w1, w3, w2)


def _combine_kernel(off_ref, n_ref, base_ref, y_ref, pos_ref, gate_ref, x_ref, mod_ref, o_ref, ybuf, sem):
    tile = pl.program_id(0)
    ybuf[...] = jnp.zeros_like(ybuf)

    def chunk_copy(local, glob):
        return pltpu.make_async_copy(y_ref.at[pl.ds(glob, MOE_CHUNK)], ybuf.at[pl.ds(local, MOE_CHUNK)], sem)

    _for_each_chunk(off_ref, n_ref, base_ref, tile, lambda a, b: chunk_copy(a, b).start())
    _for_each_chunk(off_ref, n_ref, base_ref, tile, lambda a, b: chunk_copy(a, b).wait())
    tm = x_ref.shape[0]
    pos, g = pos_ref[...], gate_ref[...]
    local = lax.broadcasted_iota(I32, (tm, ybuf.shape[0]), 1)
    pick = (jnp.where(local == pos[:, 0:1], g[:, 0:1], 0.0)
            + jnp.where(local == pos[:, 1:2], g[:, 1:2], 0.0)).astype(BF16)
    moe = _dot(pick, ybuf[...])
    o_ref[...] = x_ref[...] + mod_ref[0][5:6] * moe


def _combine(off8, n8, base, y, pos, gates, x2d, mod_l, seq, tm, lr):
    t, d = x2d.shape
    per_b = seq // tm
    grid_spec = pltpu.PrefetchScalarGridSpec(
        num_scalar_prefetch=3,
        grid=(t // tm,),
        in_specs=[pl.BlockSpec(memory_space=pl.ANY),
                  pl.BlockSpec((tm, LANES), lambda i, *_: (i, 0)), pl.BlockSpec((tm, LANES), lambda i, *_: (i, 0)),
                  pl.BlockSpec((tm, d), lambda i, *_: (i, 0)),
                  pl.BlockSpec((1, 6, d), lambda i, *_: (i // per_b, 0, 0))],
        out_specs=pl.BlockSpec((tm, d), lambda i, *_: (i, 0)),
        scratch_shapes=[pltpu.VMEM((lr, y.shape[1]), y.dtype), pltpu.SemaphoreType.DMA],
    )
    return pl.pallas_call(
        _combine_kernel,
        out_shape=jax.ShapeDtypeStruct((t, d), F32),
        grid_spec=grid_spec,
        compiler_params=_cparams(("arbitrary",), VMEM_LIMIT),
        name="moe_combine",
    )(off8, n8, base, y, pos, gates, x2d, mod_l)


def _moe(x2d, mod_l, g_norm2, w_gr, b_gr, w_er, b_er, w1, w3, w2, seq):
    t, d = x2d.shape
    pad = LANES - N_EXPERT_GROUPS - N_EXPERTS
    w_router = jnp.concatenate([w_gr, w_er, jnp.zeros((d, pad), F32)], axis=1)
    b_router = jnp.concatenate([b_gr, b_er, jnp.zeros((pad,), F32)]).reshape(1, LANES)
    tm = _tile(seq, 512)
    nt = t // tm
    lr = 2 * tm + N_EXPERTS * MOE_CHUNK
    hs, pos, gates, tab = _router(x2d, mod_l, g_norm2, w_router, b_router, seq, tm, lr)

    cnt8 = tab[:, 0, :N_EXPERTS]
    off8 = tab[:, 1, :N_EXPERTS]
    total = jnp.sum(cnt8, axis=0)
    padded = (total + MOE_BM - 1) // MOE_BM * MOE_BM
    ends = jnp.cumsum(padded)
    base = (ends - padded)[None, :] + jnp.cumsum(cnt8, axis=0) - cnt8
    nb = (2 * t + nt * N_EXPERTS * MOE_CHUNK) // MOE_BM + N_EXPERTS
    block_start = jnp.arange(nb, dtype=I32) * MOE_BM
    block_expert = jnp.minimum(jnp.sum(block_start[:, None] >= ends[None, :], axis=1), N_EXPERTS - 1).astype(I32)
    n_used = (ends[-1] // MOE_BM).astype(I32).reshape(1)
    flat = lambda a: a.reshape(-1).astype(I32)
    off8, n8, base = flat(off8), flat(cnt8 // MOE_CHUNK), flat(base)

    rows = _scatter_rows(off8, n8, base, hs, nb * MOE_BM)
    y = _grouped_ffn(block_expert, n_used, rows, w1, w3, w2)
    return _combine(off8, n8, base, y, pos, gates, x2d, mod_l, seq, tm, lr)


def _seg_matrix(width, segments):
    idx = jnp.arange(width)
    m = jnp.zeros((width, width), F32)
    for start, length in segments:
        inside = (idx >= start) & (idx < start + length)
        m = m + jnp.where(inside[:, None] & inside[None, :], 1.0 / length, 0.0)
    return m.astype(BF16)


def _layout_w_in(w_in):
    depth, d, _ = w_in.shape
    sizes = [MLA_Q_RANK, MLA_KV_RANK, MLA_ROPE, N_DIL_COLS,
             (WIN_HEADS + 2 * WIN_KV_HEADS) * HEAD_DIM, 3 * D_MODEL]
    bounds = [sum(sizes[:k + 1]) for k in range(len(sizes) - 1)]
    c_q, c_kv, k_rope, dil, win, gate = jnp.split(w_in.astype(BF16), bounds, axis=-1)
    win_q, win_kv = win[..., :WIN_HEADS * HEAD_DIM], win[..., WIN_HEADS * HEAD_DIM:]
    z = lambda w: jnp.zeros((depth, d, w), BF16)
    out = jnp.concatenate([gate, c_q, win_q, c_kv, win_kv, z(MLA_NOPE), k_rope,
                           z(N_COLS - OFF_KR - MLA_NOPE - MLA_ROPE)], axis=-1)
    assert out.shape[-1] == N_COLS
    return out, dil


def _layout_mla(w_uq, w_ukv, g_q, g_k):
    qd = MLA_NOPE + MLA_ROPE
    wq = w_uq.reshape(MLA_Q_RANK, MLA_HEADS, qd)
    wq = jnp.pad(wq, ((0, 0), (0, 0), (0, SLOT - qd))).reshape(MLA_Q_RANK, MLA_HEADS * SLOT)
    wkv = w_ukv.reshape(MLA_KV_RANK, MLA_HEADS, MLA_NOPE + MLA_V)
    wk = jnp.pad(wkv[:, :, :MLA_NOPE], ((0, 0), (0, 0), (0, SLOT - MLA_NOPE))).reshape(MLA_KV_RANK, MLA_HEADS * SLOT)
    wv = wkv[:, :, MLA_NOPE:].reshape(MLA_KV_RANK, MLA_HEADS * MLA_V)
    scale = LOG2E * float(qd) ** -0.5
    gq_slot = (jnp.pad(g_q, (0, SLOT - qd)) * scale).reshape(1, SLOT)
    gkn_slot = jnp.pad(g_k[:MLA_NOPE], (0, SLOT - MLA_NOPE)).reshape(1, SLOT)
    gkr_slot = jnp.pad(g_k[MLA_NOPE:], (MLA_NOPE, SLOT - qd)).reshape(1, SLOT)
    return wq.astype(BF16), jnp.concatenate([wk, wv], axis=1).astype(BF16), gq_slot, gkn_slot, gkr_slot


def kernel(x, c, pos, w_ada, b_ada, g_norm1, w_in, g_cq, w_uq, g_ckv, w_ukv, g_q_mla, g_k_mla, g_q_dil, g_k_dil,
           g_q_win, g_k_win, sink_win, w_br_mla, w_br_dil, w_br_win, w_out, g_norm2, w_gr, b_gr, w_er, b_er,
           w1, w3, w2):
    batch, seq, d = x.shape
    depth = w_ada.shape[0]
    t = batch * seq
    half = MLA_ROPE // 2
    inv_freq = ROPE_THETA ** (-jnp.arange(half, dtype=F32) / half)
    invf_slot = jnp.concatenate([jnp.zeros((MLA_NOPE,), F32), inv_freq, inv_freq,
                                 jnp.zeros((SLOT - MLA_NOPE - MLA_ROPE,), F32)]).reshape(1, SLOT)
    cos_t, sin_t = _rope_tables(pos.reshape(t, 1), invf_slot)
    mod = _modulation(c, w_ada, b_ada)
    w_in_k, w_dil_k = _layout_w_in(w_in)
    mseg_slot = _seg_matrix(SLOT, ((0, MLA_NOPE), (MLA_NOPE, MLA_ROPE)))
    mseg128 = _seg_matrix(128, tuple((k * HEAD_DIM, HEAD_DIM) for k in range(2)))
    head_scale = LOG2E * float(HEAD_DIM) ** -0.5
    win_bq = min(128, seq)

    x2d = x.reshape(t, d)
    for l in range(depth):
        mod_l = mod[l].reshape(batch, 6, d)
        g1 = g_norm1[l].reshape(1, d)
        cols = _in_projection(x2d, mod_l, g1, w_in_k[l], seq, BF16, N_COLS // 2, "in_projection")
        dil_cols = _in_projection(x2d, mod_l, g1, w_dil_k[l], seq, F32, N_DIL_COLS // 3, "in_projection_dil")

        wuq, wukv, gq_slot, gkn_slot, gkr_slot = _layout_mla(w_uq[l], w_ukv[l], g_q_mla[l], g_k_mla[l])
        qm, km, vm = _mla_prep(cols, cos_t, sin_t, g_cq[l].reshape(1, -1), g_ckv[l].reshape(1, -1),
                               wuq, wukv, gq_slot, gkn_slot, gkr_slot, mseg_slot)
        o_mla = _mla_attention(qm, km, vm, batch, seq)

        gq_dil = (jnp.tile(g_q_dil[l], 2) * head_scale).reshape(1, -1)
        gk_dil = jnp.tile(g_k_dil[l], 2).reshape(1, -1)
        o_dil, lse_dil = [], []
        for gi in range(DIL_GROUPS):
            o_g, lse_g = _dilated_group(dil_cols, pos, gq_dil, gk_dil, mseg128, gi, batch, seq)
            o_dil.append(o_g)
            lse_dil.append(lse_g)

        gq_win = (jnp.tile(g_q_win[l], 2) * head_scale).reshape(1, -1)
        gk_win = jnp.tile(g_k_win[l], WIN_KV_HEADS).reshape(1, -1)
        sink_row = jnp.repeat(sink_win[l].astype(F32) * LOG2E, win_bq).reshape(1, WIN_HEADS * win_bq)
        o_win = _window_mixer(cols, pos, gq_win, gk_win, sink_row, mseg128, batch, seq)

        x2d = _merge(cols, o_mla, o_dil, lse_dil, o_win, x2d, mod_l, w_br_mla[l].astype(BF16),
                     w_br_dil[l].astype(BF16), w_br_win[l].astype(BF16), w_out[l].astype(BF16), seq)
        x2d = _moe(x2d, mod_l, g_norm2[l].reshape(1, d), w_gr[l], b_gr[l], w_er[l], b_er[l],
                   w1[l], w3[l], w2[l], seq)
    return x2d.reshape(batch, seq, d)
```

```python
import functools
import math

import jax
import jax.numpy as jnp
from jax import lax
from jax.experimental import pallas as pl
from jax.experimental.pallas import tpu as pltpu

F32 = jnp.float32
BF16 = jnp.bfloat16
I32 = jnp.int32

D_MODEL = 1024
HEAD_DIM = 64
NEG_INF = -1e30
EPS = 1e-6
LOG2E = math.log2(math.e)
LN2 = math.log(2.0)
MLA_HEADS = 8
MLA_Q_RANK = 512
MLA_KV_RANK = 256
MLA_NOPE = 64
MLA_ROPE = 32
MLA_V = 64
ROPE_THETA = 10000.0
DIL_PATTERNS = ((128, 1), (512, 4), (2048, 16))
DIL_GROUPS = 3
DIL_HPG = 4
DIL_HEADS = DIL_GROUPS * DIL_HPG
DIL_RADIUS = 64
WIN_HEADS = 8
WIN_KV_HEADS = 2
WIN_RADIUS = 128
N_EXPERT_GROUPS = 4
EXPERTS_PER_GROUP = 8
N_EXPERTS = N_EXPERT_GROUPS * EXPERTS_PER_GROUP
D_EXPERT = 384

LANES = 128
SLOT = 128
VMEM_LIMIT = 48 * 1024 * 1024

OFF_GATE = 0
OFF_CQ = 3072
OFF_WQ = 3584
OFF_CKV = 4096
OFF_WK = 4352
OFF_WV = 4480
OFF_KR = 4608
N_COLS = 4864
N_DIL_COLS = 3 * DIL_HEADS * HEAD_DIM

MOE_BM = 512
MOE_CHUNK = 16
MLA_TQ = 256
MLA_HPS = 8


def _cparams(sem, vmem=None, flags=None):
    return pltpu.CompilerParams(dimension_semantics=sem, vmem_limit_bytes=vmem, flags=flags)


def _tile(n, pref):
    t = min(n, pref)
    assert n % t == 0, (n, pref)
    return t


def _dot(a, b):
    return jnp.dot(a, b, preferred_element_type=F32)


def _dot_nt(a, b):
    return lax.dot_general(a, b, (((1,), (1,)), ((), ())), preferred_element_type=F32)


def _dot_tn(a, b):
    return lax.dot_general(a, b, (((0,), (0,)), ((), ())), preferred_element_type=F32)


def _seg_mean_sq(x, mseg):
    x2 = x * x
    hi = x2.astype(BF16)
    lo = (x2 - hi.astype(F32)).astype(BF16)
    return _dot(hi, mseg) + _dot(lo, mseg)


def _seg_norm(x, mseg, gain):
    return x * lax.rsqrt(_seg_mean_sq(x, mseg) + EPS) * gain


def _rms(x, gain):
    ms = jnp.mean(x * x, axis=-1, keepdims=True)
    return x * lax.rsqrt(ms + EPS) * gain


def _mod_kernel(c_ref, w_ref, b_ref, o_ref):
    c = c_ref[...]
    cond = (c * jax.nn.sigmoid(c)).astype(BF16)
    o_ref[0] = _dot(cond, w_ref[0].astype(BF16)) + b_ref[0]


def _modulation(c, w_ada, b_ada):
    depth, d, n = w_ada.shape
    b = c.shape[0]
    tn = _tile(n, 1536)
    return pl.pallas_call(
        _mod_kernel,
        out_shape=jax.ShapeDtypeStruct((depth, b, n), F32),
        grid=(depth, n // tn),
        in_specs=[
            pl.BlockSpec((b, d), lambda l, j: (0, 0)),
            pl.BlockSpec((1, d, tn), lambda l, j: (l, 0, j)),
            pl.BlockSpec((1, 1, tn), lambda l, j: (l, 0, j)),
        ],
        out_specs=pl.BlockSpec((1, b, tn), lambda l, j: (l, 0, j)),
        compiler_params=_cparams(("parallel", "parallel")),
        name="adaln_mod",
    )(c, w_ada, b_ada.reshape(depth, 1, n))


def _rope_table_kernel(pos_ref, invf_ref, cos_ref, sin_ref):
    ang = pos_ref[...].astype(F32) * invf_ref[...]
    cos_ref[...] = jnp.cos(ang)
    sin_ref[...] = jnp.sin(ang)


def _rope_tables(pos_col, invf_slot):
    t = pos_col.shape[0]
    tm = _tile(t, 1024)
    return pl.pallas_call(
        _rope_table_kernel,
        out_shape=(jax.ShapeDtypeStruct((t, SLOT), F32), jax.ShapeDtypeStruct((t, SLOT), F32)),
        grid=(t // tm,),
        in_specs=[pl.BlockSpec((tm, 1), lambda i: (i, 0)), pl.BlockSpec((1, SLOT), lambda i: (0, 0))],
        out_specs=(pl.BlockSpec((tm, SLOT), lambda i: (i, 0)), pl.BlockSpec((tm, SLOT), lambda i: (i, 0))),
        compiler_params=_cparams(("parallel",)),
        name="rope_tables",
    )(pos_col, invf_slot)


def _inproj_kernel(x_ref, mod_ref, g_ref, w_ref, o_ref, h_scr):
    @pl.when(pl.program_id(1) == 0)
    def _():
        m = mod_ref[0]
        h = _rms(x_ref[...], g_ref[...]) * (1.0 + m[1:2]) + m[0:1]
        h_scr[...] = h.astype(BF16)

    o_ref[...] = _dot(h_scr[...], w_ref[...]).astype(o_ref.dtype)


def _in_projection(x2d, mod_l, g_norm, w_in_l, seq, out_dtype, tn, name):
    t, d = x2d.shape
    nc = w_in_l.shape[1]
    tm = _tile(seq, 1024)
    assert nc % tn == 0
    per_b = seq // tm
    return pl.pallas_call(
        _inproj_kernel,
        out_shape=jax.ShapeDtypeStruct((t, nc), out_dtype),
        grid=(t // tm, nc // tn),
        in_specs=[
            pl.BlockSpec((tm, d), lambda i, j: (i, 0)),
            pl.BlockSpec((1, 6, d), lambda i, j: (i // per_b, 0, 0)),
            pl.BlockSpec((1, d), lambda i, j: (0, 0)),
            pl.BlockSpec((d, tn), lambda i, j: (0, j)),
        ],
        out_specs=pl.BlockSpec((tm, tn), lambda i, j: (i, j)),
        scratch_shapes=[pltpu.VMEM((tm, d), BF16)],
        compiler_params=_cparams(("parallel", "arbitrary"), VMEM_LIMIT),
        name=name,
    )(x2d, mod_l, g_norm, w_in_l)


def _mla_prep_kernel(cq_ref, ckv_ref, kr_ref, cos_ref, sin_ref, gcq_ref, gckv_ref, wuq_ref, wukv_ref,
                     gq_ref, gkn_ref, gkr_ref, mseg_ref, q_out, k_out, v_out):
    cos = cos_ref[...]
    sin = sin_ref[...]
    lane = lax.broadcasted_iota(I32, (1, SLOT), 1)
    s_neg = jnp.where((lane >= 64) & (lane < 80), -sin, 0.0)
    s_pos = jnp.where((lane >= 80) & (lane < 96), sin, 0.0)
    mseg = mseg_ref[...]

    def rope(xn):
        return xn * cos + pltpu.roll(xn, SLOT - 16, 1) * s_neg + pltpu.roll(xn, 16, 1) * s_pos

    cqn = _rms(cq_ref[...].astype(F32), gcq_ref[...]).astype(BF16)
    q = _dot(cqn, wuq_ref[...])
    for h in range(MLA_HEADS):
        sl = slice(h * SLOT, (h + 1) * SLOT)
        q_out[:, sl] = rope(_seg_norm(q[:, sl], mseg, gq_ref[...])).astype(q_out.dtype)

    ckvn = _rms(ckv_ref[...].astype(F32), gckv_ref[...]).astype(BF16)
    kv = _dot(ckvn, wukv_ref[...])
    kr = rope(_seg_norm(kr_ref[...].astype(F32), mseg, gkr_ref[...]))
    for h in range(MLA_HEADS):
        sl = slice(h * SLOT, (h + 1) * SLOT)
        k_out[:, sl] = (_seg_norm(kv[:, sl], mseg, gkn_ref[...]) + kr).astype(k_out.dtype)
    v_out[...] = kv[:, MLA_HEADS * SLOT:].astype(v_out.dtype)


def _mla_prep(cols, cos_t, sin_t, gcq, gckv, wuq, wukv, gq_slot, gkn_slot, gkr_slot, mseg):
    t = cols.shape[0]
    tm = _tile(t, 512)
    hs = MLA_HEADS * SLOT
    full = lambda shape: pl.BlockSpec(shape, lambda i: (0,) * len(shape))
    return pl.pallas_call(
        _mla_prep_kernel,
        out_shape=(jax.ShapeDtypeStruct((t, hs), BF16), jax.ShapeDtypeStruct((t, hs), BF16),
                   jax.ShapeDtypeStruct((t, MLA_HEADS * MLA_V), BF16)),
        grid=(t // tm,),
        in_specs=[
            pl.BlockSpec((tm, MLA_Q_RANK), lambda i: (i, OFF_CQ // MLA_Q_RANK)),
            pl.BlockSpec((tm, MLA_KV_RANK), lambda i: (i, OFF_CKV // MLA_KV_RANK)),
            pl.BlockSpec((tm, SLOT), lambda i: (i, OFF_KR // SLOT)),
            pl.BlockSpec((tm, SLOT), lambda i: (i, 0)),
            pl.BlockSpec((tm, SLOT), lambda i: (i, 0)),
            full((1, MLA_Q_RANK)), full((1, MLA_KV_RANK)),
            full((MLA_Q_RANK, hs)), full((MLA_KV_RANK, hs + MLA_HEADS * MLA_V)),
            full((1, SLOT)), full((1, SLOT)), full((1, SLOT)), full((SLOT, SLOT)),
        ],
        out_specs=(pl.BlockSpec((tm, hs), lambda i: (i, 0)), pl.BlockSpec((tm, hs), lambda i: (i, 0)),
                   pl.BlockSpec((tm, MLA_HEADS * MLA_V), lambda i: (i, 0))),
        compiler_params=_cparams(("parallel",), VMEM_LIMIT),
        name="mla_prep",
    )(cols, cols, cols, cos_t, sin_t, gcq, gckv, wuq, wukv, gq_slot, gkn_slot, gkr_slot, mseg)


def _mla_attn_kernel(q_ref, k_ref, v_ref, o_ref):
    lane = lax.broadcasted_iota(I32, (1, 2 * MLA_V), 1)
    for pair in range(MLA_HPS // 2):
        outs = []
        vp = v_ref[0, :, pair * 2 * MLA_V:(pair + 1) * 2 * MLA_V]
        for a in range(2):
            sl = slice((2 * pair + a) * SLOT, (2 * pair + a + 1) * SLOT)
            s = _dot_nt(q_ref[0, :, sl], k_ref[0, :, sl])
            m = jnp.max(s, axis=-1, keepdims=True)
            p = jnp.exp2(s - m)
            l = jnp.sum(p, axis=-1, keepdims=True)
            outs.append(_dot(p.astype(BF16), vp) / l)
        o_ref[0, :, pair * 2 * MLA_V:(pair + 1) * 2 * MLA_V] = jnp.where(lane < MLA_V, outs[0], outs[1]).astype(o_ref.dtype)


def _mla_attention(qm, km, vm, batch, seq):
    hs = MLA_HEADS * SLOT
    q3 = qm.reshape(batch, seq, hs)
    k3 = km.reshape(batch, seq, hs)
    v3 = vm.reshape(batch, seq, MLA_HEADS * MLA_V)
    tq = _tile(seq, MLA_TQ)
    out = pl.pallas_call(
        _mla_attn_kernel,
        out_shape=jax.ShapeDtypeStruct((batch, seq, MLA_HEADS * MLA_V), BF16),
        grid=(batch, MLA_HEADS // MLA_HPS, seq // tq),
        in_specs=[
            pl.BlockSpec((1, tq, MLA_HPS * SLOT), lambda b, p, i: (b, i, p)),
            pl.BlockSpec((1, seq, MLA_HPS * SLOT), lambda b, p, i: (b, 0, p)),
            pl.BlockSpec((1, seq, MLA_HPS * MLA_V), lambda b, p, i: (b, 0, p)),
        ],
        out_specs=pl.BlockSpec((1, tq, MLA_HPS * MLA_V), lambda b, p, i: (b, i, p)),
        compiler_params=_cparams(("parallel", "parallel", "arbitrary"), VMEM_LIMIT),
        name="mla_attention",
    )(q3, k3, v3)
    return out.reshape(batch * seq, MLA_HEADS * MLA_V)


WIN_BLOCKS_PER_STEP = 4
BAND_UNROLL = 16


def _window_start(i, bq, radius, n, kw):
    ws = jnp.clip(i * bq - radius, 0, n - kw)
    return pl.multiple_of(ws, 16)


MASKED_DIST = 1e30


def _masked_dist_t(i, bq, ws, kw, radius, pos_keys, pos_queries):
    keys = lax.broadcasted_iota(I32, (kw, bq), 0)
    queries = lax.broadcasted_iota(I32, (kw, bq), 1)
    rel = queries - keys + (i * bq - ws)
    d = pos_keys - pos_queries
    return jnp.where(jnp.maximum(rel, -rel) <= radius, jnp.maximum(d, -d).astype(F32), MASKED_DIST)


def _sub_positions(pos, batch, n, dil, bq):
    pos_sub = jnp.transpose(pos.reshape(batch, n, dil), (0, 2, 1))
    return pos_sub.reshape(batch, dil * n, 1), pos_sub.reshape(batch, dil * (n // bq), 1, bq)


def _dil_kernel(q_ref, k_ref, v_ref, pcol_ref, prow_ref, gq_ref, gk_ref, slope_ref, mseg_ref, o_ref, lse_ref,
                kn_scr, vn_scr, *, n, bq, kw, dil):
    half = pl.program_id(1)
    nblk = n // bq
    mseg = mseg_ref[...]
    gq = gq_ref[...]
    low_lane = lax.broadcasted_iota(I32, (1, 2 * HEAD_DIM), 1) < HEAD_DIM
    slope0 = slope_ref[pl.ds(2 * half, 1), 0:1]
    slope1 = slope_ref[pl.ds(2 * half + 1, 1), 0:1]

    def rows(start, size):
        return pl.ds(start, size) if dil == 1 else pl.ds(start, size, stride=dil)

    def normed_kv(r):
        kn = _seg_norm(k_ref[0, rows(r, n), :], mseg, gk_ref[...]).astype(BF16)
        return kn, v_ref[0, rows(r, n), :].astype(BF16)

    def block(r, i, ws, kwin, vwin):
        q_rows = rows(r + i * (bq * dil), bq)
        qn = _seg_norm(q_ref[0, q_rows, :], mseg, gq)
        qs = jnp.concatenate([jnp.where(low_lane, qn, 0.0), jnp.where(low_lane, 0.0, qn)], axis=0).astype(BF16)
        dist = _masked_dist_t(i, bq, ws, kw, DIL_RADIUS,
                              pcol_ref[0, pl.ds(r * n + ws, kw), :], prow_ref[0, r * nblk + i])
        s = _dot_nt(kwin, qs) - jnp.concatenate([slope0 * dist, slope1 * dist], axis=1)
        m = jnp.max(s, axis=0, keepdims=True)
        p = jnp.exp2(s - m)
        l = jnp.sum(p, axis=0, keepdims=True)
        ot = _dot_tn(vwin, p.astype(BF16)) / l
        lse = (m + jnp.log2(l)) * LN2
        o_ref[0, q_rows, :] = jnp.concatenate([ot[:HEAD_DIM, :bq], ot[HEAD_DIM:, bq:]], axis=0).T
        lse_ref[0, q_rows, :] = jnp.concatenate([jnp.broadcast_to(lse[:, :bq], (HEAD_DIM, bq)),
                                                 jnp.broadcast_to(lse[:, bq:], (HEAD_DIM, bq))], axis=0).T

    def fill(r, carry):
        base = pl.multiple_of(r * n, 16)
        kn_scr[pl.ds(base, n), :], vn_scr[pl.ds(base, n), :] = normed_kv(r)
        return carry

    lax.fori_loop(0, dil, fill, 0)

    def query_block(b, carry):
        r = b >> (nblk.bit_length() - 1)
        i = b & (nblk - 1)
        ws = _window_start(i, bq, DIL_RADIUS, n, kw)
        start = pl.multiple_of(r * n + ws, 16)
        block(r, i, ws, kn_scr[pl.ds(start, kw), :], vn_scr[pl.ds(start, kw), :])
        return carry

    lax.fori_loop(0, dil * nblk, query_block, 0, unroll=min(dil * nblk, BAND_UNROLL))


def _dilated_group(dil_cols, pos, gq_row, gk_row, mseg, gi, batch, seq):
    _, dil = DIL_PATTERNS[gi]
    n = seq // dil
    bq = min(128, n)
    kw = min(bq + 2 * DIL_RADIUS, n)
    nblk = n // bq
    width = DIL_HPG * HEAD_DIM
    hw = 2 * HEAD_DIM
    cols3 = dil_cols.reshape(batch, seq, N_DIL_COLS)
    pcol, prow = _sub_positions(pos, batch, n, dil, bq)
    slopes = jnp.asarray([LOG2E * 2.0 ** (-8.0 * (gi * DIL_HPG + hh + 1) / DIL_HEADS) for hh in range(DIL_HPG)], F32)
    slope_rows = jnp.broadcast_to(jnp.pad(slopes, (0, 8 - DIL_HPG))[:, None], (8, LANES))
    kern = functools.partial(_dil_kernel, n=n, bq=bq, kw=kw, dil=dil)
    col = lambda which: (lambda b, h: (b, 0, (which * DIL_GROUPS + gi) * 2 + h))
    small = lambda shape: pl.BlockSpec(shape, lambda b, h: (0,) * len(shape))
    o, lse = pl.pallas_call(
        kern,
        out_shape=(jax.ShapeDtypeStruct((batch, seq, width), F32), jax.ShapeDtypeStruct((batch, seq, width), F32)),
        grid=(batch, 2),
        in_specs=[
            pl.BlockSpec((1, seq, hw), col(0)),
            pl.BlockSpec((1, seq, hw), col(1)),
            pl.BlockSpec((1, seq, hw), col(2)),
            pl.BlockSpec((1, seq, 1), lambda b, h: (b, 0, 0)),
            pl.BlockSpec((1, dil * nblk, 1, bq), lambda b, h: (b, 0, 0, 0)),
            small((1, hw)), small((1, hw)), small((8, LANES)), small((hw, hw)),
        ],
        out_specs=(pl.BlockSpec((1, seq, hw), lambda b, h: (b, 0, h)),
                   pl.BlockSpec((1, seq, hw), lambda b, h: (b, 0, h))),
        scratch_shapes=[pltpu.VMEM((seq, hw), BF16), pltpu.VMEM((seq, hw), BF16)],
        compiler_params=_cparams(("parallel", "parallel"), VMEM_LIMIT),
        name=f"dilated_group{gi}",
    )(cols3, cols3, cols3, pcol, prow, gq_row, gk_row, slope_rows, mseg)
    return o.reshape(batch * seq, width), lse.reshape(batch * seq, width)


def _win_kernel(q_ref, k_ref, v_ref, pcol_ref, prow_ref, gq_ref, gk_ref, sink_ref, mseg_ref, o_ref, kn_scr,
                *, n, bq, kw, slopes):
    step = pl.program_id(1)
    mseg = mseg_ref[...]

    @pl.when(step == 0)
    def _():
        kn_scr[...] = _seg_norm(k_ref[0].astype(F32), mseg, gk_ref[...]).astype(BF16)

    low_lane = lax.broadcasted_iota(I32, (1, 2 * HEAD_DIM), 1) < HEAD_DIM
    rep = WIN_HEADS // WIN_KV_HEADS
    sink = sink_ref[...]
    per_step = q_ref.shape[1] // bq
    for j in range(per_step):
        i = step * per_step + j
        rows = slice(j * bq, (j + 1) * bq)
        ws = _window_start(i, bq, WIN_RADIUS, n, kw)
        kwin = kn_scr[pl.ds(ws, kw), :]
        vwin = v_ref[0, pl.ds(ws, kw), :]
        dist = _masked_dist_t(i, bq, ws, kw, WIN_RADIUS, pcol_ref[0, pl.ds(ws, kw), :], prow_ref[0, j])

        qn_pairs = [_seg_norm(q_ref[0, rows, pair * 128:(pair + 1) * 128].astype(F32), mseg, gq_ref[...])
                    for pair in range(WIN_HEADS // 2)]
        q_parts = []
        for h in range(WIN_HEADS):
            pair, upper = divmod(h, 2)
            qm = jnp.where(low_lane, 0.0, qn_pairs[pair]) if upper else jnp.where(low_lane, qn_pairs[pair], 0.0)
            if upper != h // rep:
                qm = pltpu.roll(qm, HEAD_DIM, 1)
            q_parts.append(qm.astype(BF16))
        qs = jnp.concatenate(q_parts, axis=0)
        bias = jnp.concatenate([slopes[h] * dist for h in range(WIN_HEADS)], axis=1)
        s = _dot_nt(kwin, qs) - bias
        m = jnp.maximum(jnp.max(s, axis=0, keepdims=True), sink)
        e = jnp.exp2(s - m)
        den = jnp.sum(e, axis=0, keepdims=True) + jnp.exp2(sink - m)
        ot = _dot_tn(vwin, e.astype(BF16)) / den
        for pair in range(WIN_HEADS // 2):
            slabs = []
            for h in (2 * pair, 2 * pair + 1):
                g = h // rep
                slabs.append(ot[g * HEAD_DIM:(g + 1) * HEAD_DIM, h * bq:(h + 1) * bq])
            o_ref[0, rows, pair * 128:(pair + 1) * 128] = jnp.concatenate(slabs, axis=0).T.astype(o_ref.dtype)


def _window_mixer(cols, pos, gq_row, gk_row, sink_row, mseg128, batch, seq):
    n = seq
    bq = min(128, n)
    kw = min(bq + 2 * WIN_RADIUS, n)
    nblk = n // bq
    qw = WIN_HEADS * HEAD_DIM
    kvw = WIN_KV_HEADS * HEAD_DIM
    cols_v = cols.reshape(batch, seq, N_COLS)
    pcol = pos.reshape(batch, seq, 1)
    prow = pos.reshape(batch, nblk, 1, bq)
    slopes = tuple(float(LOG2E * 2.0 ** (-8.0 * (h + 1) / WIN_HEADS)) for h in range(WIN_HEADS))
    kern = functools.partial(_win_kernel, n=n, bq=bq, kw=kw, slopes=slopes)
    small = lambda shape: pl.BlockSpec(shape, lambda b, i: (0,) * len(shape))
    per_step = min(WIN_BLOCKS_PER_STEP, nblk)
    o = pl.pallas_call(
        kern,
        out_shape=jax.ShapeDtypeStruct((batch, seq, qw), BF16),
        grid=(batch, nblk // per_step),
        in_specs=[
            pl.BlockSpec((1, per_step * bq, qw), lambda b, i: (b, i, OFF_WQ // qw)),
            pl.BlockSpec((1, n, kvw), lambda b, i: (b, 0, OFF_WK // kvw)),
            pl.BlockSpec((1, n, kvw), lambda b, i: (b, 0, OFF_WV // kvw)),
            pl.BlockSpec((1, n, 1), lambda b, i: (b, 0, 0)),
            pl.BlockSpec((1, per_step, 1, bq), lambda b, i: (b, i, 0, 0)),
            small((1, kvw)), small((1, kvw)), small((1, WIN_HEADS * bq)), small((kvw, kvw)),
        ],
        out_specs=pl.BlockSpec((1, per_step * bq, qw), lambda b, i: (b, i, 0)),
        scratch_shapes=[pltpu.VMEM((n, kvw), BF16)],
        compiler_params=_cparams(("parallel", "arbitrary"), VMEM_LIMIT),
        name="window_mixer",
    )(cols_v, cols_v, cols_v, pcol, prow, gq_row, gk_row, sink_row, mseg128)
    return o.reshape(batch * seq, qw)


def _merge_kernel(gm_ref, gd_ref, gw_ref, om_ref, od0_ref, od1_ref, od2_ref, l0_ref, l1_ref, l2_ref, ow_ref,
                  x_ref, mod_ref, wm_ref, wd_ref, ww_ref, wo_ref, o_ref):
    l0, l1, l2 = l0_ref[...], l1_ref[...], l2_ref[...]
    m = jnp.maximum(jnp.maximum(l0, l1), l2)
    e0, e1, e2 = jnp.exp(l0 - m), jnp.exp(l1 - m), jnp.exp(l2 - m)
    od = (e0 * od0_ref[...] + e1 * od1_ref[...] + e2 * od2_ref[...]) / (e0 + e1 + e2)
    y = jax.nn.sigmoid(gm_ref[...].astype(F32)) * _dot(om_ref[...], wm_ref[...])
    y += jax.nn.sigmoid(gd_ref[...].astype(F32)) * _dot(od.astype(BF16), wd_ref[...])
    y += jax.nn.sigmoid(gw_ref[...].astype(F32)) * _dot(ow_ref[...], ww_ref[...])
    z = _dot(y.astype(BF16), wo_ref[...])
    o_ref[...] = x_ref[...] + mod_ref[0][2:3] * z


def _merge(cols, o_mla, o_dil, lse_dil, o_win, x2d, mod_l, wm, wd, ww, wo, seq):
    t, d = x2d.shape
    tm = _tile(seq, 512)
    per_b = seq // tm
    row = lambda w, j=0: pl.BlockSpec((tm, w), lambda i: (i, j))
    full = lambda a: pl.BlockSpec(a.shape, lambda i: (0, 0))
    dw = DIL_HPG * HEAD_DIM
    return pl.pallas_call(
        _merge_kernel,
        out_shape=jax.ShapeDtypeStruct((t, d), F32),
        grid=(t // tm,),
        in_specs=[row(d, 0), row(d, 1), row(d, 2), row(o_mla.shape[1]),
                  row(dw), row(dw), row(dw), row(dw), row(dw), row(dw), row(o_win.shape[1]),
                  row(d), pl.BlockSpec((1, 6, d), lambda i: (i // per_b, 0, 0)),
                  full(wm), full(wd), full(ww), full(wo)],
        out_specs=row(d),
        compiler_params=_cparams(("parallel",), VMEM_LIMIT),
        name="merge_out_proj",
    )(cols, cols, cols, o_mla, o_dil[0], o_dil[1], o_dil[2], lse_dil[0], lse_dil[1], lse_dil[2], o_win,
      x2d, mod_l, wm, wd, ww, wo)


def _router_kernel(x_ref, mod_ref, g_ref, whi_ref, wlo_ref, br_ref, upper_ref, hs_ref, pos_ref, gate_ref, tab_ref,
                   *, lr):
    m = mod_ref[0]
    h = _rms(x_ref[...], g_ref[...]) * (1.0 + m[4:5]) + m[3:4]
    tm = h.shape[0]
    h_hi = h.astype(BF16)
    h_lo = (h - h_hi.astype(F32)).astype(BF16)
    logits = _dot(h_hi, whi_ref[...]) + (_dot(h_hi, wlo_ref[...]) + _dot(h_lo, whi_ref[...])) + br_ref[...]
    lane = lax.broadcasted_iota(I32, (tm, LANES), 1)
    lane_f = lane.astype(F32)

    def first_argmax(vals, mx):
        return jnp.min(jnp.where(vals == mx, lane_f, float(LANES)), axis=-1, keepdims=True).astype(I32)

    lg = jnp.where(lane < N_EXPERT_GROUPS, logits, NEG_INF)
    mg = jnp.max(lg, axis=-1, keepdims=True)
    g_w = 1.0 / jnp.sum(jnp.exp(lg - mg), axis=-1, keepdims=True)
    g_idx = first_argmax(lg, mg)
    eid = lane - N_EXPERT_GROUPS
    in_grp = (eid >= 0) & (eid < N_EXPERTS) & ((eid >> 3) == g_idx)
    le = jnp.where(in_grp, logits, NEG_INF)
    m1 = jnp.max(le, axis=-1, keepdims=True)
    i1 = first_argmax(le, m1)
    le2 = jnp.where(lane == i1, NEG_INF, le)
    m2 = jnp.max(le2, axis=-1, keepdims=True)
    i2 = first_argmax(le2, m2)
    r = jnp.exp(m2 - m1)
    gate1 = g_w / (1.0 + r)
    gate2 = g_w * r / (1.0 + r)
    e1 = i1 - N_EXPERT_GROUPS
    e2 = i2 - N_EXPERT_GROUPS
    hit1 = lane == e1
    hit2 = lane == e2
    onehot = jnp.where(hit1 | hit2, 1.0, 0.0)
    rows = lax.broadcasted_iota(I32, (tm, tm), 0)
    cols = lax.broadcasted_iota(I32, (tm, tm), 1)
    before = jnp.where(rows > cols, 1.0, 0.0).astype(BF16)
    rank = _dot(before, onehot.astype(BF16))
    cnt8 = (jnp.sum(onehot, axis=0, keepdims=True).astype(I32) + (MOE_CHUNK - 1)) & jnp.int32(-MOE_CHUNK)
    off8 = _dot(jnp.broadcast_to(cnt8.astype(F32), (8, LANES)).astype(BF16), upper_ref[...])[0:1]
    pos1 = jnp.sum(jnp.where(hit1, rank + off8, 0.0), axis=-1, keepdims=True).astype(I32)
    pos2 = jnp.sum(jnp.where(hit2, rank + off8, 0.0), axis=-1, keepdims=True).astype(I32)
    pos_lanes = jnp.where(lane == 0, pos1, jnp.where(lane == 1, pos2, 0))
    pos_rows = pos_lanes.astype(F32).T.astype(I32)
    local = lax.broadcasted_iota(I32, (lr, tm), 0)
    place = jnp.where((local == pos_rows[0:1]) | (local == pos_rows[1:2]), 1.0, 0.0).astype(BF16)
    hs_ref[0] = _dot(place, h_hi).astype(BF16)
    pos_ref[...] = pos_lanes
    gate_ref[...] = jnp.where(lane == 0, gate1, jnp.where(lane == 1, gate2, 0.0))
    sub = lax.broadcasted_iota(I32, (8, LANES), 0)
    tab_ref[0] = jnp.where(sub == 0, cnt8, jnp.where(sub == 1, off8.astype(I32), 0))


def _router(x2d, mod_l, g_norm, w_router, b_router, seq, tm, lr):
    t, d = x2d.shape
    per_b = seq // tm
    nt = t // tm
    row = lambda w: pl.BlockSpec((tm, w), lambda i: (i, 0))
    idx = jnp.arange(LANES)
    upper = jnp.where(idx[:, None] < idx[None, :], 1.0, 0.0).astype(BF16)
    w_hi = w_router.astype(BF16)
    w_lo = (w_router - w_hi.astype(F32)).astype(BF16)
    return pl.pallas_call(
        functools.partial(_router_kernel, lr=lr),
        out_shape=(jax.ShapeDtypeStruct((nt, lr, d), BF16), jax.ShapeDtypeStruct((t, LANES), I32),
                   jax.ShapeDtypeStruct((t, LANES), F32), jax.ShapeDtypeStruct((nt, 8, LANES), I32)),
        grid=(nt,),
        in_specs=[row(d), pl.BlockSpec((1, 6, d), lambda i: (i // per_b, 0, 0)),
                  pl.BlockSpec((1, d), lambda i: (0, 0)), pl.BlockSpec((d, LANES), lambda i: (0, 0)),
                  pl.BlockSpec((d, LANES), lambda i: (0, 0)),
                  pl.BlockSpec((1, LANES), lambda i: (0, 0)), pl.BlockSpec((LANES, LANES), lambda i: (0, 0))],
        out_specs=(pl.BlockSpec((1, lr, d), lambda i: (i, 0, 0)), row(LANES), row(LANES),
                   pl.BlockSpec((1, 8, LANES), lambda i: (i, 0, 0))),
        compiler_params=_cparams(("parallel",), VMEM_LIMIT),
        name="moe_router",
    )(x2d, mod_l, g_norm, w_hi, w_lo, b_router, upper)


def _for_each_chunk(off_ref, n_ref, base_ref, tile, fn):
    def per_expert(e, carry):
        k = tile * N_EXPERTS + e
        off, base = off_ref[k], base_ref[k]

        def per_chunk(c, carry2):
            fn(pl.multiple_of(off + c * MOE_CHUNK, MOE_CHUNK), pl.multiple_of(base + c * MOE_CHUNK, MOE_CHUNK))
            return carry2

        return lax.fori_loop(0, n_ref[k], per_chunk, carry)

    lax.fori_loop(0, N_EXPERTS, per_expert, 0)


def _scatter_kernel(off_ref, n_ref, base_ref, hs_ref, rows_in_ref, rows_ref, sem):
    del rows_in_ref
    tile = pl.program_id(0)

    def chunk_copy(local, glob):
        return pltpu.make_async_copy(hs_ref.at[0, pl.ds(local, MOE_CHUNK)], rows_ref.at[pl.ds(glob, MOE_CHUNK)], sem)

    _for_each_chunk(off_ref, n_ref, base_ref, tile, lambda a, b: chunk_copy(a, b).start())
    _for_each_chunk(off_ref, n_ref, base_ref, tile, lambda a, b: chunk_copy(a, b).wait())


def _scatter_rows(off8, n8, base, hs, n_rows):
    nt, lr, w = hs.shape
    zeros = jnp.zeros((n_rows, w), hs.dtype)
    grid_spec = pltpu.PrefetchScalarGridSpec(
        num_scalar_prefetch=3,
        grid=(nt,),
        in_specs=[pl.BlockSpec((1, lr, w), lambda i, *_: (i, 0, 0)), pl.BlockSpec(memory_space=pl.ANY)],
        out_specs=pl.BlockSpec(memory_space=pl.ANY),
        scratch_shapes=[pltpu.SemaphoreType.DMA],
    )
    return pl.pallas_call(
        _scatter_kernel,
        out_shape=jax.ShapeDtypeStruct((n_rows, w), hs.dtype),
        grid_spec=grid_spec,
        input_output_aliases={4: 0},
        compiler_params=_cparams(("arbitrary",)),
        name="moe_scatter",
    )(off8, n8, base, hs, zeros)


def _ffn_kernel(be_ref, nu_ref, rows_ref, w1_ref, w3_ref, w2_ref, y_ref, w13_scr, w2_scr):
    j = pl.program_id(0)
    used = j < nu_ref[0]
    new_expert = (j == 0) | (be_ref[j] != be_ref[jnp.maximum(j - 1, 0)])

    @pl.when(used & new_expert)
    def _():
        w13_scr[:, :D_EXPERT] = w1_ref[0].astype(BF16)
        w13_scr[:, D_EXPERT:] = w3_ref[0].astype(BF16)
        w2_scr[...] = w2_ref[0].astype(BF16)

    @pl.when(used)
    def _():
        h = _dot(rows_ref[...], w13_scr[...])
        a = h[:, :D_EXPERT]
        act = a * jax.nn.sigmoid(a) * h[:, D_EXPERT:]
        y_ref[...] = _dot(act.astype(BF16), w2_scr[...]).astype(y_ref.dtype)

    @pl.when(jnp.logical_not(used))
    def _():
        y_ref[...] = jnp.zeros_like(y_ref)


def _grouped_ffn(block_expert, n_used, rows, w1, w3, w2):
    n_rows, w = rows.shape
    d = w1.shape[1]
    nb = n_rows // MOE_BM
    grid_spec = pltpu.PrefetchScalarGridSpec(
        num_scalar_prefetch=2,
        grid=(nb,),
        in_specs=[
            pl.BlockSpec((MOE_BM, w), lambda j, be, nu: (j, 0)),
            pl.BlockSpec((1, d, D_EXPERT), lambda j, be, nu: (be[j], 0, 0)),
            pl.BlockSpec((1, d, D_EXPERT), lambda j, be, nu: (be[j], 0, 0)),
            pl.BlockSpec((1, D_EXPERT, d), lambda j, be, nu: (be[j], 0, 0)),
        ],
        out_specs=pl.BlockSpec((MOE_BM, w), lambda j, be, nu: (j, 0)),
        scratch_shapes=[pltpu.VMEM((d, 2 * D_EXPERT), BF16), pltpu.VMEM((D_EXPERT, d), BF16)],
    )
    return pl.pallas_call(
        _ffn_kernel,
        out_shape=jax.ShapeDtypeStruct((n_rows, w), BF16),
        grid_spec=grid_spec,
        compiler_params=_cparams(("arbitrary",), VMEM_LIMIT),
        name="moe_grouped_ffn",
    )(block_expert, n_used, rows, w1, w3, w2)


def _combine_kernel(off_ref, n_ref, base_ref, y_ref, pos_ref, gate_ref, x_ref, mod_ref, o_ref, ybuf, sem):
    tile = pl.program_id(0)
    slot = tile % 2

    def chunk_copy(s, local, glob):
        return pltpu.make_async_copy(y_ref.at[pl.ds(glob, MOE_CHUNK)], ybuf.at[s, pl.ds(local, MOE_CHUNK)], sem.at[s])

    def request(t, s):
        ybuf[s] = jnp.zeros(ybuf.shape[1:], ybuf.dtype)
        _for_each_chunk(off_ref, n_ref, base_ref, t, lambda a, b: chunk_copy(s, a, b).start())

    @pl.when(tile == 0)
    def _():
        request(tile, slot)

    @pl.when(tile + 1 < pl.num_programs(0))
    def _():
        request(tile + 1, 1 - slot)

    _for_each_chunk(off_ref, n_ref, base_ref, tile, lambda a, b: chunk_copy(slot, a, b).wait())
    tm = x_ref.shape[0]
    pos, g = pos_ref[...], gate_ref[...]
    local = lax.broadcasted_iota(I32, (tm, ybuf.shape[1]), 1)
    pick = (jnp.where(local == pos[:, 0:1], g[:, 0:1], 0.0)
            + jnp.where(local == pos[:, 1:2], g[:, 1:2], 0.0)).astype(BF16)
    moe = _dot(pick, ybuf[slot])
    o_ref[...] = x_ref[...] + mod_ref[0][5:6] * moe


def _combine(off8, n8, base, y, pos, gates, x2d, mod_l, seq, tm, lr):
    t, d = x2d.shape
    per_b = seq // tm
    grid_spec = pltpu.PrefetchScalarGridSpec(
        num_scalar_prefetch=3,
        grid=(t // tm,),
        in_specs=[pl.BlockSpec(memory_space=pl.ANY),
                  pl.BlockSpec((tm, LANES), lambda i, *_: (i, 0)), pl.BlockSpec((tm, LANES), lambda i, *_: (i, 0)),
                  pl.BlockSpec((tm, d), lambda i, *_: (i, 0)),
                  pl.BlockSpec((1, 6, d), lambda i, *_: (i // per_b, 0, 0))],
        out_specs=pl.BlockSpec((tm, d), lambda i, *_: (i, 0)),
        scratch_shapes=[pltpu.VMEM((2, lr, y.shape[1]), y.dtype), pltpu.SemaphoreType.DMA((2,))],
    )
    return pl.pallas_call(
        _combine_kernel,
        out_shape=jax.ShapeDtypeStruct((t, d), F32),
        grid_spec=grid_spec,
        compiler_params=_cparams(("arbitrary",), VMEM_LIMIT),
        name="moe_combine",
    )(off8, n8, base, y, pos, gates, x2d, mod_l)


def _moe(x2d, mod_l, g_norm2, w_gr, b_gr, w_er, b_er, w1, w3, w2, seq):
    t, d = x2d.shape
    pad = LANES - N_EXPERT_GROUPS - N_EXPERTS
    w_router = jnp.concatenate([w_gr, w_er, jnp.zeros((d, pad), F32)], axis=1)
    b_router = jnp.concatenate([b_gr, b_er, jnp.zeros((pad,), F32)]).reshape(1, LANES)
    tm = _tile(seq, 512)
    nt = t // tm
    lr = 2 * tm + N_EXPERTS * MOE_CHUNK
    hs, pos, gates, tab = _router(x2d, mod_l, g_norm2, w_router, b_router, seq, tm, lr)

    cnt8 = tab[:, 0, :N_EXPERTS]
    off8 = tab[:, 1, :N_EXPERTS]
    total = jnp.sum(cnt8, axis=0)
    padded = (total + MOE_BM - 1) // MOE_BM * MOE_BM
    ends = jnp.cumsum(padded)
    base = (ends - padded)[None, :] + jnp.cumsum(cnt8, axis=0) - cnt8
    nb = (2 * t + nt * N_EXPERTS * MOE_CHUNK) // MOE_BM + N_EXPERTS
    block_start = jnp.arange(nb, dtype=I32) * MOE_BM
    block_expert = jnp.minimum(jnp.sum(block_start[:, None] >= ends[None, :], axis=1), N_EXPERTS - 1).astype(I32)
    n_used = (ends[-1] // MOE_BM).astype(I32).reshape(1)
    flat = lambda a: a.reshape(-1).astype(I32)
    off8, n8, base = flat(off8), flat(cnt8 // MOE_CHUNK), flat(base)

    rows = _scatter_rows(off8, n8, base, hs, nb * MOE_BM)
    y = _grouped_ffn(block_expert, n_used, rows, w1, w3, w2)
    return _combine(off8, n8, base, y, pos, gates, x2d, mod_l, seq, tm, lr)


def _seg_matrix(width, segments):
    idx = jnp.arange(width)
    m = jnp.zeros((width, width), F32)
    for start, length in segments:
        inside = (idx >= start) & (idx < start + length)
        m = m + jnp.where(inside[:, None] & inside[None, :], 1.0 / length, 0.0)
    return m.astype(BF16)


def _layout_w_in(w_in):
    depth, d, _ = w_in.shape
    sizes = [MLA_Q_RANK, MLA_KV_RANK, MLA_ROPE, N_DIL_COLS,
             (WIN_HEADS + 2 * WIN_KV_HEADS) * HEAD_DIM, 3 * D_MODEL]
    bounds = [sum(sizes[:k + 1]) for k in range(len(sizes) - 1)]
    c_q, c_kv, k_rope, dil, win, gate = jnp.split(w_in.astype(BF16), bounds, axis=-1)
    win_q, win_kv = win[..., :WIN_HEADS * HEAD_DIM], win[..., WIN_HEADS * HEAD_DIM:]
    z = lambda w: jnp.zeros((depth, d, w), BF16)
    out = jnp.concatenate([gate, c_q, win_q, c_kv, win_kv, z(MLA_NOPE), k_rope,
                           z(N_COLS - OFF_KR - MLA_NOPE - MLA_ROPE)], axis=-1)
    assert out.shape[-1] == N_COLS
    return out, dil


def _layout_mla(w_uq, w_ukv, g_q, g_k):
    qd = MLA_NOPE + MLA_ROPE
    wq = w_uq.reshape(MLA_Q_RANK, MLA_HEADS, qd)
    wq = jnp.pad(wq, ((0, 0), (0, 0), (0, SLOT - qd))).reshape(MLA_Q_RANK, MLA_HEADS * SLOT)
    wkv = w_ukv.reshape(MLA_KV_RANK, MLA_HEADS, MLA_NOPE + MLA_V)
    wk = jnp.pad(wkv[:, :, :MLA_NOPE], ((0, 0), (0, 0), (0, SLOT - MLA_NOPE))).reshape(MLA_KV_RANK, MLA_HEADS * SLOT)
    wv = wkv[:, :, MLA_NOPE:].reshape(MLA_KV_RANK, MLA_HEADS * MLA_V)
    scale = LOG2E * float(qd) ** -0.5
    gq_slot = (jnp.pad(g_q, (0, SLOT - qd)) * scale).reshape(1, SLOT)
    gkn_slot = jnp.pad(g_k[:MLA_NOPE], (0, SLOT - MLA_NOPE)).reshape(1, SLOT)
    gkr_slot = jnp.pad(g_k[MLA_NOPE:], (MLA_NOPE, SLOT - qd)).reshape(1, SLOT)
    return wq.astype(BF16), jnp.concatenate([wk, wv], axis=1).astype(BF16), gq_slot, gkn_slot, gkr_slot


def kernel(x, c, pos, w_ada, b_ada, g_norm1, w_in, g_cq, w_uq, g_ckv, w_ukv, g_q_mla, g_k_mla, g_q_dil, g_k_dil,
           g_q_win, g_k_win, sink_win, w_br_mla, w_br_dil, w_br_win, w_out, g_norm2, w_gr, b_gr, w_er, b_er,
           w1, w3, w2):
    batch, seq, d = x.shape
    depth = w_ada.shape[0]
    t = batch * seq
    half = MLA_ROPE // 2
    inv_freq = ROPE_THETA ** (-jnp.arange(half, dtype=F32) / half)
    invf_slot = jnp.concatenate([jnp.zeros((MLA_NOPE,), F32), inv_freq, inv_freq,
                                 jnp.zeros((SLOT - MLA_NOPE - MLA_ROPE,), F32)]).reshape(1, SLOT)
    cos_t, sin_t = _rope_tables(pos.reshape(t, 1), invf_slot)
    mod = _modulation(c, w_ada, b_ada)
    w_in_k, w_dil_k = _layout_w_in(w_in)
    mseg_slot = _seg_matrix(SLOT, ((0, MLA_NOPE), (MLA_NOPE, MLA_ROPE)))
    mseg128 = _seg_matrix(128, tuple((k * HEAD_DIM, HEAD_DIM) for k in range(2)))
    head_scale = LOG2E * float(HEAD_DIM) ** -0.5
    win_bq = min(128, seq)

    x2d = x.reshape(t, d)
    for l in range(depth):
        mod_l = mod[l].reshape(batch, 6, d)
        g1 = g_norm1[l].reshape(1, d)
        cols = _in_projection(x2d, mod_l, g1, w_in_k[l], seq, BF16, N_COLS // 2, "in_projection")
        dil_cols = _in_projection(x2d, mod_l, g1, w_dil_k[l], seq, F32, N_DIL_COLS // 3, "in_projection_dil")

        wuq, wukv, gq_slot, gkn_slot, gkr_slot = _layout_mla(w_uq[l], w_ukv[l], g_q_mla[l], g_k_mla[l])
        qm, km, vm = _mla_prep(cols, cos_t, sin_t, g_cq[l].reshape(1, -1), g_ckv[l].reshape(1, -1),
                               wuq, wukv, gq_slot, gkn_slot, gkr_slot, mseg_slot)
        o_mla = _mla_attention(qm, km, vm, batch, seq)

        gq_dil = (jnp.tile(g_q_dil[l], 2) * head_scale).reshape(1, -1)
        gk_dil = jnp.tile(g_k_dil[l], 2).reshape(1, -1)
        o_dil, lse_dil = [], []
        for gi in range(DIL_GROUPS):
            o_g, lse_g = _dilated_group(dil_cols, pos, gq_dil, gk_dil, mseg128, gi, batch, seq)
            o_dil.append(o_g)
            lse_dil.append(lse_g)

        gq_win = (jnp.tile(g_q_win[l], 2) * head_scale).reshape(1, -1)
        gk_win = jnp.tile(g_k_win[l], WIN_KV_HEADS).reshape(1, -1)
        sink_row = jnp.repeat(sink_win[l].astype(F32) * LOG2E, win_bq).reshape(1, WIN_HEADS * win_bq)
        o_win = _window_mixer(cols, pos, gq_win, gk_win, sink_row, mseg128, batch, seq)

        x2d = _merge(cols, o_mla, o_dil, lse_dil, o_win, x2d, mod_l, w_br_mla[l].astype(BF16),
                     w_br_dil[l].astype(BF16), w_br_win[l].astype(BF16), w_out[l].astype(BF16), seq)
        x2d = _moe(x2d, mod_l, g_norm2[l].reshape(1, d), w_gr[l], b_gr[l], w_er[l], b_er[l],
                   w1[l], w3[l], w2[l], seq)
    return x2d.reshape(batch, seq, d)
```

```python
import functools
import math

import jax
import jax.numpy as jnp
from jax import lax
from jax.experimental import pallas as pl
from jax.experimental.pallas import tpu as pltpu

F32 = jnp.float32
BF16 = jnp.bfloat16
I32 = jnp.int32

D_MODEL = 1024
HEAD_DIM = 64
NEG_INF = -1e30
EPS = 1e-6
LOG2E = math.log2(math.e)
LN2 = math.log(2.0)
MLA_HEADS = 8
MLA_Q_RANK = 512
MLA_KV_RANK = 256
MLA_NOPE = 64
MLA_ROPE = 32
MLA_V = 64
ROPE_THETA = 10000.0
DIL_PATTERNS = ((128, 1), (512, 4), (2048, 16))
DIL_GROUPS = 3
DIL_HPG = 4
DIL_HEADS = DIL_GROUPS * DIL_HPG
DIL_RADIUS = 64
WIN_HEADS = 8
WIN_KV_HEADS = 2
WIN_RADIUS = 128
N_EXPERT_GROUPS = 4
EXPERTS_PER_GROUP = 8
N_EXPERTS = N_EXPERT_GROUPS * EXPERTS_PER_GROUP
D_EXPERT = 384

LANES = 128
SLOT = 128
VMEM_LIMIT = 48 * 1024 * 1024

OFF_GATE = 0
OFF_CQ = 3072
OFF_WQ = 3584
OFF_CKV = 4096
OFF_WK = 4352
OFF_WV = 4480
OFF_KR = 4608
N_COLS = 4864
N_DIL_COLS = 3 * DIL_HEADS * HEAD_DIM

MOE_BM = 512
MOE_CHUNK = 16
MLA_TQ = 256
MLA_HPS = 8


def _cparams(sem, vmem=None, flags=None):
    return pltpu.CompilerParams(dimension_semantics=sem, vmem_limit_bytes=vmem, flags=flags)


def _tile(n, pref):
    t = min(n, pref)
    assert n % t == 0, (n, pref)
    return t


def _dot(a, b):
    return jnp.dot(a, b, preferred_element_type=F32)


def _dot_nt(a, b):
    return lax.dot_general(a, b, (((1,), (1,)), ((), ())), preferred_element_type=F32)


def _dot_tn(a, b):
    return lax.dot_general(a, b, (((0,), (0,)), ((), ())), preferred_element_type=F32)


def _seg_mean_sq(x, mseg):
    x2 = x * x
    hi = x2.astype(BF16)
    lo = (x2 - hi.astype(F32)).astype(BF16)
    return _dot(hi, mseg) + _dot(lo, mseg)


def _seg_norm(x, mseg, gain):
    return x * lax.rsqrt(_seg_mean_sq(x, mseg) + EPS) * gain


def _rms(x, gain):
    ms = jnp.mean(x * x, axis=-1, keepdims=True)
    return x * lax.rsqrt(ms + EPS) * gain


def _mod_kernel(c_ref, w_ref, b_ref, o_ref):
    c = c_ref[...]
    cond = (c * jax.nn.sigmoid(c)).astype(BF16)
    o_ref[0] = _dot(cond, w_ref[0].astype(BF16)) + b_ref[0]


def _modulation(c, w_ada, b_ada):
    depth, d, n = w_ada.shape
    b = c.shape[0]
    tn = _tile(n, 1536)
    return pl.pallas_call(
        _mod_kernel,
        out_shape=jax.ShapeDtypeStruct((depth, b, n), F32),
        grid=(depth, n // tn),
        in_specs=[
            pl.BlockSpec((b, d), lambda l, j: (0, 0)),
            pl.BlockSpec((1, d, tn), lambda l, j: (l, 0, j)),
            pl.BlockSpec((1, 1, tn), lambda l, j: (l, 0, j)),
        ],
        out_specs=pl.BlockSpec((1, b, tn), lambda l, j: (l, 0, j)),
        compiler_params=_cparams(("parallel", "parallel")),
        name="adaln_mod",
    )(c, w_ada, b_ada.reshape(depth, 1, n))


def _rope_table_kernel(pos_ref, invf_ref, cos_ref, sin_ref):
    ang = pos_ref[...].astype(F32) * invf_ref[...]
    cos_ref[...] = jnp.cos(ang)
    sin_ref[...] = jnp.sin(ang)


def _rope_tables(pos_col, invf_slot):
    t = pos_col.shape[0]
    tm = _tile(t, 1024)
    return pl.pallas_call(
        _rope_table_kernel,
        out_shape=(jax.ShapeDtypeStruct((t, SLOT), F32), jax.ShapeDtypeStruct((t, SLOT), F32)),
        grid=(t // tm,),
        in_specs=[pl.BlockSpec((tm, 1), lambda i: (i, 0)), pl.BlockSpec((1, SLOT), lambda i: (0, 0))],
        out_specs=(pl.BlockSpec((tm, SLOT), lambda i: (i, 0)), pl.BlockSpec((tm, SLOT), lambda i: (i, 0))),
        compiler_params=_cparams(("parallel",)),
        name="rope_tables",
    )(pos_col, invf_slot)


def _inproj_kernel(x_ref, mod_ref, g_ref, w_ref, o_ref, h_scr):
    @pl.when(pl.program_id(1) == 0)
    def _():
        m = mod_ref[0]
        h = _rms(x_ref[...], g_ref[...]) * (1.0 + m[1:2]) + m[0:1]
        h_scr[...] = h.astype(BF16)

    o_ref[...] = _dot(h_scr[...], w_ref[...]).astype(o_ref.dtype)


def _in_projection(x2d, mod_l, g_norm, w_in_l, seq, out_dtype, tn, name):
    t, d = x2d.shape
    nc = w_in_l.shape[1]
    tm = _tile(seq, 1024)
    assert nc % tn == 0
    per_b = seq // tm
    return pl.pallas_call(
        _inproj_kernel,
        out_shape=jax.ShapeDtypeStruct((t, nc), out_dtype),
        grid=(t // tm, nc // tn),
        in_specs=[
            pl.BlockSpec((tm, d), lambda i, j: (i, 0)),
            pl.BlockSpec((1, 6, d), lambda i, j: (i // per_b, 0, 0)),
            pl.BlockSpec((1, d), lambda i, j: (0, 0)),
            pl.BlockSpec((d, tn), lambda i, j: (0, j)),
        ],
        out_specs=pl.BlockSpec((tm, tn), lambda i, j: (i, j)),
        scratch_shapes=[pltpu.VMEM((tm, d), BF16)],
        compiler_params=_cparams(("parallel", "arbitrary"), VMEM_LIMIT),
        name=name,
    )(x2d, mod_l, g_norm, w_in_l)


def _mla_prep_kernel(cq_ref, ckv_ref, kr_ref, cos_ref, sin_ref, gcq_ref, gckv_ref, wuq_ref, wukv_ref,
                     gq_ref, gkn_ref, gkr_ref, mseg_ref, q_out, k_out, v_out):
    cos = cos_ref[...]
    sin = sin_ref[...]
    lane = lax.broadcasted_iota(I32, (1, SLOT), 1)
    s_neg = jnp.where((lane >= 64) & (lane < 80), -sin, 0.0)
    s_pos = jnp.where((lane >= 80) & (lane < 96), sin, 0.0)
    mseg = mseg_ref[...]

    def rope(xn):
        return xn * cos + pltpu.roll(xn, SLOT - 16, 1) * s_neg + pltpu.roll(xn, 16, 1) * s_pos

    cqn = _rms(cq_ref[...].astype(F32), gcq_ref[...]).astype(BF16)
    q = _dot(cqn, wuq_ref[...])
    for h in range(MLA_HEADS):
        sl = slice(h * SLOT, (h + 1) * SLOT)
        q_out[:, sl] = rope(_seg_norm(q[:, sl], mseg, gq_ref[...])).astype(q_out.dtype)

    ckvn = _rms(ckv_ref[...].astype(F32), gckv_ref[...]).astype(BF16)
    kv = _dot(ckvn, wukv_ref[...])
    kr = rope(_seg_norm(kr_ref[...].astype(F32), mseg, gkr_ref[...]))
    for h in range(MLA_HEADS):
        sl = slice(h * SLOT, (h + 1) * SLOT)
        k_out[:, sl] = (_seg_norm(kv[:, sl], mseg, gkn_ref[...]) + kr).astype(k_out.dtype)
    v_out[...] = kv[:, MLA_HEADS * SLOT:].astype(v_out.dtype)


def _mla_prep(cols, cos_t, sin_t, gcq, gckv, wuq, wukv, gq_slot, gkn_slot, gkr_slot, mseg):
    t = cols.shape[0]
    tm = _tile(t, 512)
    hs = MLA_HEADS * SLOT
    full = lambda shape: pl.BlockSpec(shape, lambda i: (0,) * len(shape))
    return pl.pallas_call(
        _mla_prep_kernel,
        out_shape=(jax.ShapeDtypeStruct((t, hs), BF16), jax.ShapeDtypeStruct((t, hs), BF16),
                   jax.ShapeDtypeStruct((t, MLA_HEADS * MLA_V), BF16)),
        grid=(t // tm,),
        in_specs=[
            pl.BlockSpec((tm, MLA_Q_RANK), lambda i: (i, OFF_CQ // MLA_Q_RANK)),
            pl.BlockSpec((tm, MLA_KV_RANK), lambda i: (i, OFF_CKV // MLA_KV_RANK)),
            pl.BlockSpec((tm, SLOT), lambda i: (i, OFF_KR // SLOT)),
            pl.BlockSpec((tm, SLOT), lambda i: (i, 0)),
            pl.BlockSpec((tm, SLOT), lambda i: (i, 0)),
            full((1, MLA_Q_RANK)), full((1, MLA_KV_RANK)),
            full((MLA_Q_RANK, hs)), full((MLA_KV_RANK, hs + MLA_HEADS * MLA_V)),
            full((1, SLOT)), full((1, SLOT)), full((1, SLOT)), full((SLOT, SLOT)),
        ],
        out_specs=(pl.BlockSpec((tm, hs), lambda i: (i, 0)), pl.BlockSpec((tm, hs), lambda i: (i, 0)),
                   pl.BlockSpec((tm, MLA_HEADS * MLA_V), lambda i: (i, 0))),
        compiler_params=_cparams(("parallel",), VMEM_LIMIT),
        name="mla_prep",
    )(cols, cols, cols, cos_t, sin_t, gcq, gckv, wuq, wukv, gq_slot, gkn_slot, gkr_slot, mseg)


def _mla_attn_kernel(q_ref, k_ref, v_ref, o_ref):
    lane = lax.broadcasted_iota(I32, (1, 2 * MLA_V), 1)
    for pair in range(MLA_HPS // 2):
        outs = []
        vp = v_ref[0, :, pair * 2 * MLA_V:(pair + 1) * 2 * MLA_V]
        for a in range(2):
            sl = slice((2 * pair + a) * SLOT, (2 * pair + a + 1) * SLOT)
            s = _dot_nt(q_ref[0, :, sl], k_ref[0, :, sl])
            m = jnp.max(s, axis=-1, keepdims=True)
            p = jnp.exp2(s - m)
            l = jnp.sum(p, axis=-1, keepdims=True)
            outs.append(_dot(p.astype(BF16), vp) / l)
        o_ref[0, :, pair * 2 * MLA_V:(pair + 1) * 2 * MLA_V] = jnp.where(lane < MLA_V, outs[0], outs[1]).astype(o_ref.dtype)


def _mla_attention(qm, km, vm, batch, seq):
    hs = MLA_HEADS * SLOT
    q3 = qm.reshape(batch, seq, hs)
    k3 = km.reshape(batch, seq, hs)
    v3 = vm.reshape(batch, seq, MLA_HEADS * MLA_V)
    tq = _tile(seq, MLA_TQ)
    out = pl.pallas_call(
        _mla_attn_kernel,
        out_shape=jax.ShapeDtypeStruct((batch, seq, MLA_HEADS * MLA_V), BF16),
        grid=(batch, MLA_HEADS // MLA_HPS, seq // tq),
        in_specs=[
            pl.BlockSpec((1, tq, MLA_HPS * SLOT), lambda b, p, i: (b, i, p)),
            pl.BlockSpec((1, seq, MLA_HPS * SLOT), lambda b, p, i: (b, 0, p)),
            pl.BlockSpec((1, seq, MLA_HPS * MLA_V), lambda b, p, i: (b, 0, p)),
        ],
        out_specs=pl.BlockSpec((1, tq, MLA_HPS * MLA_V), lambda b, p, i: (b, i, p)),
        compiler_params=_cparams(("parallel", "parallel", "arbitrary"), VMEM_LIMIT),
        name="mla_attention",
    )(q3, k3, v3)
    return out.reshape(batch * seq, MLA_HEADS * MLA_V)


WIN_BLOCKS_PER_STEP = 4
BAND_UNROLL = 16


def _window_start(i, bq, radius, n, kw):
    ws = jnp.clip(i * bq - radius, 0, n - kw)
    return pl.multiple_of(ws, 16)


MASKED_DIST = 1e30


def _masked_dist_t(i, bq, ws, kw, radius, pos_keys, pos_queries):
    keys = lax.broadcasted_iota(I32, (kw, bq), 0)
    queries = lax.broadcasted_iota(I32, (kw, bq), 1)
    rel = queries - keys + (i * bq - ws)
    d = pos_keys - pos_queries
    return jnp.where(jnp.maximum(rel, -rel) <= radius, jnp.maximum(d, -d).astype(F32), MASKED_DIST)


def _sub_positions(pos, batch, n, dil, bq):
    pos_sub = jnp.transpose(pos.reshape(batch, n, dil), (0, 2, 1))
    return pos_sub.reshape(batch, dil * n, 1), pos_sub.reshape(batch, dil * (n // bq), 1, bq)


def _dil_kernel(q0_ref, q1_ref, k0_ref, k1_ref, v0_ref, v1_ref, pcol_ref, prow_ref, gq_ref, gk_ref, mseg_ref,
                o_ref, lse_ref, kn_scr, vn_scr, *, n, bq, kw, dil, slopes):
    q_refs, k_refs, v_refs = (q0_ref, q1_ref), (k0_ref, k1_ref), (v0_ref, v1_ref)
    nblk = n // bq
    mseg = mseg_ref[...]
    gq = gq_ref[...]
    low_lane = lax.broadcasted_iota(I32, (1, 2 * HEAD_DIM), 1) < HEAD_DIM

    def rows(start, size):
        return pl.ds(start, size) if dil == 1 else pl.ds(start, size, stride=dil)

    def fill(r, carry):
        base = pl.multiple_of(r * n, 16)
        for hf in range(2):
            kn_scr[hf, pl.ds(base, n), :] = _seg_norm(k_refs[hf][0, rows(r, n), :], mseg, gk_ref[...]).astype(BF16)
            vn_scr[hf, pl.ds(base, n), :] = v_refs[hf][0, rows(r, n), :].astype(BF16)
        return carry

    lax.fori_loop(0, dil, fill, 0)

    def query_block(b, carry):
        r = b >> (nblk.bit_length() - 1)
        i = b & (nblk - 1)
        ws = _window_start(i, bq, DIL_RADIUS, n, kw)
        start = pl.multiple_of(r * n + ws, 16)
        q_rows = rows(r + i * (bq * dil), bq)
        dist = _masked_dist_t(i, bq, ws, kw, DIL_RADIUS, pcol_ref[0, pl.ds(start, kw), :], prow_ref[0, r * nblk + i])
        s_halves = []
        for hf in range(2):
            qn = _seg_norm(q_refs[hf][0, q_rows, :], mseg, gq)
            qs = jnp.concatenate([jnp.where(low_lane, qn, 0.0), jnp.where(low_lane, 0.0, qn)], axis=0).astype(BF16)
            bias = jnp.concatenate([slopes[2 * hf] * dist, slopes[2 * hf + 1] * dist], axis=1)
            s_halves.append(_dot_nt(kn_scr[hf, pl.ds(start, kw), :], qs) - bias)
        s = jnp.concatenate(s_halves, axis=1)
        m = jnp.max(s, axis=0, keepdims=True)
        p = jnp.exp2(s - m)
        l = jnp.sum(p, axis=0, keepdims=True)
        lse = (m + jnp.log2(l)) * LN2
        pb = p.astype(BF16)
        for hf in range(2):
            c0, c1, c2 = 2 * hf * bq, (2 * hf + 1) * bq, (2 * hf + 2) * bq
            ot = _dot_tn(vn_scr[hf, pl.ds(start, kw), :], pb[:, c0:c2]) / l[:, c0:c2]
            o_ref[0, hf, q_rows, :] = jnp.concatenate([ot[:HEAD_DIM, :bq], ot[HEAD_DIM:, bq:]], axis=0).T
            lse_ref[0, hf, q_rows, :] = jnp.concatenate([jnp.broadcast_to(lse[:, c0:c1], (HEAD_DIM, bq)),
                                                         jnp.broadcast_to(lse[:, c1:c2], (HEAD_DIM, bq))], axis=0).T
        return carry

    lax.fori_loop(0, dil * nblk, query_block, 0, unroll=min(dil * nblk, BAND_UNROLL))


def _dilated_group(dil_cols, pos, gq_row, gk_row, mseg, gi, batch, seq):
    _, dil = DIL_PATTERNS[gi]
    n = seq // dil
    bq = min(128, n)
    kw = min(bq + 2 * DIL_RADIUS, n)
    nblk = n // bq
    hw = 2 * HEAD_DIM
    cols3 = dil_cols.reshape(batch, seq, N_DIL_COLS)
    pcol, prow = _sub_positions(pos, batch, n, dil, bq)
    slopes = tuple(float(LOG2E * 2.0 ** (-8.0 * (gi * DIL_HPG + hh + 1) / DIL_HEADS)) for hh in range(DIL_HPG))
    kern = functools.partial(_dil_kernel, n=n, bq=bq, kw=kw, dil=dil, slopes=slopes)
    col = lambda which, hf: pl.BlockSpec((1, seq, hw), lambda b: (b, 0, (which * DIL_GROUPS + gi) * 2 + hf))
    small = lambda shape: pl.BlockSpec(shape, lambda b: (0,) * len(shape))
    halves = jax.ShapeDtypeStruct((batch, 2, seq, hw), F32)
    return pl.pallas_call(
        kern,
        out_shape=(halves, halves),
        grid=(batch,),
        in_specs=[
            col(0, 0), col(0, 1), col(1, 0), col(1, 1), col(2, 0), col(2, 1),
            pl.BlockSpec((1, seq, 1), lambda b: (b, 0, 0)),
            pl.BlockSpec((1, dil * nblk, 1, bq), lambda b: (b, 0, 0, 0)),
            small((1, hw)), small((1, hw)), small((hw, hw)),
        ],
        out_specs=(pl.BlockSpec((1, 2, seq, hw), lambda b: (b, 0, 0, 0)),
                   pl.BlockSpec((1, 2, seq, hw), lambda b: (b, 0, 0, 0))),
        scratch_shapes=[pltpu.VMEM((2, seq, hw), BF16), pltpu.VMEM((2, seq, hw), BF16)],
        compiler_params=_cparams(("parallel",), VMEM_LIMIT),
        name=f"dilated_group{gi}",
    )(cols3, cols3, cols3, cols3, cols3, cols3, pcol, prow, gq_row, gk_row, mseg)


def _win_kernel(q_ref, k_ref, v_ref, pcol_ref, prow_ref, gq_ref, gk_ref, sink_ref, mseg_ref, o_ref, kn_scr,
                *, n, bq, kw, slopes):
    step = pl.program_id(1)
    mseg = mseg_ref[...]

    @pl.when(step == 0)
    def _():
        kn_scr[...] = _seg_norm(k_ref[0].astype(F32), mseg, gk_ref[...]).astype(BF16)

    low_lane = lax.broadcasted_iota(I32, (1, 2 * HEAD_DIM), 1) < HEAD_DIM
    rep = WIN_HEADS // WIN_KV_HEADS
    sink = sink_ref[...]
    per_step = q_ref.shape[1] // bq
    for j in range(per_step):
        i = step * per_step + j
        rows = slice(j * bq, (j + 1) * bq)
        ws = _window_start(i, bq, WIN_RADIUS, n, kw)
        kwin = kn_scr[pl.ds(ws, kw), :]
        vwin = v_ref[0, pl.ds(ws, kw), :]
        dist = _masked_dist_t(i, bq, ws, kw, WIN_RADIUS, pcol_ref[0, pl.ds(ws, kw), :], prow_ref[0, j])

        qn_pairs = [_seg_norm(q_ref[0, rows, pair * 128:(pair + 1) * 128].astype(F32), mseg, gq_ref[...])
                    for pair in range(WIN_HEADS // 2)]
        q_parts = []
        for h in range(WIN_HEADS):
            pair, upper = divmod(h, 2)
            qm = jnp.where(low_lane, 0.0, qn_pairs[pair]) if upper else jnp.where(low_lane, qn_pairs[pair], 0.0)
            if upper != h // rep:
                qm = pltpu.roll(qm, HEAD_DIM, 1)
            q_parts.append(qm.astype(BF16))
        qs = jnp.concatenate(q_parts, axis=0)
        bias = jnp.concatenate([slopes[h] * dist for h in range(WIN_HEADS)], axis=1)
        s = _dot_nt(kwin, qs) - bias
        m = jnp.maximum(jnp.max(s, axis=0, keepdims=True), sink)
        e = jnp.exp2(s - m)
        den = jnp.sum(e, axis=0, keepdims=True) + jnp.exp2(sink - m)
        ot = _dot_tn(vwin, e.astype(BF16)) / den
        for pair in range(WIN_HEADS // 2):
            slabs = []
            for h in (2 * pair, 2 * pair + 1):
                g = h // rep
                slabs.append(ot[g * HEAD_DIM:(g + 1) * HEAD_DIM, h * bq:(h + 1) * bq])
            o_ref[0, rows, pair * 128:(pair + 1) * 128] = jnp.concatenate(slabs, axis=0).T.astype(o_ref.dtype)


def _window_mixer(cols, pos, gq_row, gk_row, sink_row, mseg128, batch, seq):
    n = seq
    bq = min(128, n)
    kw = min(bq + 2 * WIN_RADIUS, n)
    nblk = n // bq
    qw = WIN_HEADS * HEAD_DIM
    kvw = WIN_KV_HEADS * HEAD_DIM
    cols_v = cols.reshape(batch, seq, N_COLS)
    pcol = pos.reshape(batch, seq, 1)
    prow = pos.reshape(batch, nblk, 1, bq)
    slopes = tuple(float(LOG2E * 2.0 ** (-8.0 * (h + 1) / WIN_HEADS)) for h in range(WIN_HEADS))
    kern = functools.partial(_win_kernel, n=n, bq=bq, kw=kw, slopes=slopes)
    small = lambda shape: pl.BlockSpec(shape, lambda b, i: (0,) * len(shape))
    per_step = min(WIN_BLOCKS_PER_STEP, nblk)
    o = pl.pallas_call(
        kern,
        out_shape=jax.ShapeDtypeStruct((batch, seq, qw), BF16),
        grid=(batch, nblk // per_step),
        in_specs=[
            pl.BlockSpec((1, per_step * bq, qw), lambda b, i: (b, i, OFF_WQ // qw)),
            pl.BlockSpec((1, n, kvw), lambda b, i: (b, 0, OFF_WK // kvw)),
            pl.BlockSpec((1, n, kvw), lambda b, i: (b, 0, OFF_WV // kvw)),
            pl.BlockSpec((1, n, 1), lambda b, i: (b, 0, 0)),
            pl.BlockSpec((1, per_step, 1, bq), lambda b, i: (b, i, 0, 0)),
            small((1, kvw)), small((1, kvw)), small((1, WIN_HEADS * bq)), small((kvw, kvw)),
        ],
        out_specs=pl.BlockSpec((1, per_step * bq, qw), lambda b, i: (b, i, 0)),
        scratch_shapes=[pltpu.VMEM((n, kvw), BF16)],
        compiler_params=_cparams(("parallel", "arbitrary"), VMEM_LIMIT),
        name="window_mixer",
    )(cols_v, cols_v, cols_v, pcol, prow, gq_row, gk_row, sink_row, mseg128)
    return o.reshape(batch * seq, qw)


def _merge_kernel(gm_ref, gd_ref, gw_ref, om_ref, od0_ref, od1_ref, od2_ref, l0_ref, l1_ref, l2_ref, ow_ref,
                  x_ref, mod_ref, wm_ref, wd_ref, ww_ref, wo_ref, o_ref):
    def heads(ref):
        return jnp.concatenate([ref[0, 0], ref[0, 1]], axis=1)

    l0, l1, l2 = heads(l0_ref), heads(l1_ref), heads(l2_ref)
    m = jnp.maximum(jnp.maximum(l0, l1), l2)
    e0, e1, e2 = jnp.exp(l0 - m), jnp.exp(l1 - m), jnp.exp(l2 - m)
    od = (e0 * heads(od0_ref) + e1 * heads(od1_ref) + e2 * heads(od2_ref)) / (e0 + e1 + e2)
    y = jax.nn.sigmoid(gm_ref[...].astype(F32)) * _dot(om_ref[...], wm_ref[...])
    y += jax.nn.sigmoid(gd_ref[...].astype(F32)) * _dot(od.astype(BF16), wd_ref[...])
    y += jax.nn.sigmoid(gw_ref[...].astype(F32)) * _dot(ow_ref[...], ww_ref[...])
    z = _dot(y.astype(BF16), wo_ref[...])
    o_ref[...] = x_ref[...] + mod_ref[0][2:3] * z


def _merge(cols, o_mla, o_dil, lse_dil, o_win, x2d, mod_l, wm, wd, ww, wo, seq):
    t, d = x2d.shape
    tm = _tile(seq, 512)
    per_b = seq // tm
    row = lambda w, j=0: pl.BlockSpec((tm, w), lambda i: (i, j))
    full = lambda a: pl.BlockSpec(a.shape, lambda i: (0, 0))
    dil = pl.BlockSpec((1, 2, tm, 2 * HEAD_DIM), lambda i: (i // per_b, 0, i % per_b, 0))
    return pl.pallas_call(
        _merge_kernel,
        out_shape=jax.ShapeDtypeStruct((t, d), F32),
        grid=(t // tm,),
        in_specs=[row(d, 0), row(d, 1), row(d, 2), row(o_mla.shape[1]),
                  dil, dil, dil, dil, dil, dil, row(o_win.shape[1]),
                  row(d), pl.BlockSpec((1, 6, d), lambda i: (i // per_b, 0, 0)),
                  full(wm), full(wd), full(ww), full(wo)],
        out_specs=row(d),
        compiler_params=_cparams(("parallel",), VMEM_LIMIT),
        name="merge_out_proj",
    )(cols, cols, cols, o_mla, o_dil[0], o_dil[1], o_dil[2], lse_dil[0], lse_dil[1], lse_dil[2], o_win,
      x2d, mod_l, wm, wd, ww, wo)


def _router_kernel(x_ref, mod_ref, g_ref, whi_ref, wlo_ref, br_ref, upper_ref, hs_ref, pos_ref, gate_ref, tab_ref,
                   *, lr):
    m = mod_ref[0]
    h = _rms(x_ref[...], g_ref[...]) * (1.0 + m[4:5]) + m[3:4]
    tm = h.shape[0]
    h_hi = h.astype(BF16)
    h_lo = (h - h_hi.astype(F32)).astype(BF16)
    logits = _dot(h_hi, whi_ref[...]) + (_dot(h_hi, wlo_ref[...]) + _dot(h_lo, whi_ref[...])) + br_ref[...]
    lane = lax.broadcasted_iota(I32, (tm, LANES), 1)
    lane_f = lane.astype(F32)

    def first_argmax(vals, mx):
        return jnp.min(jnp.where(vals == mx, lane_f, float(LANES)), axis=-1, keepdims=True).astype(I32)

    lg = jnp.where(lane < N_EXPERT_GROUPS, logits, NEG_INF)
    mg = jnp.max(lg, axis=-1, keepdims=True)
    g_w = 1.0 / jnp.sum(jnp.exp(lg - mg), axis=-1, keepdims=True)
    g_idx = first_argmax(lg, mg)
    eid = lane - N_EXPERT_GROUPS
    in_grp = (eid >= 0) & (eid < N_EXPERTS) & ((eid >> 3) == g_idx)
    le = jnp.where(in_grp, logits, NEG_INF)
    m1 = jnp.max(le, axis=-1, keepdims=True)
    i1 = first_argmax(le, m1)
    le2 = jnp.where(lane == i1, NEG_INF, le)
    m2 = jnp.max(le2, axis=-1, keepdims=True)
    i2 = first_argmax(le2, m2)
    r = jnp.exp(m2 - m1)
    gate1 = g_w / (1.0 + r)
    gate2 = g_w * r / (1.0 + r)
    e1 = i1 - N_EXPERT_GROUPS
    e2 = i2 - N_EXPERT_GROUPS
    hit1 = lane == e1
    hit2 = lane == e2
    onehot = jnp.where(hit1 | hit2, 1.0, 0.0)
    rows = lax.broadcasted_iota(I32, (tm, tm), 0)
    cols = lax.broadcasted_iota(I32, (tm, tm), 1)
    before = jnp.where(rows > cols, 1.0, 0.0).astype(BF16)
    rank = _dot(before, onehot.astype(BF16))
    cnt8 = (jnp.sum(onehot, axis=0, keepdims=True).astype(I32) + (MOE_CHUNK - 1)) & jnp.int32(-MOE_CHUNK)
    off8 = _dot(jnp.broadcast_to(cnt8.astype(F32), (8, LANES)).astype(BF16), upper_ref[...])[0:1]
    pos1 = jnp.sum(jnp.where(hit1, rank + off8, 0.0), axis=-1, keepdims=True).astype(I32)
    pos2 = jnp.sum(jnp.where(hit2, rank + off8, 0.0), axis=-1, keepdims=True).astype(I32)
    pos_lanes = jnp.where(lane == 0, pos1, jnp.where(lane == 1, pos2, 0))
    pos_rows = pos_lanes.astype(F32).T.astype(I32)
    local = lax.broadcasted_iota(I32, (lr, tm), 0)
    place = jnp.where((local == pos_rows[0:1]) | (local == pos_rows[1:2]), 1.0, 0.0).astype(BF16)
    hs_ref[0] = _dot(place, h_hi).astype(BF16)
    pos_ref[...] = pos_lanes
    gate_ref[...] = jnp.where(lane == 0, gate1, jnp.where(lane == 1, gate2, 0.0))
    sub = lax.broadcasted_iota(I32, (8, LANES), 0)
    tab_ref[0] = jnp.where(sub == 0, cnt8, jnp.where(sub == 1, off8.astype(I32), 0))


def _router(x2d, mod_l, g_norm, w_router, b_router, seq, tm, lr):
    t, d = x2d.shape
    per_b = seq // tm
    nt = t // tm
    row = lambda w: pl.BlockSpec((tm, w), lambda i: (i, 0))
    idx = jnp.arange(LANES)
    upper = jnp.where(idx[:, None] < idx[None, :], 1.0, 0.0).astype(BF16)
    w_hi = w_router.astype(BF16)
    w_lo = (w_router - w_hi.astype(F32)).astype(BF16)
    return pl.pallas_call(
        functools.partial(_router_kernel, lr=lr),
        out_shape=(jax.ShapeDtypeStruct((nt, lr, d), BF16), jax.ShapeDtypeStruct((t, LANES), I32),
                   jax.ShapeDtypeStruct((t, LANES), F32), jax.ShapeDtypeStruct((nt, 8, LANES), I32)),
        grid=(nt,),
        in_specs=[row(d), pl.BlockSpec((1, 6, d), lambda i: (i // per_b, 0, 0)),
                  pl.BlockSpec((1, d), lambda i: (0, 0)), pl.BlockSpec((d, LANES), lambda i: (0, 0)),
                  pl.BlockSpec((d, LANES), lambda i: (0, 0)),
                  pl.BlockSpec((1, LANES), lambda i: (0, 0)), pl.BlockSpec((LANES, LANES), lambda i: (0, 0))],
        out_specs=(pl.BlockSpec((1, lr, d), lambda i: (i, 0, 0)), row(LANES), row(LANES),
                   pl.BlockSpec((1, 8, LANES), lambda i: (i, 0, 0))),
        compiler_params=_cparams(("parallel",), VMEM_LIMIT),
        name="moe_router",
    )(x2d, mod_l, g_norm, w_hi, w_lo, b_router, upper)


def _for_each_chunk(off_ref, n_ref, base_ref, tile, fn):
    def per_expert(e, carry):
        k = tile * N_EXPERTS + e
        off, base = off_ref[k], base_ref[k]

        def per_chunk(c, carry2):
            fn(pl.multiple_of(off + c * MOE_CHUNK, MOE_CHUNK), pl.multiple_of(base + c * MOE_CHUNK, MOE_CHUNK))
            return carry2

        return lax.fori_loop(0, n_ref[k], per_chunk, carry)

    lax.fori_loop(0, N_EXPERTS, per_expert, 0)


def _scatter_kernel(off_ref, n_ref, base_ref, hs_ref, rows_in_ref, rows_ref, sem):
    del rows_in_ref
    tile = pl.program_id(0)

    def chunk_copy(local, glob):
        return pltpu.make_async_copy(hs_ref.at[0, pl.ds(local, MOE_CHUNK)], rows_ref.at[pl.ds(glob, MOE_CHUNK)], sem)

    _for_each_chunk(off_ref, n_ref, base_ref, tile, lambda a, b: chunk_copy(a, b).start())
    _for_each_chunk(off_ref, n_ref, base_ref, tile, lambda a, b: chunk_copy(a, b).wait())


def _scatter_rows(off8, n8, base, hs, n_rows):
    nt, lr, w = hs.shape
    zeros = jnp.zeros((n_rows, w), hs.dtype)
    grid_spec = pltpu.PrefetchScalarGridSpec(
        num_scalar_prefetch=3,
        grid=(nt,),
        in_specs=[pl.BlockSpec((1, lr, w), lambda i, *_: (i, 0, 0)), pl.BlockSpec(memory_space=pl.ANY)],
        out_specs=pl.BlockSpec(memory_space=pl.ANY),
        scratch_shapes=[pltpu.SemaphoreType.DMA],
    )
    return pl.pallas_call(
        _scatter_kernel,
        out_shape=jax.ShapeDtypeStruct((n_rows, w), hs.dtype),
        grid_spec=grid_spec,
        input_output_aliases={4: 0},
        compiler_params=_cparams(("arbitrary",)),
        name="moe_scatter",
    )(off8, n8, base, hs, zeros)


def _ffn_kernel(be_ref, nu_ref, rows_ref, w1_ref, w3_ref, w2_ref, y_ref, w13_scr, w2_scr):
    j = pl.program_id(0)
    used = j < nu_ref[0]
    new_expert = (j == 0) | (be_ref[j] != be_ref[jnp.maximum(j - 1, 0)])

    @pl.when(used & new_expert)
    def _():
        w13_scr[:, :D_EXPERT] = w1_ref[0].astype(BF16)
        w13_scr[:, D_EXPERT:] = w3_ref[0].astype(BF16)
        w2_scr[...] = w2_ref[0].astype(BF16)

    @pl.when(used)
    def _():
        h = _dot(rows_ref[...], w13_scr[...])
        a = h[:, :D_EXPERT]
        act = a * jax.nn.sigmoid(a) * h[:, D_EXPERT:]
        y_ref[...] = _dot(act.astype(BF16), w2_scr[...]).astype(y_ref.dtype)

    @pl.when(jnp.logical_not(used))
    def _():
        y_ref[...] = jnp.zeros_like(y_ref)


def _grouped_ffn(block_expert, n_used, rows, w1, w3, w2):
    n_rows, w = rows.shape
    d = w1.shape[1]
    nb = n_rows // MOE_BM
    grid_spec = pltpu.PrefetchScalarGridSpec(
        num_scalar_prefetch=2,
        grid=(nb,),
        in_specs=[
            pl.BlockSpec((MOE_BM, w), lambda j, be, nu: (j, 0)),
            pl.BlockSpec((1, d, D_EXPERT), lambda j, be, nu: (be[j], 0, 0)),
            pl.BlockSpec((1, d, D_EXPERT), lambda j, be, nu: (be[j], 0, 0)),
            pl.BlockSpec((1, D_EXPERT, d), lambda j, be, nu: (be[j], 0, 0)),
        ],
        out_specs=pl.BlockSpec((MOE_BM, w), lambda j, be, nu: (j, 0)),
        scratch_shapes=[pltpu.VMEM((d, 2 * D_EXPERT), BF16), pltpu.VMEM((D_EXPERT, d), BF16)],
    )
    return pl.pallas_call(
        _ffn_kernel,
        out_shape=jax.ShapeDtypeStruct((n_rows, w), BF16),
        grid_spec=grid_spec,
        compiler_params=_cparams(("arbitrary",), VMEM_LIMIT),
        name="moe_grouped_ffn",
    )(block_expert, n_used, rows, w1, w3, w2)


def _combine_kernel(off_ref, n_ref, base_ref, y_ref, pos_ref, gate_ref, x_ref, mod_ref, o_ref, ybuf, sem):
    tile = pl.program_id(0)
    slot = tile % 2

    def chunk_copy(s, local, glob):
        return pltpu.make_async_copy(y_ref.at[pl.ds(glob, MOE_CHUNK)], ybuf.at[s, pl.ds(local, MOE_CHUNK)], sem.at[s])

    def request(t, s):
        ybuf[s] = jnp.zeros(ybuf.shape[1:], ybuf.dtype)
        _for_each_chunk(off_ref, n_ref, base_ref, t, lambda a, b: chunk_copy(s, a, b).start())

    @pl.when(tile == 0)
    def _():
        request(tile, slot)

    @pl.when(tile + 1 < pl.num_programs(0))
    def _():
        request(tile + 1, 1 - slot)

    _for_each_chunk(off_ref, n_ref, base_ref, tile, lambda a, b: chunk_copy(slot, a, b).wait())
    tm = x_ref.shape[0]
    pos, g = pos_ref[...], gate_ref[...]
    local = lax.broadcasted_iota(I32, (tm, ybuf.shape[1]), 1)
    pick = (jnp.where(local == pos[:, 0:1], g[:, 0:1], 0.0)
            + jnp.where(local == pos[:, 1:2], g[:, 1:2], 0.0)).astype(BF16)
    moe = _dot(pick, ybuf[slot])
    o_ref[...] = x_ref[...] + mod_ref[0][5:6] * moe


def _combine(off8, n8, base, y, pos, gates, x2d, mod_l, seq, tm, lr):
    t, d = x2d.shape
    per_b = seq // tm
    grid_spec = pltpu.PrefetchScalarGridSpec(
        num_scalar_prefetch=3,
        grid=(t // tm,),
        in_specs=[pl.BlockSpec(memory_space=pl.ANY),
                  pl.BlockSpec((tm, LANES), lambda i, *_: (i, 0)), pl.BlockSpec((tm, LANES), lambda i, *_: (i, 0)),
                  pl.BlockSpec((tm, d), lambda i, *_: (i, 0)),
                  pl.BlockSpec((1, 6, d), lambda i, *_: (i // per_b, 0, 0))],
        out_specs=pl.BlockSpec((tm, d), lambda i, *_: (i, 0)),
        scratch_shapes=[pltpu.VMEM((2, lr, y.shape[1]), y.dtype), pltpu.SemaphoreType.DMA((2,))],
    )
    return pl.pallas_call(
        _combine_kernel,
        out_shape=jax.ShapeDtypeStruct((t, d), F32),
        grid_spec=grid_spec,
        compiler_params=_cparams(("arbitrary",), VMEM_LIMIT),
        name="moe_combine",
    )(off8, n8, base, y, pos, gates, x2d, mod_l)


def _moe(x2d, mod_l, g_norm2, w_gr, b_gr, w_er, b_er, w1, w3, w2, seq):
    t, d = x2d.shape
    pad = LANES - N_EXPERT_GROUPS - N_EXPERTS
    w_router = jnp.concatenate([w_gr, w_er, jnp.zeros((d, pad), F32)], axis=1)
    b_router = jnp.concatenate([b_gr, b_er, jnp.zeros((pad,), F32)]).reshape(1, LANES)
    tm = _tile(seq, 512)
    nt = t // tm
    lr = 2 * tm + N_EXPERTS * MOE_CHUNK
    hs, pos, gates, tab = _router(x2d, mod_l, g_norm2, w_router, b_router, seq, tm, lr)

    cnt8 = tab[:, 0, :N_EXPERTS]
    off8 = tab[:, 1, :N_EXPERTS]
    total = jnp.sum(cnt8, axis=0)
    padded = (total + MOE_BM - 1) // MOE_BM * MOE_BM
    ends = jnp.cumsum(padded)
    base = (ends - padded)[None, :] + jnp.cumsum(cnt8, axis=0) - cnt8
    nb = (2 * t + nt * N_EXPERTS * MOE_CHUNK) // MOE_BM + N_EXPERTS
    block_start = jnp.arange(nb, dtype=I32) * MOE_BM
    block_expert = jnp.minimum(jnp.sum(block_start[:, None] >= ends[None, :], axis=1), N_EXPERTS - 1).astype(I32)
    n_used = (ends[-1] // MOE_BM).astype(I32).reshape(1)
    flat = lambda a: a.reshape(-1).astype(I32)
    off8, n8, base = flat(off8), flat(cnt8 // MOE_CHUNK), flat(base)

    rows = _scatter_rows(off8, n8, base, hs, nb * MOE_BM)
    y = _grouped_ffn(block_expert, n_used, rows, w1, w3, w2)
    return _combine(off8, n8, base, y, pos, gates, x2d, mod_l, seq, tm, lr)


def _seg_matrix(width, segments):
    idx = jnp.arange(width)
    m = jnp.zeros((width, width), F32)
    for start, length in segments:
        inside = (idx >= start) & (idx < start + length)
        m = m + jnp.where(inside[:, None] & inside[None, :], 1.0 / length, 0.0)
    return m.astype(BF16)


def _layout_w_in(w_in):
    depth, d, _ = w_in.shape
    sizes = [MLA_Q_RANK, MLA_KV_RANK, MLA_ROPE, N_DIL_COLS,
             (WIN_HEADS + 2 * WIN_KV_HEADS) * HEAD_DIM, 3 * D_MODEL]
    bounds = [sum(sizes[:k + 1]) for k in range(len(sizes) - 1)]
    c_q, c_kv, k_rope, dil, win, gate = jnp.split(w_in.astype(BF16), bounds, axis=-1)
    win_q, win_kv = win[..., :WIN_HEADS * HEAD_DIM], win[..., WIN_HEADS * HEAD_DIM:]
    z = lambda w: jnp.zeros((depth, d, w), BF16)
    out = jnp.concatenate([gate, c_q, win_q, c_kv, win_kv, z(MLA_NOPE), k_rope,
                           z(N_COLS - OFF_KR - MLA_NOPE - MLA_ROPE)], axis=-1)
    assert out.shape[-1] == N_COLS
    return out, dil


def _layout_mla(w_uq, w_ukv, g_q, g_k):
    qd = MLA_NOPE + MLA_ROPE
    wq = w_uq.reshape(MLA_Q_RANK, MLA_HEADS, qd)
    wq = jnp.pad(wq, ((0, 0), (0, 0), (0, SLOT - qd))).reshape(MLA_Q_RANK, MLA_HEADS * SLOT)
    wkv = w_ukv.reshape(MLA_KV_RANK, MLA_HEADS, MLA_NOPE + MLA_V)
    wk = jnp.pad(wkv[:, :, :MLA_NOPE], ((0, 0), (0, 0), (0, SLOT - MLA_NOPE))).reshape(MLA_KV_RANK, MLA_HEADS * SLOT)
    wv = wkv[:, :, MLA_NOPE:].reshape(MLA_KV_RANK, MLA_HEADS * MLA_V)
    scale = LOG2E * float(qd) ** -0.5
    gq_slot = (jnp.pad(g_q, (0, SLOT - qd)) * scale).reshape(1, SLOT)
    gkn_slot = jnp.pad(g_k[:MLA_NOPE], (0, SLOT - MLA_NOPE)).reshape(1, SLOT)
    gkr_slot = jnp.pad(g_k[MLA_NOPE:], (MLA_NOPE, SLOT - qd)).reshape(1, SLOT)
    return wq.astype(BF16), jnp.concatenate([wk, wv], axis=1).astype(BF16), gq_slot, gkn_slot, gkr_slot


def kernel(x, c, pos, w_ada, b_ada, g_norm1, w_in, g_cq, w_uq, g_ckv, w_ukv, g_q_mla, g_k_mla, g_q_dil, g_k_dil,
           g_q_win, g_k_win, sink_win, w_br_mla, w_br_dil, w_br_win, w_out, g_norm2, w_gr, b_gr, w_er, b_er,
           w1, w3, w2):
    batch, seq, d = x.shape
    depth = w_ada.shape[0]
    t = batch * seq
    half = MLA_ROPE // 2
    inv_freq = ROPE_THETA ** (-jnp.arange(half, dtype=F32) / half)
    invf_slot = jnp.concatenate([jnp.zeros((MLA_NOPE,), F32), inv_freq, inv_freq,
                                 jnp.zeros((SLOT - MLA_NOPE - MLA_ROPE,), F32)]).reshape(1, SLOT)
    cos_t, sin_t = _rope_tables(pos.reshape(t, 1), invf_slot)
    mod = _modulation(c, w_ada, b_ada)
    w_in_k, w_dil_k = _layout_w_in(w_in)
    mseg_slot = _seg_matrix(SLOT, ((0, MLA_NOPE), (MLA_NOPE, MLA_ROPE)))
    mseg128 = _seg_matrix(128, tuple((k * HEAD_DIM, HEAD_DIM) for k in range(2)))
    head_scale = LOG2E * float(HEAD_DIM) ** -0.5
    win_bq = min(128, seq)

    x2d = x.reshape(t, d)
    for l in range(depth):
        mod_l = mod[l].reshape(batch, 6, d)
        g1 = g_norm1[l].reshape(1, d)
        cols = _in_projection(x2d, mod_l, g1, w_in_k[l], seq, BF16, N_COLS // 2, "in_projection")
        dil_cols = _in_projection(x2d, mod_l, g1, w_dil_k[l], seq, F32, N_DIL_COLS // 3, "in_projection_dil")

        wuq, wukv, gq_slot, gkn_slot, gkr_slot = _layout_mla(w_uq[l], w_ukv[l], g_q_mla[l], g_k_mla[l])
        qm, km, vm = _mla_prep(cols, cos_t, sin_t, g_cq[l].reshape(1, -1), g_ckv[l].reshape(1, -1),
                               wuq, wukv, gq_slot, gkn_slot, gkr_slot, mseg_slot)
        o_mla = _mla_attention(qm, km, vm, batch, seq)

        gq_dil = (jnp.tile(g_q_dil[l], 2) * head_scale).reshape(1, -1)
        gk_dil = jnp.tile(g_k_dil[l], 2).reshape(1, -1)
        o_dil, lse_dil = [], []
        for gi in range(DIL_GROUPS):
            o_g, lse_g = _dilated_group(dil_cols, pos, gq_dil, gk_dil, mseg128, gi, batch, seq)
            o_dil.append(o_g)
            lse_dil.append(lse_g)

        gq_win = (jnp.tile(g_q_win[l], 2) * head_scale).reshape(1, -1)
        gk_win = jnp.tile(g_k_win[l], WIN_KV_HEADS).reshape(1, -1)
        sink_row = jnp.repeat(sink_win[l].astype(F32) * LOG2E, win_bq).reshape(1, WIN_HEADS * win_bq)
        o_win = _window_mixer(cols, pos, gq_win, gk_win, sink_row, mseg128, batch, seq)

        x2d = _merge(cols, o_mla, o_dil, lse_dil, o_win, x2d, mod_l, w_br_mla[l].astype(BF16),
                     w_br_dil[l].astype(BF16), w_br_win[l].astype(BF16), w_out[l].astype(BF16), seq)
        x2d = _moe(x2d, mod_l, g_norm2[l].reshape(1, d), w_gr[l], b_gr[l], w_er[l], b_er[l],
                   w1[l], w3[l], w2[l], seq)
    return x2d.reshape(batch, seq, d)
```

```python
import functools
import math

import jax
import jax.numpy as jnp
from jax import lax
from jax.experimental import pallas as pl
from jax.experimental.pallas import tpu as pltpu

F32 = jnp.float32
BF16 = jnp.bfloat16
I32 = jnp.int32

D_MODEL = 1024
HEAD_DIM = 64
NEG_INF = -1e30
EPS = 1e-6
LOG2E = math.log2(math.e)
LN2 = math.log(2.0)
MLA_HEADS = 8
MLA_Q_RANK = 512
MLA_KV_RANK = 256
MLA_NOPE = 64
MLA_ROPE = 32
MLA_V = 64
ROPE_THETA = 10000.0
DIL_PATTERNS = ((128, 1), (512, 4), (2048, 16))
DIL_GROUPS = 3
DIL_HPG = 4
DIL_HEADS = DIL_GROUPS * DIL_HPG
DIL_RADIUS = 64
WIN_HEADS = 8
WIN_KV_HEADS = 2
WIN_RADIUS = 128
N_EXPERT_GROUPS = 4
EXPERTS_PER_GROUP = 8
N_EXPERTS = N_EXPERT_GROUPS * EXPERTS_PER_GROUP
D_EXPERT = 384

LANES = 128
SLOT = 128
VMEM_LIMIT = 48 * 1024 * 1024

OFF_GATE = 0
OFF_CQ = 3072
OFF_WQ = 3584
OFF_CKV = 4096
OFF_WK = 4352
OFF_WV = 4480
OFF_KR = 4608
N_COLS = 4864
N_DIL_COLS = 3 * DIL_HEADS * HEAD_DIM

MOE_BM = 512
MOE_CHUNK = 16
MLA_TQ = 512
MLA_HPS = 8


def _cparams(sem, vmem=None, flags=None):
    return pltpu.CompilerParams(dimension_semantics=sem, vmem_limit_bytes=vmem, flags=flags)


def _tile(n, pref):
    t = min(n, pref)
    assert n % t == 0, (n, pref)
    return t


def _dot(a, b):
    return jnp.dot(a, b, preferred_element_type=F32)


def _dot_nt(a, b):
    return lax.dot_general(a, b, (((1,), (1,)), ((), ())), preferred_element_type=F32)


def _dot_tn(a, b):
    return lax.dot_general(a, b, (((0,), (0,)), ((), ())), preferred_element_type=F32)


def _seg_mean_sq(x, mseg):
    x2 = x * x
    hi = x2.astype(BF16)
    lo = (x2 - hi.astype(F32)).astype(BF16)
    return _dot(hi, mseg) + _dot(lo, mseg)


def _seg_norm(x, mseg, gain):
    return x * lax.rsqrt(_seg_mean_sq(x, mseg) + EPS) * gain


def _rms(x, gain):
    ms = jnp.mean(x * x, axis=-1, keepdims=True)
    return x * lax.rsqrt(ms + EPS) * gain


def _mod_kernel(c_ref, w_ref, b_ref, o_ref):
    c = c_ref[...]
    cond = (c * jax.nn.sigmoid(c)).astype(BF16)
    o_ref[0] = _dot(cond, w_ref[0].astype(BF16)) + b_ref[0]


def _modulation(c, w_ada, b_ada):
    depth, d, n = w_ada.shape
    b = c.shape[0]
    tn = _tile(n, 1536)
    return pl.pallas_call(
        _mod_kernel,
        out_shape=jax.ShapeDtypeStruct((depth, b, n), F32),
        grid=(depth, n // tn),
        in_specs=[
            pl.BlockSpec((b, d), lambda l, j: (0, 0)),
            pl.BlockSpec((1, d, tn), lambda l, j: (l, 0, j)),
            pl.BlockSpec((1, 1, tn), lambda l, j: (l, 0, j)),
        ],
        out_specs=pl.BlockSpec((1, b, tn), lambda l, j: (l, 0, j)),
        compiler_params=_cparams(("parallel", "parallel")),
        name="adaln_mod",
    )(c, w_ada, b_ada.reshape(depth, 1, n))


def _rope_table_kernel(pos_ref, invf_ref, cos_ref, sin_ref):
    ang = pos_ref[...].astype(F32) * invf_ref[...]
    cos_ref[...] = jnp.cos(ang)
    sin_ref[...] = jnp.sin(ang)


def _rope_tables(pos_col, invf_slot):
    t = pos_col.shape[0]
    tm = _tile(t, 1024)
    return pl.pallas_call(
        _rope_table_kernel,
        out_shape=(jax.ShapeDtypeStruct((t, SLOT), F32), jax.ShapeDtypeStruct((t, SLOT), F32)),
        grid=(t // tm,),
        in_specs=[pl.BlockSpec((tm, 1), lambda i: (i, 0)), pl.BlockSpec((1, SLOT), lambda i: (0, 0))],
        out_specs=(pl.BlockSpec((tm, SLOT), lambda i: (i, 0)), pl.BlockSpec((tm, SLOT), lambda i: (i, 0))),
        compiler_params=_cparams(("parallel",)),
        name="rope_tables",
    )(pos_col, invf_slot)


def _inproj_kernel(x_ref, mod_ref, g_ref, w_ref, o_ref, h_scr):
    @pl.when(pl.program_id(1) == 0)
    def _():
        m = mod_ref[0]
        h = _rms(x_ref[...], g_ref[...]) * (1.0 + m[1:2]) + m[0:1]
        h_scr[...] = h.astype(BF16)

    o_ref[...] = _dot(h_scr[...], w_ref[...]).astype(o_ref.dtype)


def _in_projection(x2d, mod_l, g_norm, w_in_l, seq, out_dtype, tm, tn, name):
    t, d = x2d.shape
    nc = w_in_l.shape[1]
    tm = _tile(seq, tm)
    assert nc % tn == 0
    per_b = seq // tm
    return pl.pallas_call(
        _inproj_kernel,
        out_shape=jax.ShapeDtypeStruct((t, nc), out_dtype),
        grid=(t // tm, nc // tn),
        in_specs=[
            pl.BlockSpec((tm, d), lambda i, j: (i, 0)),
            pl.BlockSpec((1, 6, d), lambda i, j: (i // per_b, 0, 0)),
            pl.BlockSpec((1, d), lambda i, j: (0, 0)),
            pl.BlockSpec((d, tn), lambda i, j: (0, j)),
        ],
        out_specs=pl.BlockSpec((tm, tn), lambda i, j: (i, j)),
        scratch_shapes=[pltpu.VMEM((tm, d), BF16)],
        compiler_params=_cparams(("parallel", "arbitrary"), VMEM_LIMIT),
        name=name,
    )(x2d, mod_l, g_norm, w_in_l)


def _mla_prep_kernel(cq_ref, ckv_ref, kr_ref, cos_ref, sin_ref, gcq_ref, gckv_ref, wuq_ref, wukv_ref,
                     gq_ref, gkn_ref, gkr_ref, mseg_ref, q_out, k_out, v_out):
    cos = cos_ref[...]
    sin = sin_ref[...]
    lane = lax.broadcasted_iota(I32, (1, SLOT), 1)
    s_neg = jnp.where((lane >= 64) & (lane < 80), -sin, 0.0)
    s_pos = jnp.where((lane >= 80) & (lane < 96), sin, 0.0)
    mseg = mseg_ref[...]

    def rope(xn):
        return xn * cos + pltpu.roll(xn, SLOT - 16, 1) * s_neg + pltpu.roll(xn, 16, 1) * s_pos

    cqn = _rms(cq_ref[...].astype(F32), gcq_ref[...]).astype(BF16)
    q = _dot(cqn, wuq_ref[...])
    for h in range(MLA_HEADS):
        sl = slice(h * SLOT, (h + 1) * SLOT)
        q_out[:, sl] = rope(_seg_norm(q[:, sl], mseg, gq_ref[...])).astype(q_out.dtype)

    ckvn = _rms(ckv_ref[...].astype(F32), gckv_ref[...]).astype(BF16)
    kv = _dot(ckvn, wukv_ref[...])
    kr = rope(_seg_norm(kr_ref[...].astype(F32), mseg, gkr_ref[...]))
    for h in range(MLA_HEADS):
        sl = slice(h * SLOT, (h + 1) * SLOT)
        k_out[:, sl] = (_seg_norm(kv[:, sl], mseg, gkn_ref[...]) + kr).astype(k_out.dtype)
    v_out[...] = kv[:, MLA_HEADS * SLOT:].astype(v_out.dtype)


def _mla_prep(cols, cos_t, sin_t, gcq, gckv, wuq, wukv, gq_slot, gkn_slot, gkr_slot, mseg):
    t = cols.shape[0]
    tm = _tile(t, 512)
    hs = MLA_HEADS * SLOT
    full = lambda shape: pl.BlockSpec(shape, lambda i: (0,) * len(shape))
    return pl.pallas_call(
        _mla_prep_kernel,
        out_shape=(jax.ShapeDtypeStruct((t, hs), BF16), jax.ShapeDtypeStruct((t, hs), BF16),
                   jax.ShapeDtypeStruct((t, MLA_HEADS * MLA_V), BF16)),
        grid=(t // tm,),
        in_specs=[
            pl.BlockSpec((tm, MLA_Q_RANK), lambda i: (i, OFF_CQ // MLA_Q_RANK)),
            pl.BlockSpec((tm, MLA_KV_RANK), lambda i: (i, OFF_CKV // MLA_KV_RANK)),
            pl.BlockSpec((tm, SLOT), lambda i: (i, OFF_KR // SLOT)),
            pl.BlockSpec((tm, SLOT), lambda i: (i, 0)),
            pl.BlockSpec((tm, SLOT), lambda i: (i, 0)),
            full((1, MLA_Q_RANK)), full((1, MLA_KV_RANK)),
            full((MLA_Q_RANK, hs)), full((MLA_KV_RANK, hs + MLA_HEADS * MLA_V)),
            full((1, SLOT)), full((1, SLOT)), full((1, SLOT)), full((SLOT, SLOT)),
        ],
        out_specs=(pl.BlockSpec((tm, hs), lambda i: (i, 0)), pl.BlockSpec((tm, hs), lambda i: (i, 0)),
                   pl.BlockSpec((tm, MLA_HEADS * MLA_V), lambda i: (i, 0))),
        compiler_params=_cparams(("parallel",), VMEM_LIMIT),
        name="mla_prep",
    )(cols, cols, cols, cos_t, sin_t, gcq, gckv, wuq, wukv, gq_slot, gkn_slot, gkr_slot, mseg)


def _mla_attn_kernel(q_ref, k_ref, v_ref, o_ref):
    lane = lax.broadcasted_iota(I32, (1, 2 * MLA_V), 1)
    for pair in range(MLA_HPS // 2):
        outs = []
        vp = v_ref[0, :, pair * 2 * MLA_V:(pair + 1) * 2 * MLA_V]
        for a in range(2):
            sl = slice((2 * pair + a) * SLOT, (2 * pair + a + 1) * SLOT)
            s = _dot_nt(q_ref[0, :, sl], k_ref[0, :, sl])
            m = jnp.max(s, axis=-1, keepdims=True)
            p = jnp.exp2(s - m)
            l = jnp.sum(p, axis=-1, keepdims=True)
            outs.append(_dot(p.astype(BF16), vp) / l)
        o_ref[0, :, pair * 2 * MLA_V:(pair + 1) * 2 * MLA_V] = jnp.where(lane < MLA_V, outs[0], outs[1]).astype(o_ref.dtype)


def _mla_attention(qm, km, vm, batch, seq):
    hs = MLA_HEADS * SLOT
    q3 = qm.reshape(batch, seq, hs)
    k3 = km.reshape(batch, seq, hs)
    v3 = vm.reshape(batch, seq, MLA_HEADS * MLA_V)
    tq = _tile(seq, MLA_TQ)
    out = pl.pallas_call(
        _mla_attn_kernel,
        out_shape=jax.ShapeDtypeStruct((batch, seq, MLA_HEADS * MLA_V), BF16),
        grid=(batch, MLA_HEADS // MLA_HPS, seq // tq),
        in_specs=[
            pl.BlockSpec((1, tq, MLA_HPS * SLOT), lambda b, p, i: (b, i, p)),
            pl.BlockSpec((1, seq, MLA_HPS * SLOT), lambda b, p, i: (b, 0, p)),
            pl.BlockSpec((1, seq, MLA_HPS * MLA_V), lambda b, p, i: (b, 0, p)),
        ],
        out_specs=pl.BlockSpec((1, tq, MLA_HPS * MLA_V), lambda b, p, i: (b, i, p)),
        compiler_params=_cparams(("parallel", "parallel", "arbitrary"), VMEM_LIMIT),
        name="mla_attention",
    )(q3, k3, v3)
    return out.reshape(batch * seq, MLA_HEADS * MLA_V)


WIN_BLOCKS_PER_STEP = 4
BAND_UNROLL = 16


def _window_start(i, bq, radius, n, kw):
    ws = jnp.clip(i * bq - radius, 0, n - kw)
    return pl.multiple_of(ws, 16)


MASKED_DIST = 1e30


def _masked_dist_t(i, bq, ws, kw, radius, pos_keys, pos_queries):
    keys = lax.broadcasted_iota(I32, (kw, bq), 0)
    queries = lax.broadcasted_iota(I32, (kw, bq), 1)
    rel = queries - keys + (i * bq - ws)
    d = pos_keys - pos_queries
    return jnp.where(jnp.maximum(rel, -rel) <= radius, jnp.maximum(d, -d).astype(F32), MASKED_DIST)


def _sub_positions(pos, batch, n, dil, bq):
    pos_sub = jnp.transpose(pos.reshape(batch, n, dil), (0, 2, 1))
    return pos_sub.reshape(batch, dil * n, 1), pos_sub.reshape(batch, dil * (n // bq), 1, bq)


def _dil_kernel(q0_ref, q1_ref, k0_ref, k1_ref, v0_ref, v1_ref, pcol_ref, prow_ref, gq_ref, gk_ref, mseg_ref,
                o_ref, lse_ref, kn_scr, vn_scr, *, n, bq, kw, dil, slopes):
    q_refs, k_refs, v_refs = (q0_ref, q1_ref), (k0_ref, k1_ref), (v0_ref, v1_ref)
    nblk = n // bq
    mseg = mseg_ref[...]
    gq = gq_ref[...]
    low_lane = lax.broadcasted_iota(I32, (1, 2 * HEAD_DIM), 1) < HEAD_DIM

    def rows(start, size):
        return pl.ds(start, size) if dil == 1 else pl.ds(start, size, stride=dil)

    def fill(r, carry):
        base = pl.multiple_of(r * n, 16)
        for hf in range(2):
            kn_scr[hf, pl.ds(base, n), :] = _seg_norm(k_refs[hf][0, rows(r, n), :], mseg, gk_ref[...]).astype(BF16)
            vn_scr[hf, pl.ds(base, n), :] = v_refs[hf][0, rows(r, n), :].astype(BF16)
        return carry

    lax.fori_loop(0, dil, fill, 0)

    def query_block(b, carry):
        r = b >> (nblk.bit_length() - 1)
        i = b & (nblk - 1)
        ws = _window_start(i, bq, DIL_RADIUS, n, kw)
        start = pl.multiple_of(r * n + ws, 16)
        q_rows = rows(r + i * (bq * dil), bq)
        dist = _masked_dist_t(i, bq, ws, kw, DIL_RADIUS, pcol_ref[0, pl.ds(start, kw), :], prow_ref[0, r * nblk + i])
        s_halves = []
        for hf in range(2):
            qn = _seg_norm(q_refs[hf][0, q_rows, :], mseg, gq)
            qs = jnp.concatenate([jnp.where(low_lane, qn, 0.0), jnp.where(low_lane, 0.0, qn)], axis=0).astype(BF16)
            bias = jnp.concatenate([slopes[2 * hf] * dist, slopes[2 * hf + 1] * dist], axis=1)
            s_halves.append(_dot_nt(kn_scr[hf, pl.ds(start, kw), :], qs) - bias)
        s = jnp.concatenate(s_halves, axis=1)
        m = jnp.max(s, axis=0, keepdims=True)
        p = jnp.exp2(s - m)
        l = jnp.sum(p, axis=0, keepdims=True)
        lse = (m + jnp.log2(l)) * LN2
        pb = p.astype(BF16)
        for hf in range(2):
            c0, c1, c2 = 2 * hf * bq, (2 * hf + 1) * bq, (2 * hf + 2) * bq
            ot = _dot_tn(vn_scr[hf, pl.ds(start, kw), :], pb[:, c0:c2]) / l[:, c0:c2]
            o_ref[0, hf, q_rows, :] = jnp.concatenate([ot[:HEAD_DIM, :bq], ot[HEAD_DIM:, bq:]], axis=0).T
            lse_ref[0, hf, q_rows, :] = jnp.concatenate([jnp.broadcast_to(lse[:, c0:c1], (HEAD_DIM, bq)),
                                                         jnp.broadcast_to(lse[:, c1:c2], (HEAD_DIM, bq))], axis=0).T
        return carry

    lax.fori_loop(0, dil * nblk, query_block, 0, unroll=min(dil * nblk, BAND_UNROLL))


def _dilated_group(dil_cols, pos, gq_row, gk_row, mseg, gi, batch, seq):
    _, dil = DIL_PATTERNS[gi]
    n = seq // dil
    bq = min(128, n)
    kw = min(bq + 2 * DIL_RADIUS, n)
    nblk = n // bq
    hw = 2 * HEAD_DIM
    cols3 = dil_cols.reshape(batch, seq, N_DIL_COLS)
    pcol, prow = _sub_positions(pos, batch, n, dil, bq)
    slopes = tuple(float(LOG2E * 2.0 ** (-8.0 * (gi * DIL_HPG + hh + 1) / DIL_HEADS)) for hh in range(DIL_HPG))
    kern = functools.partial(_dil_kernel, n=n, bq=bq, kw=kw, dil=dil, slopes=slopes)
    col = lambda which, hf: pl.BlockSpec((1, seq, hw), lambda b: (b, 0, (which * DIL_GROUPS + gi) * 2 + hf))
    small = lambda shape: pl.BlockSpec(shape, lambda b: (0,) * len(shape))
    halves = jax.ShapeDtypeStruct((batch, 2, seq, hw), F32)
    return pl.pallas_call(
        kern,
        out_shape=(halves, halves),
        grid=(batch,),
        in_specs=[
            col(0, 0), col(0, 1), col(1, 0), col(1, 1), col(2, 0), col(2, 1),
            pl.BlockSpec((1, seq, 1), lambda b: (b, 0, 0)),
            pl.BlockSpec((1, dil * nblk, 1, bq), lambda b: (b, 0, 0, 0)),
            small((1, hw)), small((1, hw)), small((hw, hw)),
        ],
        out_specs=(pl.BlockSpec((1, 2, seq, hw), lambda b: (b, 0, 0, 0)),
                   pl.BlockSpec((1, 2, seq, hw), lambda b: (b, 0, 0, 0))),
        scratch_shapes=[pltpu.VMEM((2, seq, hw), BF16), pltpu.VMEM((2, seq, hw), BF16)],
        compiler_params=_cparams(("parallel",), VMEM_LIMIT),
        name=f"dilated_group{gi}",
    )(cols3, cols3, cols3, cols3, cols3, cols3, pcol, prow, gq_row, gk_row, mseg)


def _win_kernel(q_ref, k_ref, v_ref, pcol_ref, prow_ref, gq_ref, gk_ref, sink_ref, mseg_ref, o_ref, kn_scr,
                *, n, bq, kw, slopes):
    step = pl.program_id(1)
    mseg = mseg_ref[...]

    @pl.when(step == 0)
    def _():
        kn_scr[...] = _seg_norm(k_ref[0].astype(F32), mseg, gk_ref[...]).astype(BF16)

    low_lane = lax.broadcasted_iota(I32, (1, 2 * HEAD_DIM), 1) < HEAD_DIM
    rep = WIN_HEADS // WIN_KV_HEADS
    sink = sink_ref[...]
    per_step = q_ref.shape[1] // bq
    for j in range(per_step):
        i = step * per_step + j
        rows = slice(j * bq, (j + 1) * bq)
        ws = _window_start(i, bq, WIN_RADIUS, n, kw)
        kwin = kn_scr[pl.ds(ws, kw), :]
        vwin = v_ref[0, pl.ds(ws, kw), :]
        dist = _masked_dist_t(i, bq, ws, kw, WIN_RADIUS, pcol_ref[0, pl.ds(ws, kw), :], prow_ref[0, j])

        qn_pairs = [_seg_norm(q_ref[0, rows, pair * 128:(pair + 1) * 128].astype(F32), mseg, gq_ref[...])
                    for pair in range(WIN_HEADS // 2)]
        q_parts = []
        for h in range(WIN_HEADS):
            pair, upper = divmod(h, 2)
            qm = jnp.where(low_lane, 0.0, qn_pairs[pair]) if upper else jnp.where(low_lane, qn_pairs[pair], 0.0)
            if upper != h // rep:
                qm = pltpu.roll(qm, HEAD_DIM, 1)
            q_parts.append(qm.astype(BF16))
        qs = jnp.concatenate(q_parts, axis=0)
        bias = jnp.concatenate([slopes[h] * dist for h in range(WIN_HEADS)], axis=1)
        s = _dot_nt(kwin, qs) - bias
        m = jnp.maximum(jnp.max(s, axis=0, keepdims=True), sink)
        e = jnp.exp2(s - m)
        den = jnp.sum(e, axis=0, keepdims=True) + jnp.exp2(sink - m)
        ot = _dot_tn(vwin, e.astype(BF16)) / den
        for pair in range(WIN_HEADS // 2):
            slabs = []
            for h in (2 * pair, 2 * pair + 1):
                g = h // rep
                slabs.append(ot[g * HEAD_DIM:(g + 1) * HEAD_DIM, h * bq:(h + 1) * bq])
            o_ref[0, rows, pair * 128:(pair + 1) * 128] = jnp.concatenate(slabs, axis=0).T.astype(o_ref.dtype)


def _window_mixer(cols, pos, gq_row, gk_row, sink_row, mseg128, batch, seq):
    n = seq
    bq = min(128, n)
    kw = min(bq + 2 * WIN_RADIUS, n)
    nblk = n // bq
    qw = WIN_HEADS * HEAD_DIM
    kvw = WIN_KV_HEADS * HEAD_DIM
    cols_v = cols.reshape(batch, seq, N_COLS)
    pcol = pos.reshape(batch, seq, 1)
    prow = pos.reshape(batch, nblk, 1, bq)
    slopes = tuple(float(LOG2E * 2.0 ** (-8.0 * (h + 1) / WIN_HEADS)) for h in range(WIN_HEADS))
    kern = functools.partial(_win_kernel, n=n, bq=bq, kw=kw, slopes=slopes)
    small = lambda shape: pl.BlockSpec(shape, lambda b, i: (0,) * len(shape))
    per_step = min(WIN_BLOCKS_PER_STEP, nblk)
    o = pl.pallas_call(
        kern,
        out_shape=jax.ShapeDtypeStruct((batch, seq, qw), BF16),
        grid=(batch, nblk // per_step),
        in_specs=[
            pl.BlockSpec((1, per_step * bq, qw), lambda b, i: (b, i, OFF_WQ // qw)),
            pl.BlockSpec((1, n, kvw), lambda b, i: (b, 0, OFF_WK // kvw)),
            pl.BlockSpec((1, n, kvw), lambda b, i: (b, 0, OFF_WV // kvw)),
            pl.BlockSpec((1, n, 1), lambda b, i: (b, 0, 0)),
            pl.BlockSpec((1, per_step, 1, bq), lambda b, i: (b, i, 0, 0)),
            small((1, kvw)), small((1, kvw)), small((1, WIN_HEADS * bq)), small((kvw, kvw)),
        ],
        out_specs=pl.BlockSpec((1, per_step * bq, qw), lambda b, i: (b, i, 0)),
        scratch_shapes=[pltpu.VMEM((n, kvw), BF16)],
        compiler_params=_cparams(("parallel", "arbitrary"), VMEM_LIMIT),
        name="window_mixer",
    )(cols_v, cols_v, cols_v, pcol, prow, gq_row, gk_row, sink_row, mseg128)
    return o.reshape(batch * seq, qw)


def _merge_kernel(gm_ref, gd_ref, gw_ref, om_ref, od0_ref, od1_ref, od2_ref, l0_ref, l1_ref, l2_ref, ow_ref,
                  x_ref, mod_ref, wm_ref, wd_ref, ww_ref, wo_ref, o_ref):
    def heads(ref):
        return jnp.concatenate([ref[0, 0], ref[0, 1]], axis=1)

    l0, l1, l2 = heads(l0_ref), heads(l1_ref), heads(l2_ref)
    m = jnp.maximum(jnp.maximum(l0, l1), l2)
    e0, e1, e2 = jnp.exp(l0 - m), jnp.exp(l1 - m), jnp.exp(l2 - m)
    od = (e0 * heads(od0_ref) + e1 * heads(od1_ref) + e2 * heads(od2_ref)) / (e0 + e1 + e2)
    y = jax.nn.sigmoid(gm_ref[...].astype(F32)) * _dot(om_ref[...], wm_ref[...])
    y += jax.nn.sigmoid(gd_ref[...].astype(F32)) * _dot(od.astype(BF16), wd_ref[...])
    y += jax.nn.sigmoid(gw_ref[...].astype(F32)) * _dot(ow_ref[...], ww_ref[...])
    z = _dot(y.astype(BF16), wo_ref[...])
    o_ref[...] = x_ref[...] + mod_ref[0][2:3] * z


def _merge(cols, o_mla, o_dil, lse_dil, o_win, x2d, mod_l, wm, wd, ww, wo, seq):
    t, d = x2d.shape
    tm = _tile(seq, 512)
    per_b = seq // tm
    row = lambda w, j=0: pl.BlockSpec((tm, w), lambda i: (i, j))
    full = lambda a: pl.BlockSpec(a.shape, lambda i: (0, 0))
    dil = pl.BlockSpec((1, 2, tm, 2 * HEAD_DIM), lambda i: (i // per_b, 0, i % per_b, 0))
    return pl.pallas_call(
        _merge_kernel,
        out_shape=jax.ShapeDtypeStruct((t, d), F32),
        grid=(t // tm,),
        in_specs=[row(d, 0), row(d, 1), row(d, 2), row(o_mla.shape[1]),
                  dil, dil, dil, dil, dil, dil, row(o_win.shape[1]),
                  row(d), pl.BlockSpec((1, 6, d), lambda i: (i // per_b, 0, 0)),
                  full(wm), full(wd), full(ww), full(wo)],
        out_specs=row(d),
        compiler_params=_cparams(("parallel",), VMEM_LIMIT),
        name="merge_out_proj",
    )(cols, cols, cols, o_mla, o_dil[0], o_dil[1], o_dil[2], lse_dil[0], lse_dil[1], lse_dil[2], o_win,
      x2d, mod_l, wm, wd, ww, wo)


def _router_kernel(x_ref, mod_ref, g_ref, whi_ref, wlo_ref, br_ref, upper_ref, hs_ref, pos_ref, gate_ref, tab_ref,
                   *, lr):
    m = mod_ref[0]
    h = _rms(x_ref[...], g_ref[...]) * (1.0 + m[4:5]) + m[3:4]
    tm = h.shape[0]
    h_hi = h.astype(BF16)
    h_lo = (h - h_hi.astype(F32)).astype(BF16)
    logits = _dot(h_hi, whi_ref[...]) + (_dot(h_hi, wlo_ref[...]) + _dot(h_lo, whi_ref[...])) + br_ref[...]
    lane = lax.broadcasted_iota(I32, (tm, LANES), 1)
    lane_f = lane.astype(F32)

    def first_argmax(vals, mx):
        return jnp.min(jnp.where(vals == mx, lane_f, float(LANES)), axis=-1, keepdims=True).astype(I32)

    lg = jnp.where(lane < N_EXPERT_GROUPS, logits, NEG_INF)
    mg = jnp.max(lg, axis=-1, keepdims=True)
    g_w = 1.0 / jnp.sum(jnp.exp(lg - mg), axis=-1, keepdims=True)
    g_idx = first_argmax(lg, mg)
    eid = lane - N_EXPERT_GROUPS
    in_grp = (eid >= 0) & (eid < N_EXPERTS) & ((eid >> 3) == g_idx)
    le = jnp.where(in_grp, logits, NEG_INF)
    m1 = jnp.max(le, axis=-1, keepdims=True)
    i1 = first_argmax(le, m1)
    le2 = jnp.where(lane == i1, NEG_INF, le)
    m2 = jnp.max(le2, axis=-1, keepdims=True)
    i2 = first_argmax(le2, m2)
    r = jnp.exp(m2 - m1)
    gate1 = g_w / (1.0 + r)
    gate2 = g_w * r / (1.0 + r)
    e1 = i1 - N_EXPERT_GROUPS
    e2 = i2 - N_EXPERT_GROUPS
    hit1 = lane == e1
    hit2 = lane == e2
    onehot = jnp.where(hit1 | hit2, 1.0, 0.0)
    rows = lax.broadcasted_iota(I32, (tm, tm), 0)
    cols = lax.broadcasted_iota(I32, (tm, tm), 1)
    before = jnp.where(rows > cols, 1.0, 0.0).astype(BF16)
    rank = _dot(before, onehot.astype(BF16))
    cnt8 = (jnp.sum(onehot, axis=0, keepdims=True).astype(I32) + (MOE_CHUNK - 1)) & jnp.int32(-MOE_CHUNK)
    off8 = _dot(jnp.broadcast_to(cnt8.astype(F32), (8, LANES)).astype(BF16), upper_ref[...])[0:1]
    pos1 = jnp.sum(jnp.where(hit1, rank + off8, 0.0), axis=-1, keepdims=True).astype(I32)
    pos2 = jnp.sum(jnp.where(hit2, rank + off8, 0.0), axis=-1, keepdims=True).astype(I32)
    pos_lanes = jnp.where(lane == 0, pos1, jnp.where(lane == 1, pos2, 0))
    pos_rows = pos_lanes.astype(F32).T.astype(I32)
    local = lax.broadcasted_iota(I32, (lr, tm), 0)
    place = jnp.where((local == pos_rows[0:1]) | (local == pos_rows[1:2]), 1.0, 0.0).astype(BF16)
    hs_ref[0] = _dot(place, h_hi).astype(BF16)
    pos_ref[...] = pos_lanes
    gate_ref[...] = jnp.where(lane == 0, gate1, jnp.where(lane == 1, gate2, 0.0))
    sub = lax.broadcasted_iota(I32, (8, LANES), 0)
    tab_ref[0] = jnp.where(sub == 0, cnt8, jnp.where(sub == 1, off8.astype(I32), 0))


def _router(x2d, mod_l, g_norm, w_router, b_router, seq, tm, lr):
    t, d = x2d.shape
    per_b = seq // tm
    nt = t // tm
    row = lambda w: pl.BlockSpec((tm, w), lambda i: (i, 0))
    idx = jnp.arange(LANES)
    upper = jnp.where(idx[:, None] < idx[None, :], 1.0, 0.0).astype(BF16)
    w_hi = w_router.astype(BF16)
    w_lo = (w_router - w_hi.astype(F32)).astype(BF16)
    return pl.pallas_call(
        functools.partial(_router_kernel, lr=lr),
        out_shape=(jax.ShapeDtypeStruct((nt, lr, d), BF16), jax.ShapeDtypeStruct((t, LANES), I32),
                   jax.ShapeDtypeStruct((t, LANES), F32), jax.ShapeDtypeStruct((nt, 8, LANES), I32)),
        grid=(nt,),
        in_specs=[row(d), pl.BlockSpec((1, 6, d), lambda i: (i // per_b, 0, 0)),
                  pl.BlockSpec((1, d), lambda i: (0, 0)), pl.BlockSpec((d, LANES), lambda i: (0, 0)),
                  pl.BlockSpec((d, LANES), lambda i: (0, 0)),
                  pl.BlockSpec((1, LANES), lambda i: (0, 0)), pl.BlockSpec((LANES, LANES), lambda i: (0, 0))],
        out_specs=(pl.BlockSpec((1, lr, d), lambda i: (i, 0, 0)), row(LANES), row(LANES),
                   pl.BlockSpec((1, 8, LANES), lambda i: (i, 0, 0))),
        compiler_params=_cparams(("parallel",), VMEM_LIMIT),
        name="moe_router",
    )(x2d, mod_l, g_norm, w_hi, w_lo, b_router, upper)


def _for_each_chunk(off_ref, n_ref, base_ref, tile, fn):
    def per_expert(e, carry):
        k = tile * N_EXPERTS + e
        off, base = off_ref[k], base_ref[k]

        def per_chunk(c, carry2):
            fn(pl.multiple_of(off + c * MOE_CHUNK, MOE_CHUNK), pl.multiple_of(base + c * MOE_CHUNK, MOE_CHUNK))
            return carry2

        return lax.fori_loop(0, n_ref[k], per_chunk, carry)

    lax.fori_loop(0, N_EXPERTS, per_expert, 0)


def _scatter_kernel(off_ref, n_ref, base_ref, hs_ref, rows_in_ref, rows_ref, sem):
    del rows_in_ref
    tile = pl.program_id(0)

    def chunk_copy(local, glob):
        return pltpu.make_async_copy(hs_ref.at[0, pl.ds(local, MOE_CHUNK)], rows_ref.at[pl.ds(glob, MOE_CHUNK)], sem)

    _for_each_chunk(off_ref, n_ref, base_ref, tile, lambda a, b: chunk_copy(a, b).start())
    _for_each_chunk(off_ref, n_ref, base_ref, tile, lambda a, b: chunk_copy(a, b).wait())


def _scatter_rows(off8, n8, base, hs, n_rows):
    nt, lr, w = hs.shape
    zeros = jnp.zeros((n_rows, w), hs.dtype)
    grid_spec = pltpu.PrefetchScalarGridSpec(
        num_scalar_prefetch=3,
        grid=(nt,),
        in_specs=[pl.BlockSpec((1, lr, w), lambda i, *_: (i, 0, 0)), pl.BlockSpec(memory_space=pl.ANY)],
        out_specs=pl.BlockSpec(memory_space=pl.ANY),
        scratch_shapes=[pltpu.SemaphoreType.DMA],
    )
    return pl.pallas_call(
        _scatter_kernel,
        out_shape=jax.ShapeDtypeStruct((n_rows, w), hs.dtype),
        grid_spec=grid_spec,
        input_output_aliases={4: 0},
        compiler_params=_cparams(("arbitrary",)),
        name="moe_scatter",
    )(off8, n8, base, hs, zeros)


def _ffn_kernel(be_ref, nu_ref, rows_ref, w1_ref, w3_ref, w2_ref, y_ref, w13_scr, w2_scr):
    j = pl.program_id(0)
    used = j < nu_ref[0]
    new_expert = (j == 0) | (be_ref[j] != be_ref[jnp.maximum(j - 1, 0)])

    @pl.when(used & new_expert)
    def _():
        w13_scr[:, :D_EXPERT] = w1_ref[0].astype(BF16)
        w13_scr[:, D_EXPERT:] = w3_ref[0].astype(BF16)
        w2_scr[...] = w2_ref[0].astype(BF16)

    @pl.when(used)
    def _():
        h = _dot(rows_ref[...], w13_scr[...])
        a = h[:, :D_EXPERT]
        act = a * jax.nn.sigmoid(a) * h[:, D_EXPERT:]
        y_ref[...] = _dot(act.astype(BF16), w2_scr[...]).astype(y_ref.dtype)

    @pl.when(jnp.logical_not(used))
    def _():
        y_ref[...] = jnp.zeros_like(y_ref)


def _grouped_ffn(block_expert, n_used, rows, w1, w3, w2):
    n_rows, w = rows.shape
    d = w1.shape[1]
    nb = n_rows // MOE_BM
    grid_spec = pltpu.PrefetchScalarGridSpec(
        num_scalar_prefetch=2,
        grid=(nb,),
        in_specs=[
            pl.BlockSpec((MOE_BM, w), lambda j, be, nu: (j, 0)),
            pl.BlockSpec((1, d, D_EXPERT), lambda j, be, nu: (be[j], 0, 0)),
            pl.BlockSpec((1, d, D_EXPERT), lambda j, be, nu: (be[j], 0, 0)),
            pl.BlockSpec((1, D_EXPERT, d), lambda j, be, nu: (be[j], 0, 0)),
        ],
        out_specs=pl.BlockSpec((MOE_BM, w), lambda j, be, nu: (j, 0)),
        scratch_shapes=[pltpu.VMEM((d, 2 * D_EXPERT), BF16), pltpu.VMEM((D_EXPERT, d), BF16)],
    )
    return pl.pallas_call(
        _ffn_kernel,
        out_shape=jax.ShapeDtypeStruct((n_rows, w), BF16),
        grid_spec=grid_spec,
        compiler_params=_cparams(("arbitrary",), VMEM_LIMIT),
        name="moe_grouped_ffn",
    )(block_expert, n_used, rows, w1, w3, w2)


def _combine_kernel(off_ref, n_ref, base_ref, y_ref, pos_ref, gate_ref, x_ref, mod_ref, o_ref, ybuf, sem):
    tile = pl.program_id(0)
    slot = tile % 2

    def chunk_copy(s, local, glob):
        return pltpu.make_async_copy(y_ref.at[pl.ds(glob, MOE_CHUNK)], ybuf.at[s, pl.ds(local, MOE_CHUNK)], sem.at[s])

    def request(t, s):
        ybuf[s] = jnp.zeros(ybuf.shape[1:], ybuf.dtype)
        _for_each_chunk(off_ref, n_ref, base_ref, t, lambda a, b: chunk_copy(s, a, b).start())

    @pl.when(tile == 0)
    def _():
        request(tile, slot)

    @pl.when(tile + 1 < pl.num_programs(0))
    def _():
        request(tile + 1, 1 - slot)

    _for_each_chunk(off_ref, n_ref, base_ref, tile, lambda a, b: chunk_copy(slot, a, b).wait())
    tm = x_ref.shape[0]
    pos, g = pos_ref[...], gate_ref[...]
    local = lax.broadcasted_iota(I32, (tm, ybuf.shape[1]), 1)
    pick = (jnp.where(local == pos[:, 0:1], g[:, 0:1], 0.0)
            + jnp.where(local == pos[:, 1:2], g[:, 1:2], 0.0)).astype(BF16)
    moe = _dot(pick, ybuf[slot])
    o_ref[...] = x_ref[...] + mod_ref[0][5:6] * moe


def _combine(off8, n8, base, y, pos, gates, x2d, mod_l, seq, tm, lr):
    t, d = x2d.shape
    per_b = seq // tm
    grid_spec = pltpu.PrefetchScalarGridSpec(
        num_scalar_prefetch=3,
        grid=(t // tm,),
        in_specs=[pl.BlockSpec(memory_space=pl.ANY),
                  pl.BlockSpec((tm, LANES), lambda i, *_: (i, 0)), pl.BlockSpec((tm, LANES), lambda i, *_: (i, 0)),
                  pl.BlockSpec((tm, d), lambda i, *_: (i, 0)),
                  pl.BlockSpec((1, 6, d), lambda i, *_: (i // per_b, 0, 0))],
        out_specs=pl.BlockSpec((tm, d), lambda i, *_: (i, 0)),
        scratch_shapes=[pltpu.VMEM((2, lr, y.shape[1]), y.dtype), pltpu.SemaphoreType.DMA((2,))],
    )
    return pl.pallas_call(
        _combine_kernel,
        out_shape=jax.ShapeDtypeStruct((t, d), F32),
        grid_spec=grid_spec,
        compiler_params=_cparams(("arbitrary",), VMEM_LIMIT),
        name="moe_combine",
    )(off8, n8, base, y, pos, gates, x2d, mod_l)


def _moe(x2d, mod_l, g_norm2, w_gr, b_gr, w_er, b_er, w1, w3, w2, seq):
    t, d = x2d.shape
    pad = LANES - N_EXPERT_GROUPS - N_EXPERTS
    w_router = jnp.concatenate([w_gr, w_er, jnp.zeros((d, pad), F32)], axis=1)
    b_router = jnp.concatenate([b_gr, b_er, jnp.zeros((pad,), F32)]).reshape(1, LANES)
    tm = _tile(seq, 512)
    nt = t // tm
    lr = 2 * tm + N_EXPERTS * MOE_CHUNK
    hs, pos, gates, tab = _router(x2d, mod_l, g_norm2, w_router, b_router, seq, tm, lr)

    cnt8 = tab[:, 0, :N_EXPERTS]
    off8 = tab[:, 1, :N_EXPERTS]
    total = jnp.sum(cnt8, axis=0)
    padded = (total + MOE_BM - 1) // MOE_BM * MOE_BM
    ends = jnp.cumsum(padded)
    base = (ends - padded)[None, :] + jnp.cumsum(cnt8, axis=0) - cnt8
    nb = (2 * t + nt * N_EXPERTS * MOE_CHUNK) // MOE_BM + N_EXPERTS
    block_start = jnp.arange(nb, dtype=I32) * MOE_BM
    block_expert = jnp.minimum(jnp.sum(block_start[:, None] >= ends[None, :], axis=1), N_EXPERTS - 1).astype(I32)
    n_used = (ends[-1] // MOE_BM).astype(I32).reshape(1)
    flat = lambda a: a.reshape(-1).astype(I32)
    off8, n8, base = flat(off8), flat(cnt8 // MOE_CHUNK), flat(base)

    rows = _scatter_rows(off8, n8, base, hs, nb * MOE_BM)
    y = _grouped_ffn(block_expert, n_used, rows, w1, w3, w2)
    return _combine(off8, n8, base, y, pos, gates, x2d, mod_l, seq, tm, lr)


def _seg_matrix(width, segments):
    idx = jnp.arange(width)
    m = jnp.zeros((width, width), F32)
    for start, length in segments:
        inside = (idx >= start) & (idx < start + length)
        m = m + jnp.where(inside[:, None] & inside[None, :], 1.0 / length, 0.0)
    return m.astype(BF16)


def _layout_w_in(w_in):
    depth, d, _ = w_in.shape
    sizes = [MLA_Q_RANK, MLA_KV_RANK, MLA_ROPE, N_DIL_COLS,
             (WIN_HEADS + 2 * WIN_KV_HEADS) * HEAD_DIM, 3 * D_MODEL]
    bounds = [sum(sizes[:k + 1]) for k in range(len(sizes) - 1)]
    c_q, c_kv, k_rope, dil, win, gate = jnp.split(w_in.astype(BF16), bounds, axis=-1)
    win_q, win_kv = win[..., :WIN_HEADS * HEAD_DIM], win[..., WIN_HEADS * HEAD_DIM:]
    z = lambda w: jnp.zeros((depth, d, w), BF16)
    out = jnp.concatenate([gate, c_q, win_q, c_kv, win_kv, z(MLA_NOPE), k_rope,
                           z(N_COLS - OFF_KR - MLA_NOPE - MLA_ROPE)], axis=-1)
    assert out.shape[-1] == N_COLS
    return out, dil


def _layout_mla(w_uq, w_ukv, g_q, g_k):
    qd = MLA_NOPE + MLA_ROPE
    wq = w_uq.reshape(MLA_Q_RANK, MLA_HEADS, qd)
    wq = jnp.pad(wq, ((0, 0), (0, 0), (0, SLOT - qd))).reshape(MLA_Q_RANK, MLA_HEADS * SLOT)
    wkv = w_ukv.reshape(MLA_KV_RANK, MLA_HEADS, MLA_NOPE + MLA_V)
    wk = jnp.pad(wkv[:, :, :MLA_NOPE], ((0, 0), (0, 0), (0, SLOT - MLA_NOPE))).reshape(MLA_KV_RANK, MLA_HEADS * SLOT)
    wv = wkv[:, :, MLA_NOPE:].reshape(MLA_KV_RANK, MLA_HEADS * MLA_V)
    scale = LOG2E * float(qd) ** -0.5
    gq_slot = (jnp.pad(g_q, (0, SLOT - qd)) * scale).reshape(1, SLOT)
    gkn_slot = jnp.pad(g_k[:MLA_NOPE], (0, SLOT - MLA_NOPE)).reshape(1, SLOT)
    gkr_slot = jnp.pad(g_k[MLA_NOPE:], (MLA_NOPE, SLOT - qd)).reshape(1, SLOT)
    return wq.astype(BF16), jnp.concatenate([wk, wv], axis=1).astype(BF16), gq_slot, gkn_slot, gkr_slot


def kernel(x, c, pos, w_ada, b_ada, g_norm1, w_in, g_cq, w_uq, g_ckv, w_ukv, g_q_mla, g_k_mla, g_q_dil, g_k_dil,
           g_q_win, g_k_win, sink_win, w_br_mla, w_br_dil, w_br_win, w_out, g_norm2, w_gr, b_gr, w_er, b_er,
           w1, w3, w2):
    batch, seq, d = x.shape
    depth = w_ada.shape[0]
    t = batch * seq
    half = MLA_ROPE // 2
    inv_freq = ROPE_THETA ** (-jnp.arange(half, dtype=F32) / half)
    invf_slot = jnp.concatenate([jnp.zeros((MLA_NOPE,), F32), inv_freq, inv_freq,
                                 jnp.zeros((SLOT - MLA_NOPE - MLA_ROPE,), F32)]).reshape(1, SLOT)
    cos_t, sin_t = _rope_tables(pos.reshape(t, 1), invf_slot)
    mod = _modulation(c, w_ada, b_ada)
    w_in_k, w_dil_k = _layout_w_in(w_in)
    mseg_slot = _seg_matrix(SLOT, ((0, MLA_NOPE), (MLA_NOPE, MLA_ROPE)))
    mseg128 = _seg_matrix(128, tuple((k * HEAD_DIM, HEAD_DIM) for k in range(2)))
    head_scale = LOG2E * float(HEAD_DIM) ** -0.5
    win_bq = min(128, seq)

    x2d = x.reshape(t, d)
    for l in range(depth):
        mod_l = mod[l].reshape(batch, 6, d)
        g1 = g_norm1[l].reshape(1, d)
        cols = _in_projection(x2d, mod_l, g1, w_in_k[l], seq, BF16, 512, N_COLS, "in_projection")
        dil_cols = _in_projection(x2d, mod_l, g1, w_dil_k[l], seq, F32, 1024, N_DIL_COLS, "in_projection_dil")

        wuq, wukv, gq_slot, gkn_slot, gkr_slot = _layout_mla(w_uq[l], w_ukv[l], g_q_mla[l], g_k_mla[l])
        qm, km, vm = _mla_prep(cols, cos_t, sin_t, g_cq[l].reshape(1, -1), g_ckv[l].reshape(1, -1),
                               wuq, wukv, gq_slot, gkn_slot, gkr_slot, mseg_slot)
        o_mla = _mla_attention(qm, km, vm, batch, seq)

        gq_dil = (jnp.tile(g_q_dil[l], 2) * head_scale).reshape(1, -1)
        gk_dil = jnp.tile(g_k_dil[l], 2).reshape(1, -1)
        o_dil, lse_dil = [], []
        for gi in range(DIL_GROUPS):
            o_g, lse_g = _dilated_group(dil_cols, pos, gq_dil, gk_dil, mseg128, gi, batch, seq)
            o_dil.append(o_g)
            lse_dil.append(lse_g)

        gq_win = (jnp.tile(g_q_win[l], 2) * head_scale).reshape(1, -1)
        gk_win = jnp.tile(g_k_win[l], WIN_KV_HEADS).reshape(1, -1)
        sink_row = jnp.repeat(sink_win[l].astype(F32) * LOG2E, win_bq).reshape(1, WIN_HEADS * win_bq)
        o_win = _window_mixer(cols, pos, gq_win, gk_win, sink_row, mseg128, batch, seq)

        x2d = _merge(cols, o_mla, o_dil, lse_dil, o_win, x2d, mod_l, w_br_mla[l].astype(BF16),
                     w_br_dil[l].astype(BF16), w_br_win[l].astype(BF16), w_out[l].astype(BF16), seq)
        x2d = _moe(x2d, mod_l, g_norm2[l].reshape(1, d), w_gr[l], b_gr[l], w_er[l], b_er[l],
                   w1[l], w3[l], w2[l], seq)
    return x2d.reshape(batch, seq, d)
```

```python
import functools
import math

import jax
import jax.numpy as jnp
from jax import lax
from jax.experimental import pallas as pl
from jax.experimental.pallas import tpu as pltpu

F32 = jnp.float32
BF16 = jnp.bfloat16
I32 = jnp.int32

D_MODEL = 1024
HEAD_DIM = 64
NEG_INF = -1e30
EPS = 1e-6
LOG2E = math.log2(math.e)
LN2 = math.log(2.0)
MLA_HEADS = 8
MLA_Q_RANK = 512
MLA_KV_RANK = 256
MLA_NOPE = 64
MLA_ROPE = 32
MLA_V = 64
ROPE_THETA = 10000.0
DIL_PATTERNS = ((128, 1), (512, 4), (2048, 16))
DIL_GROUPS = 3
DIL_HPG = 4
DIL_HEADS = DIL_GROUPS * DIL_HPG
DIL_RADIUS = 64
WIN_HEADS = 8
WIN_KV_HEADS = 2
WIN_RADIUS = 128
N_EXPERT_GROUPS = 4
EXPERTS_PER_GROUP = 8
N_EXPERTS = N_EXPERT_GROUPS * EXPERTS_PER_GROUP
D_EXPERT = 384

LANES = 128
SLOT = 128
VMEM_LIMIT = 48 * 1024 * 1024

OFF_GATE = 0
OFF_CQ = 3072
OFF_WQ = 3584
OFF_CKV = 4096
OFF_WK = 4352
OFF_WV = 4480
OFF_KR = 4608
N_COLS = 4864
N_DIL_COLS = 3 * DIL_HEADS * HEAD_DIM

MOE_BM = 512
MOE_CHUNK = 16
MLA_TQ = 512
MLA_HPS = 8


def _cparams(sem, vmem=None, flags=None):
    return pltpu.CompilerParams(dimension_semantics=sem, vmem_limit_bytes=vmem, flags=flags)


def _tile(n, pref):
    t = min(n, pref)
    assert n % t == 0, (n, pref)
    return t


def _dot(a, b):
    return jnp.dot(a, b, preferred_element_type=F32)


def _dot_nt(a, b):
    return lax.dot_general(a, b, (((1,), (1,)), ((), ())), preferred_element_type=F32)


def _dot_tn(a, b):
    return lax.dot_general(a, b, (((0,), (0,)), ((), ())), preferred_element_type=F32)


def _seg_mean_sq(x, mseg):
    x2 = x * x
    hi = x2.astype(BF16)
    lo = (x2 - hi.astype(F32)).astype(BF16)
    return _dot(hi, mseg) + _dot(lo, mseg)


def _seg_norm(x, mseg, gain):
    return x * lax.rsqrt(_seg_mean_sq(x, mseg) + EPS) * gain


def _rms(x, gain):
    ms = jnp.mean(x * x, axis=-1, keepdims=True)
    return x * lax.rsqrt(ms + EPS) * gain


def _mod_kernel(c_ref, w_ref, b_ref, o_ref):
    c = c_ref[...]
    cond = (c * jax.nn.sigmoid(c)).astype(BF16)
    o_ref[0] = _dot(cond, w_ref[0].astype(BF16)) + b_ref[0]


def _modulation(c, w_ada, b_ada):
    depth, d, n = w_ada.shape
    b = c.shape[0]
    tn = _tile(n, 1536)
    return pl.pallas_call(
        _mod_kernel,
        out_shape=jax.ShapeDtypeStruct((depth, b, n), F32),
        grid=(depth, n // tn),
        in_specs=[
            pl.BlockSpec((b, d), lambda l, j: (0, 0)),
            pl.BlockSpec((1, d, tn), lambda l, j: (l, 0, j)),
            pl.BlockSpec((1, 1, tn), lambda l, j: (l, 0, j)),
        ],
        out_specs=pl.BlockSpec((1, b, tn), lambda l, j: (l, 0, j)),
        compiler_params=_cparams(("parallel", "parallel")),
        name="adaln_mod",
    )(c, w_ada, b_ada.reshape(depth, 1, n))


def _rope_table_kernel(pos_ref, invf_ref, cos_ref, sin_ref):
    ang = pos_ref[...].astype(F32) * invf_ref[...]
    cos_ref[...] = jnp.cos(ang)
    sin_ref[...] = jnp.sin(ang)


def _rope_tables(pos_col, invf_slot):
    t = pos_col.shape[0]
    tm = _tile(t, 1024)
    return pl.pallas_call(
        _rope_table_kernel,
        out_shape=(jax.ShapeDtypeStruct((t, SLOT), F32), jax.ShapeDtypeStruct((t, SLOT), F32)),
        grid=(t // tm,),
        in_specs=[pl.BlockSpec((tm, 1), lambda i: (i, 0)), pl.BlockSpec((1, SLOT), lambda i: (0, 0))],
        out_specs=(pl.BlockSpec((tm, SLOT), lambda i: (i, 0)), pl.BlockSpec((tm, SLOT), lambda i: (i, 0))),
        compiler_params=_cparams(("parallel",)),
        name="rope_tables",
    )(pos_col, invf_slot)


def _inproj_kernel(x_ref, mod_ref, g_ref, w_ref, o_ref, h_scr):
    @pl.when(pl.program_id(1) == 0)
    def _():
        m = mod_ref[0]
        h = _rms(x_ref[...], g_ref[...]) * (1.0 + m[1:2]) + m[0:1]
        h_scr[...] = h.astype(BF16)

    o_ref[...] = _dot(h_scr[...], w_ref[...]).astype(o_ref.dtype)


def _in_projection(x2d, mod_l, g_norm, w_in_l, seq, out_dtype, tm, tn, name):
    t, d = x2d.shape
    nc = w_in_l.shape[1]
    tm = _tile(seq, tm)
    assert nc % tn == 0
    per_b = seq // tm
    return pl.pallas_call(
        _inproj_kernel,
        out_shape=jax.ShapeDtypeStruct((t, nc), out_dtype),
        grid=(t // tm, nc // tn),
        in_specs=[
            pl.BlockSpec((tm, d), lambda i, j: (i, 0)),
            pl.BlockSpec((1, 6, d), lambda i, j: (i // per_b, 0, 0)),
            pl.BlockSpec((1, d), lambda i, j: (0, 0)),
            pl.BlockSpec((d, tn), lambda i, j: (0, j)),
        ],
        out_specs=pl.BlockSpec((tm, tn), lambda i, j: (i, j)),
        scratch_shapes=[pltpu.VMEM((tm, d), BF16)],
        compiler_params=_cparams(("parallel", "arbitrary"), VMEM_LIMIT),
        name=name,
    )(x2d, mod_l, g_norm, w_in_l)


def _mla_prep_kernel(cq_ref, ckv_ref, kr_ref, cos_ref, sin_ref, gcq_ref, gckv_ref, wuq_ref, wukv_ref,
                     gq_ref, gkn_ref, gkr_ref, mseg_ref, q_out, k_out, v_out):
    cos = cos_ref[...]
    sin = sin_ref[...]
    lane = lax.broadcasted_iota(I32, (1, SLOT), 1)
    s_neg = jnp.where((lane >= 64) & (lane < 80), -sin, 0.0)
    s_pos = jnp.where((lane >= 80) & (lane < 96), sin, 0.0)
    mseg = mseg_ref[...]

    def rope(xn):
        return xn * cos + pltpu.roll(xn, SLOT - 16, 1) * s_neg + pltpu.roll(xn, 16, 1) * s_pos

    cqn = _rms(cq_ref[...].astype(F32), gcq_ref[...]).astype(BF16)
    q = _dot(cqn, wuq_ref[...])
    for h in range(MLA_HEADS):
        sl = slice(h * SLOT, (h + 1) * SLOT)
        q_out[:, sl] = rope(_seg_norm(q[:, sl], mseg, gq_ref[...])).astype(q_out.dtype)

    ckvn = _rms(ckv_ref[...].astype(F32), gckv_ref[...]).astype(BF16)
    kv = _dot(ckvn, wukv_ref[...])
    kr = rope(_seg_norm(kr_ref[...].astype(F32), mseg, gkr_ref[...]))
    for h in range(MLA_HEADS):
        sl = slice(h * SLOT, (h + 1) * SLOT)
        k_out[:, sl] = (_seg_norm(kv[:, sl], mseg, gkn_ref[...]) + kr).astype(k_out.dtype)
    v_out[...] = kv[:, MLA_HEADS * SLOT:].astype(v_out.dtype)


def _mla_prep(cols, cos_t, sin_t, gcq, gckv, wuq, wukv, gq_slot, gkn_slot, gkr_slot, mseg):
    t = cols.shape[0]
    tm = _tile(t, 512)
    hs = MLA_HEADS * SLOT
    full = lambda shape: pl.BlockSpec(shape, lambda i: (0,) * len(shape))
    return pl.pallas_call(
        _mla_prep_kernel,
        out_shape=(jax.ShapeDtypeStruct((t, hs), BF16), jax.ShapeDtypeStruct((t, hs), BF16),
                   jax.ShapeDtypeStruct((t, MLA_HEADS * MLA_V), BF16)),
        grid=(t // tm,),
        in_specs=[
            pl.BlockSpec((tm, MLA_Q_RANK), lambda i: (i, OFF_CQ // MLA_Q_RANK)),
            pl.BlockSpec((tm, MLA_KV_RANK), lambda i: (i, OFF_CKV // MLA_KV_RANK)),
            pl.BlockSpec((tm, SLOT), lambda i: (i, OFF_KR // SLOT)),
            pl.BlockSpec((tm, SLOT), lambda i: (i, 0)),
            pl.BlockSpec((tm, SLOT), lambda i: (i, 0)),
            full((1, MLA_Q_RANK)), full((1, MLA_KV_RANK)),
            full((MLA_Q_RANK, hs)), full((MLA_KV_RANK, hs + MLA_HEADS * MLA_V)),
            full((1, SLOT)), full((1, SLOT)), full((1, SLOT)), full((SLOT, SLOT)),
        ],
        out_specs=(pl.BlockSpec((tm, hs), lambda i: (i, 0)), pl.BlockSpec((tm, hs), lambda i: (i, 0)),
                   pl.BlockSpec((tm, MLA_HEADS * MLA_V), lambda i: (i, 0))),
        compiler_params=_cparams(("parallel",), VMEM_LIMIT),
        name="mla_prep",
    )(cols, cols, cols, cos_t, sin_t, gcq, gckv, wuq, wukv, gq_slot, gkn_slot, gkr_slot, mseg)


def _mla_attn_kernel(q_ref, k_ref, v_ref, o_ref):
    lane = lax.broadcasted_iota(I32, (1, 2 * MLA_V), 1)
    for pair in range(MLA_HPS // 2):
        outs = []
        vp = v_ref[0, :, pair * 2 * MLA_V:(pair + 1) * 2 * MLA_V]
        for a in range(2):
            sl = slice((2 * pair + a) * SLOT, (2 * pair + a + 1) * SLOT)
            s = _dot_nt(q_ref[0, :, sl], k_ref[0, :, sl])
            m = jnp.max(s, axis=-1, keepdims=True)
            p = jnp.exp2(s - m)
            l = jnp.sum(p, axis=-1, keepdims=True)
            outs.append(_dot(p.astype(BF16), vp) / l)
        o_ref[0, :, pair * 2 * MLA_V:(pair + 1) * 2 * MLA_V] = jnp.where(lane < MLA_V, outs[0], outs[1]).astype(o_ref.dtype)


def _mla_attention(qm, km, vm, batch, seq):
    hs = MLA_HEADS * SLOT
    q3 = qm.reshape(batch, seq, hs)
    k3 = km.reshape(batch, seq, hs)
    v3 = vm.reshape(batch, seq, MLA_HEADS * MLA_V)
    tq = _tile(seq, MLA_TQ)
    out = pl.pallas_call(
        _mla_attn_kernel,
        out_shape=jax.ShapeDtypeStruct((batch, seq, MLA_HEADS * MLA_V), BF16),
        grid=(batch, MLA_HEADS // MLA_HPS, seq // tq),
        in_specs=[
            pl.BlockSpec((1, tq, MLA_HPS * SLOT), lambda b, p, i: (b, i, p)),
            pl.BlockSpec((1, seq, MLA_HPS * SLOT), lambda b, p, i: (b, 0, p)),
            pl.BlockSpec((1, seq, MLA_HPS * MLA_V), lambda b, p, i: (b, 0, p)),
        ],
        out_specs=pl.BlockSpec((1, tq, MLA_HPS * MLA_V), lambda b, p, i: (b, i, p)),
        compiler_params=_cparams(("parallel", "parallel", "arbitrary"), VMEM_LIMIT),
        name="mla_attention",
    )(q3, k3, v3)
    return out.reshape(batch * seq, MLA_HEADS * MLA_V)


WIN_BLOCKS_PER_STEP = 8
BAND_UNROLL = 16


def _window_start(i, bq, radius, n, kw):
    ws = jnp.clip(i * bq - radius, 0, n - kw)
    return pl.multiple_of(ws, 16)


MASKED_DIST = 1e30


def _masked_dist_t(i, bq, ws, kw, radius, pos_keys, pos_queries):
    keys = lax.broadcasted_iota(I32, (kw, bq), 0)
    queries = lax.broadcasted_iota(I32, (kw, bq), 1)
    rel = queries - keys + (i * bq - ws)
    d = pos_keys - pos_queries
    return jnp.where(jnp.maximum(rel, -rel) <= radius, jnp.maximum(d, -d).astype(F32), MASKED_DIST)


def _sub_positions(pos, batch, n, dil, bq):
    pos_sub = jnp.transpose(pos.reshape(batch, n, dil), (0, 2, 1))
    return pos_sub.reshape(batch, dil * n, 1), pos_sub.reshape(batch, dil * (n // bq), 1, bq)


def _dil_kernel(q0_ref, q1_ref, k0_ref, k1_ref, v0_ref, v1_ref, pcol_ref, prow_ref, gq_ref, gk_ref, mseg_ref,
                o_ref, lse_ref, kn_scr, vn_scr, *, n, bq, kw, dil, slopes):
    q_refs, k_refs, v_refs = (q0_ref, q1_ref), (k0_ref, k1_ref), (v0_ref, v1_ref)
    nblk = n // bq
    mseg = mseg_ref[...]
    gq = gq_ref[...]
    low_lane = lax.broadcasted_iota(I32, (1, 2 * HEAD_DIM), 1) < HEAD_DIM

    def rows(start, size):
        return pl.ds(start, size) if dil == 1 else pl.ds(start, size, stride=dil)

    def fill(r, carry):
        base = pl.multiple_of(r * n, 16)
        for hf in range(2):
            kn_scr[hf, pl.ds(base, n), :] = _seg_norm(k_refs[hf][0, rows(r, n), :], mseg, gk_ref[...]).astype(BF16)
            vn_scr[hf, pl.ds(base, n), :] = v_refs[hf][0, rows(r, n), :].astype(BF16)
        return carry

    lax.fori_loop(0, dil, fill, 0)

    def query_block(b, carry):
        r = b >> (nblk.bit_length() - 1)
        i = b & (nblk - 1)
        ws = _window_start(i, bq, DIL_RADIUS, n, kw)
        start = pl.multiple_of(r * n + ws, 16)
        q_rows = rows(r + i * (bq * dil), bq)
        dist = _masked_dist_t(i, bq, ws, kw, DIL_RADIUS, pcol_ref[0, pl.ds(start, kw), :], prow_ref[0, r * nblk + i])
        s_halves = []
        for hf in range(2):
            qn = _seg_norm(q_refs[hf][0, q_rows, :], mseg, gq)
            qs = jnp.concatenate([jnp.where(low_lane, qn, 0.0), jnp.where(low_lane, 0.0, qn)], axis=0).astype(BF16)
            bias = jnp.concatenate([slopes[2 * hf] * dist, slopes[2 * hf + 1] * dist], axis=1)
            s_halves.append(_dot_nt(kn_scr[hf, pl.ds(start, kw), :], qs) - bias)
        s = jnp.concatenate(s_halves, axis=1)
        m = jnp.max(s, axis=0, keepdims=True)
        p = jnp.exp2(s - m)
        l = jnp.sum(p, axis=0, keepdims=True)
        lse = (m + jnp.log2(l)) * LN2
        pb = p.astype(BF16)
        for hf in range(2):
            c0, c1, c2 = 2 * hf * bq, (2 * hf + 1) * bq, (2 * hf + 2) * bq
            ot = _dot_tn(vn_scr[hf, pl.ds(start, kw), :], pb[:, c0:c2]) / l[:, c0:c2]
            o_ref[0, hf, q_rows, :] = jnp.concatenate([ot[:HEAD_DIM, :bq], ot[HEAD_DIM:, bq:]], axis=0).T
            lse_ref[0, hf, q_rows, :] = jnp.concatenate([jnp.broadcast_to(lse[:, c0:c1], (HEAD_DIM, bq)),
                                                         jnp.broadcast_to(lse[:, c1:c2], (HEAD_DIM, bq))], axis=0).T
        return carry

    lax.fori_loop(0, dil * nblk, query_block, 0, unroll=min(dil * nblk, BAND_UNROLL))


def _dilated_group(dil_cols, pos, gq_row, gk_row, mseg, gi, batch, seq):
    _, dil = DIL_PATTERNS[gi]
    n = seq // dil
    bq = min(128, n)
    kw = min(bq + 2 * DIL_RADIUS, n)
    nblk = n // bq
    hw = 2 * HEAD_DIM
    cols3 = dil_cols.reshape(batch, seq, N_DIL_COLS)
    pcol, prow = _sub_positions(pos, batch, n, dil, bq)
    slopes = tuple(float(LOG2E * 2.0 ** (-8.0 * (gi * DIL_HPG + hh + 1) / DIL_HEADS)) for hh in range(DIL_HPG))
    kern = functools.partial(_dil_kernel, n=n, bq=bq, kw=kw, dil=dil, slopes=slopes)
    col = lambda which, hf: pl.BlockSpec((1, seq, hw), lambda b: (b, 0, (which * DIL_GROUPS + gi) * 2 + hf))
    small = lambda shape: pl.BlockSpec(shape, lambda b: (0,) * len(shape))
    halves = jax.ShapeDtypeStruct((batch, 2, seq, hw), F32)
    return pl.pallas_call(
        kern,
        out_shape=(halves, halves),
        grid=(batch,),
        in_specs=[
            col(0, 0), col(0, 1), col(1, 0), col(1, 1), col(2, 0), col(2, 1),
            pl.BlockSpec((1, seq, 1), lambda b: (b, 0, 0)),
            pl.BlockSpec((1, dil * nblk, 1, bq), lambda b: (b, 0, 0, 0)),
            small((1, hw)), small((1, hw)), small((hw, hw)),
        ],
        out_specs=(pl.BlockSpec((1, 2, seq, hw), lambda b: (b, 0, 0, 0)),
                   pl.BlockSpec((1, 2, seq, hw), lambda b: (b, 0, 0, 0))),
        scratch_shapes=[pltpu.VMEM((2, seq, hw), BF16), pltpu.VMEM((2, seq, hw), BF16)],
        compiler_params=_cparams(("parallel",), VMEM_LIMIT),
        name=f"dilated_group{gi}",
    )(cols3, cols3, cols3, cols3, cols3, cols3, pcol, prow, gq_row, gk_row, mseg)


def _win_kernel(q_ref, k_ref, v_ref, pcol_ref, prow_ref, gq_ref, gk_ref, sink_ref, mseg_ref, o_ref, kn_scr,
                *, n, bq, kw, slopes):
    step = pl.program_id(1)
    mseg = mseg_ref[...]

    @pl.when(step == 0)
    def _():
        kn_scr[...] = _seg_norm(k_ref[0].astype(F32), mseg, gk_ref[...]).astype(BF16)

    low_lane = lax.broadcasted_iota(I32, (1, 2 * HEAD_DIM), 1) < HEAD_DIM
    rep = WIN_HEADS // WIN_KV_HEADS
    sink = sink_ref[...]
    per_step = q_ref.shape[1] // bq
    for j in range(per_step):
        i = step * per_step + j
        rows = slice(j * bq, (j + 1) * bq)
        ws = _window_start(i, bq, WIN_RADIUS, n, kw)
        kwin = kn_scr[pl.ds(ws, kw), :]
        vwin = v_ref[0, pl.ds(ws, kw), :]
        dist = _masked_dist_t(i, bq, ws, kw, WIN_RADIUS, pcol_ref[0, pl.ds(ws, kw), :], prow_ref[0, j])

        qn_pairs = [_seg_norm(q_ref[0, rows, pair * 128:(pair + 1) * 128].astype(F32), mseg, gq_ref[...])
                    for pair in range(WIN_HEADS // 2)]
        q_parts = []
        for h in range(WIN_HEADS):
            pair, upper = divmod(h, 2)
            qm = jnp.where(low_lane, 0.0, qn_pairs[pair]) if upper else jnp.where(low_lane, qn_pairs[pair], 0.0)
            if upper != h // rep:
                qm = pltpu.roll(qm, HEAD_DIM, 1)
            q_parts.append(qm.astype(BF16))
        qs = jnp.concatenate(q_parts, axis=0)
        bias = jnp.concatenate([slopes[h] * dist for h in range(WIN_HEADS)], axis=1)
        s = _dot_nt(kwin, qs) - bias
        m = jnp.maximum(jnp.max(s, axis=0, keepdims=True), sink)
        e = jnp.exp2(s - m)
        den = jnp.sum(e, axis=0, keepdims=True) + jnp.exp2(sink - m)
        ot = _dot_tn(vwin, e.astype(BF16)) / den
        for pair in range(WIN_HEADS // 2):
            slabs = []
            for h in (2 * pair, 2 * pair + 1):
                g = h // rep
                slabs.append(ot[g * HEAD_DIM:(g + 1) * HEAD_DIM, h * bq:(h + 1) * bq])
            o_ref[0, rows, pair * 128:(pair + 1) * 128] = jnp.concatenate(slabs, axis=0).T.astype(o_ref.dtype)


def _window_mixer(cols, pos, gq_row, gk_row, sink_row, mseg128, batch, seq):
    n = seq
    bq = min(128, n)
    kw = min(bq + 2 * WIN_RADIUS, n)
    nblk = n // bq
    qw = WIN_HEADS * HEAD_DIM
    kvw = WIN_KV_HEADS * HEAD_DIM
    cols_v = cols.reshape(batch, seq, N_COLS)
    pcol = pos.reshape(batch, seq, 1)
    prow = pos.reshape(batch, nblk, 1, bq)
    slopes = tuple(float(LOG2E * 2.0 ** (-8.0 * (h + 1) / WIN_HEADS)) for h in range(WIN_HEADS))
    kern = functools.partial(_win_kernel, n=n, bq=bq, kw=kw, slopes=slopes)
    small = lambda shape: pl.BlockSpec(shape, lambda b, i: (0,) * len(shape))
    per_step = min(WIN_BLOCKS_PER_STEP, nblk)
    o = pl.pallas_call(
        kern,
        out_shape=jax.ShapeDtypeStruct((batch, seq, qw), BF16),
        grid=(batch, nblk // per_step),
        in_specs=[
            pl.BlockSpec((1, per_step * bq, qw), lambda b, i: (b, i, OFF_WQ // qw)),
            pl.BlockSpec((1, n, kvw), lambda b, i: (b, 0, OFF_WK // kvw)),
            pl.BlockSpec((1, n, kvw), lambda b, i: (b, 0, OFF_WV // kvw)),
            pl.BlockSpec((1, n, 1), lambda b, i: (b, 0, 0)),
            pl.BlockSpec((1, per_step, 1, bq), lambda b, i: (b, i, 0, 0)),
            small((1, kvw)), small((1, kvw)), small((1, WIN_HEADS * bq)), small((kvw, kvw)),
        ],
        out_specs=pl.BlockSpec((1, per_step * bq, qw), lambda b, i: (b, i, 0)),
        scratch_shapes=[pltpu.VMEM((n, kvw), BF16)],
        compiler_params=_cparams(("parallel", "arbitrary"), VMEM_LIMIT),
        name="window_mixer",
    )(cols_v, cols_v, cols_v, pcol, prow, gq_row, gk_row, sink_row, mseg128)
    return o.reshape(batch * seq, qw)


def _merge_kernel(gm_ref, gd_ref, gw_ref, om_ref, od0_ref, od1_ref, od2_ref, l0_ref, l1_ref, l2_ref, ow_ref,
                  x_ref, mod_ref, wm_ref, wd_ref, ww_ref, wo_ref, o_ref):
    def heads(ref):
        return jnp.concatenate([ref[0, 0], ref[0, 1]], axis=1)

    l0, l1, l2 = heads(l0_ref), heads(l1_ref), heads(l2_ref)
    m = jnp.maximum(jnp.maximum(l0, l1), l2)
    e0, e1, e2 = jnp.exp(l0 - m), jnp.exp(l1 - m), jnp.exp(l2 - m)
    od = (e0 * heads(od0_ref) + e1 * heads(od1_ref) + e2 * heads(od2_ref)) / (e0 + e1 + e2)
    y = jax.nn.sigmoid(gm_ref[...].astype(F32)) * _dot(om_ref[...], wm_ref[...])
    y += jax.nn.sigmoid(gd_ref[...].astype(F32)) * _dot(od.astype(BF16), wd_ref[...])
    y += jax.nn.sigmoid(gw_ref[...].astype(F32)) * _dot(ow_ref[...], ww_ref[...])
    z = _dot(y.astype(BF16), wo_ref[...])
    o_ref[...] = x_ref[...] + mod_ref[0][2:3] * z


def _merge(cols, o_mla, o_dil, lse_dil, o_win, x2d, mod_l, wm, wd, ww, wo, seq):
    t, d = x2d.shape
    tm = _tile(seq, 512)
    per_b = seq // tm
    row = lambda w, j=0: pl.BlockSpec((tm, w), lambda i: (i, j))
    full = lambda a: pl.BlockSpec(a.shape, lambda i: (0, 0))
    dil = pl.BlockSpec((1, 2, tm, 2 * HEAD_DIM), lambda i: (i // per_b, 0, i % per_b, 0))
    return pl.pallas_call(
        _merge_kernel,
        out_shape=jax.ShapeDtypeStruct((t, d), F32),
        grid=(t // tm,),
        in_specs=[row(d, 0), row(d, 1), row(d, 2), row(o_mla.shape[1]),
                  dil, dil, dil, dil, dil, dil, row(o_win.shape[1]),
                  row(d), pl.BlockSpec((1, 6, d), lambda i: (i // per_b, 0, 0)),
                  full(wm), full(wd), full(ww), full(wo)],
        out_specs=row(d),
        compiler_params=_cparams(("parallel",), VMEM_LIMIT),
        name="merge_out_proj",
    )(cols, cols, cols, o_mla, o_dil[0], o_dil[1], o_dil[2], lse_dil[0], lse_dil[1], lse_dil[2], o_win,
      x2d, mod_l, wm, wd, ww, wo)


def _router_kernel(x_ref, mod_ref, g_ref, whi_ref, wlo_ref, br_ref, upper_ref, hs_ref, pos_ref, gate_ref, tab_ref,
                   *, lr):
    m = mod_ref[0]
    h = _rms(x_ref[...], g_ref[...]) * (1.0 + m[4:5]) + m[3:4]
    tm = h.shape[0]
    h_hi = h.astype(BF16)
    h_lo = (h - h_hi.astype(F32)).astype(BF16)
    logits = _dot(h_hi, whi_ref[...]) + (_dot(h_hi, wlo_ref[...]) + _dot(h_lo, whi_ref[...])) + br_ref[...]
    lane = lax.broadcasted_iota(I32, (tm, LANES), 1)
    lane_f = lane.astype(F32)

    def first_argmax(vals, mx):
        return jnp.min(jnp.where(vals == mx, lane_f, float(LANES)), axis=-1, keepdims=True).astype(I32)

    lg = jnp.where(lane < N_EXPERT_GROUPS, logits, NEG_INF)
    mg = jnp.max(lg, axis=-1, keepdims=True)
    g_w = 1.0 / jnp.sum(jnp.exp(lg - mg), axis=-1, keepdims=True)
    g_idx = first_argmax(lg, mg)
    eid = lane - N_EXPERT_GROUPS
    in_grp = (eid >= 0) & (eid < N_EXPERTS) & ((eid >> 3) == g_idx)
    le = jnp.where(in_grp, logits, NEG_INF)
    m1 = jnp.max(le, axis=-1, keepdims=True)
    i1 = first_argmax(le, m1)
    le2 = jnp.where(lane == i1, NEG_INF, le)
    m2 = jnp.max(le2, axis=-1, keepdims=True)
    i2 = first_argmax(le2, m2)
    r = jnp.exp(m2 - m1)
    gate1 = g_w / (1.0 + r)
    gate2 = g_w * r / (1.0 + r)
    e1 = i1 - N_EXPERT_GROUPS
    e2 = i2 - N_EXPERT_GROUPS
    hit1 = lane == e1
    hit2 = lane == e2
    onehot = jnp.where(hit1 | hit2, 1.0, 0.0)
    rows = lax.broadcasted_iota(I32, (tm, tm), 0)
    cols = lax.broadcasted_iota(I32, (tm, tm), 1)
    before = jnp.where(rows > cols, 1.0, 0.0).astype(BF16)
    rank = _dot(before, onehot.astype(BF16))
    run_len = (jnp.sum(onehot, axis=0, keepdims=True).astype(I32) + (MOE_CHUNK - 1)) & jnp.int32(-MOE_CHUNK)
    run_off = _dot(jnp.broadcast_to(run_len.astype(F32), (8, LANES)).astype(BF16), upper_ref[...])[0:1]
    pos1 = jnp.sum(jnp.where(hit1, rank + run_off, 0.0), axis=-1, keepdims=True).astype(I32)
    pos2 = jnp.sum(jnp.where(hit2, rank + run_off, 0.0), axis=-1, keepdims=True).astype(I32)
    pos_lanes = jnp.where(lane == 0, pos1, jnp.where(lane == 1, pos2, 0))
    pos_rows = pos_lanes.astype(F32).T.astype(I32)
    local = lax.broadcasted_iota(I32, (lr, tm), 0)
    place = jnp.where((local == pos_rows[0:1]) | (local == pos_rows[1:2]), 1.0, 0.0).astype(BF16)
    hs_ref[0] = _dot(place, h_hi).astype(BF16)
    pos_ref[...] = pos_lanes
    gate_ref[...] = jnp.where(lane == 0, gate1, jnp.where(lane == 1, gate2, 0.0))
    sub = lax.broadcasted_iota(I32, (8, LANES), 0)
    tab_ref[0] = jnp.where(sub == 0, run_len, jnp.where(sub == 1, run_off.astype(I32), 0))


def _router(x2d, mod_l, g_norm, w_router, b_router, seq, tm, lr):
    t, d = x2d.shape
    per_b = seq // tm
    nt = t // tm
    row = lambda w: pl.BlockSpec((tm, w), lambda i: (i, 0))
    idx = jnp.arange(LANES)
    upper = jnp.where(idx[:, None] < idx[None, :], 1.0, 0.0).astype(BF16)
    w_hi = w_router.astype(BF16)
    w_lo = (w_router - w_hi.astype(F32)).astype(BF16)
    return pl.pallas_call(
        functools.partial(_router_kernel, lr=lr),
        out_shape=(jax.ShapeDtypeStruct((nt, lr, d), BF16), jax.ShapeDtypeStruct((t, LANES), I32),
                   jax.ShapeDtypeStruct((t, LANES), F32), jax.ShapeDtypeStruct((nt, 8, LANES), I32)),
        grid=(nt,),
        in_specs=[row(d), pl.BlockSpec((1, 6, d), lambda i: (i // per_b, 0, 0)),
                  pl.BlockSpec((1, d), lambda i: (0, 0)), pl.BlockSpec((d, LANES), lambda i: (0, 0)),
                  pl.BlockSpec((d, LANES), lambda i: (0, 0)),
                  pl.BlockSpec((1, LANES), lambda i: (0, 0)), pl.BlockSpec((LANES, LANES), lambda i: (0, 0))],
        out_specs=(pl.BlockSpec((1, lr, d), lambda i: (i, 0, 0)), row(LANES), row(LANES),
                   pl.BlockSpec((1, 8, LANES), lambda i: (i, 0, 0))),
        compiler_params=_cparams(("parallel",), VMEM_LIMIT),
        name="moe_router",
    )(x2d, mod_l, g_norm, w_hi, w_lo, b_router, upper)


def _for_each_chunk(off_ref, n_ref, base_ref, tile, fn):
    def per_expert(e, carry):
        k = tile * N_EXPERTS + e
        off, base = off_ref[k], base_ref[k]

        def per_chunk(c, carry2):
            fn(pl.multiple_of(off + c * MOE_CHUNK, MOE_CHUNK), pl.multiple_of(base + c * MOE_CHUNK, MOE_CHUNK))
            return carry2

        return lax.fori_loop(0, n_ref[k], per_chunk, carry)

    lax.fori_loop(0, N_EXPERTS, per_expert, 0)


def _scatter_kernel(off_ref, n_ref, base_ref, hs_ref, rows_in_ref, rows_ref, sem):
    del rows_in_ref
    tile = pl.program_id(0)

    def chunk_copy(local, glob):
        return pltpu.make_async_copy(hs_ref.at[0, pl.ds(local, MOE_CHUNK)], rows_ref.at[pl.ds(glob, MOE_CHUNK)], sem)

    _for_each_chunk(off_ref, n_ref, base_ref, tile, lambda a, b: chunk_copy(a, b).start())
    _for_each_chunk(off_ref, n_ref, base_ref, tile, lambda a, b: chunk_copy(a, b).wait())


def _scatter_rows(run_off, run_chunks, base, hs, n_rows):
    nt, lr, w = hs.shape
    zeros = jnp.zeros((n_rows, w), hs.dtype)
    grid_spec = pltpu.PrefetchScalarGridSpec(
        num_scalar_prefetch=3,
        grid=(nt,),
        in_specs=[pl.BlockSpec((1, lr, w), lambda i, *_: (i, 0, 0)), pl.BlockSpec(memory_space=pl.ANY)],
        out_specs=pl.BlockSpec(memory_space=pl.ANY),
        scratch_shapes=[pltpu.SemaphoreType.DMA],
    )
    return pl.pallas_call(
        _scatter_kernel,
        out_shape=jax.ShapeDtypeStruct((n_rows, w), hs.dtype),
        grid_spec=grid_spec,
        input_output_aliases={4: 0},
        compiler_params=_cparams(("arbitrary",)),
        name="moe_scatter",
    )(run_off, run_chunks, base, hs, zeros)


def _ffn_kernel(be_ref, nu_ref, rows_ref, w1_ref, w3_ref, w2_ref, y_ref, w13_scr, w2_scr):
    j = pl.program_id(0)
    used = j < nu_ref[0]
    new_expert = (j == 0) | (be_ref[j] != be_ref[jnp.maximum(j - 1, 0)])

    @pl.when(used & new_expert)
    def _():
        w13_scr[:, :D_EXPERT] = w1_ref[0].astype(BF16)
        w13_scr[:, D_EXPERT:] = w3_ref[0].astype(BF16)
        w2_scr[...] = w2_ref[0].astype(BF16)

    @pl.when(used)
    def _():
        h = _dot(rows_ref[...], w13_scr[...])
        a = h[:, :D_EXPERT]
        act = a * jax.nn.sigmoid(a) * h[:, D_EXPERT:]
        y_ref[...] = _dot(act.astype(BF16), w2_scr[...]).astype(y_ref.dtype)

    @pl.when(jnp.logical_not(used))
    def _():
        y_ref[...] = jnp.zeros_like(y_ref)


def _grouped_ffn(block_expert, n_used, rows, w1, w3, w2):
    n_rows, w = rows.shape
    d = w1.shape[1]
    nb = n_rows // MOE_BM
    grid_spec = pltpu.PrefetchScalarGridSpec(
        num_scalar_prefetch=2,
        grid=(nb,),
        in_specs=[
            pl.BlockSpec((MOE_BM, w), lambda j, be, nu: (j, 0)),
            pl.BlockSpec((1, d, D_EXPERT), lambda j, be, nu: (be[j], 0, 0)),
            pl.BlockSpec((1, d, D_EXPERT), lambda j, be, nu: (be[j], 0, 0)),
            pl.BlockSpec((1, D_EXPERT, d), lambda j, be, nu: (be[j], 0, 0)),
        ],
        out_specs=pl.BlockSpec((MOE_BM, w), lambda j, be, nu: (j, 0)),
        scratch_shapes=[pltpu.VMEM((d, 2 * D_EXPERT), BF16), pltpu.VMEM((D_EXPERT, d), BF16)],
    )
    return pl.pallas_call(
        _ffn_kernel,
        out_shape=jax.ShapeDtypeStruct((n_rows, w), BF16),
        grid_spec=grid_spec,
        compiler_params=_cparams(("arbitrary",), VMEM_LIMIT),
        name="moe_grouped_ffn",
    )(block_expert, n_used, rows, w1, w3, w2)


def _combine_kernel(off_ref, n_ref, base_ref, y_ref, pos_ref, gate_ref, x_ref, mod_ref, o_ref, ybuf, sem):
    tile = pl.program_id(0)
    slot = tile % 2

    def chunk_copy(s, local, glob):
        return pltpu.make_async_copy(y_ref.at[pl.ds(glob, MOE_CHUNK)], ybuf.at[s, pl.ds(local, MOE_CHUNK)], sem.at[s])

    def request(t, s):
        ybuf[s] = jnp.zeros(ybuf.shape[1:], ybuf.dtype)
        _for_each_chunk(off_ref, n_ref, base_ref, t, lambda a, b: chunk_copy(s, a, b).start())

    @pl.when(tile == 0)
    def _():
        request(tile, slot)

    @pl.when(tile + 1 < pl.num_programs(0))
    def _():
        request(tile + 1, 1 - slot)

    _for_each_chunk(off_ref, n_ref, base_ref, tile, lambda a, b: chunk_copy(slot, a, b).wait())
    tm = x_ref.shape[0]
    pos, g = pos_ref[...], gate_ref[...]
    local = lax.broadcasted_iota(I32, (tm, ybuf.shape[1]), 1)
    pick = (jnp.where(local == pos[:, 0:1], g[:, 0:1], 0.0)
            + jnp.where(local == pos[:, 1:2], g[:, 1:2], 0.0)).astype(BF16)
    moe = _dot(pick, ybuf[slot])
    o_ref[...] = x_ref[...] + mod_ref[0][5:6] * moe


def _combine(run_off, run_chunks, base, y, pos, gates, x2d, mod_l, seq, tm, lr):
    t, d = x2d.shape
    per_b = seq // tm
    grid_spec = pltpu.PrefetchScalarGridSpec(
        num_scalar_prefetch=3,
        grid=(t // tm,),
        in_specs=[pl.BlockSpec(memory_space=pl.ANY),
                  pl.BlockSpec((tm, LANES), lambda i, *_: (i, 0)), pl.BlockSpec((tm, LANES), lambda i, *_: (i, 0)),
                  pl.BlockSpec((tm, d), lambda i, *_: (i, 0)),
                  pl.BlockSpec((1, 6, d), lambda i, *_: (i // per_b, 0, 0))],
        out_specs=pl.BlockSpec((tm, d), lambda i, *_: (i, 0)),
        scratch_shapes=[pltpu.VMEM((2, lr, y.shape[1]), y.dtype), pltpu.SemaphoreType.DMA((2,))],
    )
    return pl.pallas_call(
        _combine_kernel,
        out_shape=jax.ShapeDtypeStruct((t, d), F32),
        grid_spec=grid_spec,
        compiler_params=_cparams(("arbitrary",), VMEM_LIMIT),
        name="moe_combine",
    )(run_off, run_chunks, base, y, pos, gates, x2d, mod_l)


def _moe(x2d, mod_l, g_norm2, w_gr, b_gr, w_er, b_er, w1, w3, w2, seq):
    t, d = x2d.shape
    pad = LANES - N_EXPERT_GROUPS - N_EXPERTS
    w_router = jnp.concatenate([w_gr, w_er, jnp.zeros((d, pad), F32)], axis=1)
    b_router = jnp.concatenate([b_gr, b_er, jnp.zeros((pad,), F32)]).reshape(1, LANES)
    tm = _tile(seq, 512)
    nt = t // tm
    lr = 2 * tm + N_EXPERTS * MOE_CHUNK
    hs, pos, gates, tab = _router(x2d, mod_l, g_norm2, w_router, b_router, seq, tm, lr)

    run_len = tab[:, 0, :N_EXPERTS]
    run_off = tab[:, 1, :N_EXPERTS]
    total = jnp.sum(run_len, axis=0)
    padded = (total + MOE_BM - 1) // MOE_BM * MOE_BM
    ends = jnp.cumsum(padded)
    base = (ends - padded)[None, :] + jnp.cumsum(run_len, axis=0) - run_len
    nb = (2 * t + nt * N_EXPERTS * MOE_CHUNK) // MOE_BM + N_EXPERTS
    block_start = jnp.arange(nb, dtype=I32) * MOE_BM
    block_expert = jnp.minimum(jnp.sum(block_start[:, None] >= ends[None, :], axis=1), N_EXPERTS - 1).astype(I32)
    n_used = (ends[-1] // MOE_BM).astype(I32).reshape(1)
    flat = lambda a: a.reshape(-1).astype(I32)
    run_off, run_chunks, base = flat(run_off), flat(run_len // MOE_CHUNK), flat(base)

    rows = _scatter_rows(run_off, run_chunks, base, hs, nb * MOE_BM)
    y = _grouped_ffn(block_expert, n_used, rows, w1, w3, w2)
    return _combine(run_off, run_chunks, base, y, pos, gates, x2d, mod_l, seq, tm, lr)


def _seg_matrix(width, segments):
    idx = jnp.arange(width)
    m = jnp.zeros((width, width), F32)
    for start, length in segments:
        inside = (idx >= start) & (idx < start + length)
        m = m + jnp.where(inside[:, None] & inside[None, :], 1.0 / length, 0.0)
    return m.astype(BF16)


def _layout_w_in(w_in):
    depth, d, _ = w_in.shape
    sizes = [MLA_Q_RANK, MLA_KV_RANK, MLA_ROPE, N_DIL_COLS,
             (WIN_HEADS + 2 * WIN_KV_HEADS) * HEAD_DIM, 3 * D_MODEL]
    bounds = [sum(sizes[:k + 1]) for k in range(len(sizes) - 1)]
    c_q, c_kv, k_rope, dil, win, gate = jnp.split(w_in.astype(BF16), bounds, axis=-1)
    win_q, win_kv = win[..., :WIN_HEADS * HEAD_DIM], win[..., WIN_HEADS * HEAD_DIM:]
    z = lambda w: jnp.zeros((depth, d, w), BF16)
    out = jnp.concatenate([gate, c_q, win_q, c_kv, win_kv, z(MLA_NOPE), k_rope,
                           z(N_COLS - OFF_KR - MLA_NOPE - MLA_ROPE)], axis=-1)
    assert out.shape[-1] == N_COLS
    return out, dil


def _layout_mla(w_uq, w_ukv, g_q, g_k):
    qd = MLA_NOPE + MLA_ROPE
    wq = w_uq.reshape(MLA_Q_RANK, MLA_HEADS, qd)
    wq = jnp.pad(wq, ((0, 0), (0, 0), (0, SLOT - qd))).reshape(MLA_Q_RANK, MLA_HEADS * SLOT)
    wkv = w_ukv.reshape(MLA_KV_RANK, MLA_HEADS, MLA_NOPE + MLA_V)
    wk = jnp.pad(wkv[:, :, :MLA_NOPE], ((0, 0), (0, 0), (0, SLOT - MLA_NOPE))).reshape(MLA_KV_RANK, MLA_HEADS * SLOT)
    wv = wkv[:, :, MLA_NOPE:].reshape(MLA_KV_RANK, MLA_HEADS * MLA_V)
    scale = LOG2E * float(qd) ** -0.5
    gq_slot = (jnp.pad(g_q, (0, SLOT - qd)) * scale).reshape(1, SLOT)
    gkn_slot = jnp.pad(g_k[:MLA_NOPE], (0, SLOT - MLA_NOPE)).reshape(1, SLOT)
    gkr_slot = jnp.pad(g_k[MLA_NOPE:], (MLA_NOPE, SLOT - qd)).reshape(1, SLOT)
    return wq.astype(BF16), jnp.concatenate([wk, wv], axis=1).astype(BF16), gq_slot, gkn_slot, gkr_slot


def kernel(x, c, pos, w_ada, b_ada, g_norm1, w_in, g_cq, w_uq, g_ckv, w_ukv, g_q_mla, g_k_mla, g_q_dil, g_k_dil,
           g_q_win, g_k_win, sink_win, w_br_mla, w_br_dil, w_br_win, w_out, g_norm2, w_gr, b_gr, w_er, b_er,
           w1, w3, w2):
    batch, seq, d = x.shape
    depth = w_ada.shape[0]
    t = batch * seq
    half = MLA_ROPE // 2
    inv_freq = ROPE_THETA ** (-jnp.arange(half, dtype=F32) / half)
    invf_slot = jnp.concatenate([jnp.zeros((MLA_NOPE,), F32), inv_freq, inv_freq,
                                 jnp.zeros((SLOT - MLA_NOPE - MLA_ROPE,), F32)]).reshape(1, SLOT)
    cos_t, sin_t = _rope_tables(pos.reshape(t, 1), invf_slot)
    mod = _modulation(c, w_ada, b_ada)
    w_in_k, w_dil_k = _layout_w_in(w_in)
    mseg_slot = _seg_matrix(SLOT, ((0, MLA_NOPE), (MLA_NOPE, MLA_ROPE)))
    mseg128 = _seg_matrix(128, tuple((k * HEAD_DIM, HEAD_DIM) for k in range(2)))
    head_scale = LOG2E * float(HEAD_DIM) ** -0.5
    win_bq = min(128, seq)

    x2d = x.reshape(t, d)
    for l in range(depth):
        mod_l = mod[l].reshape(batch, 6, d)
        g1 = g_norm1[l].reshape(1, d)
        cols = _in_projection(x2d, mod_l, g1, w_in_k[l], seq, BF16, 512, N_COLS, "in_projection")
        dil_cols = _in_projection(x2d, mod_l, g1, w_dil_k[l], seq, F32, 1024, N_DIL_COLS, "in_projection_dil")

        wuq, wukv, gq_slot, gkn_slot, gkr_slot = _layout_mla(w_uq[l], w_ukv[l], g_q_mla[l], g_k_mla[l])
        qm, km, vm = _mla_prep(cols, cos_t, sin_t, g_cq[l].reshape(1, -1), g_ckv[l].reshape(1, -1),
                               wuq, wukv, gq_slot, gkn_slot, gkr_slot, mseg_slot)
        o_mla = _mla_attention(qm, km, vm, batch, seq)

        gq_dil = (jnp.tile(g_q_dil[l], 2) * head_scale).reshape(1, -1)
        gk_dil = jnp.tile(g_k_dil[l], 2).reshape(1, -1)
        o_dil, lse_dil = [], []
        for gi in range(DIL_GROUPS):
            o_g, lse_g = _dilated_group(dil_cols, pos, gq_dil, gk_dil, mseg128, gi, batch, seq)
            o_dil.append(o_g)
            lse_dil.append(lse_g)

        gq_win = (jnp.tile(g_q_win[l], 2) * head_scale).reshape(1, -1)
        gk_win = jnp.tile(g_k_win[l], WIN_KV_HEADS).reshape(1, -1)
        sink_row = jnp.repeat(sink_win[l].astype(F32) * LOG2E, win_bq).reshape(1, WIN_HEADS * win_bq)
        o_win = _window_mixer(cols, pos, gq_win, gk_win, sink_row, mseg128, batch, seq)

        x2d = _merge(cols, o_mla, o_dil, lse_dil, o_win, x2d, mod_l, w_br_mla[l].astype(BF16),
                     w_br_dil[l].astype(BF16), w_br_win[l].astype(BF16), w_out[l].astype(BF16), seq)
        x2d = _moe(x2d, mod_l, g_norm2[l].reshape(1, d), w_gr[l], b_gr[l], w_er[l], b_er[l],
                   w1[l], w3[l], w2[l], seq)
    return x2d.reshape(batch, seq, d)
```

```python
import functools
import math

import jax
import jax.numpy as jnp
from jax import lax
from jax.experimental import pallas as pl
from jax.experimental.pallas import tpu as pltpu

F32 = jnp.float32
BF16 = jnp.bfloat16
I32 = jnp.int32

D_MODEL = 1024
HEAD_DIM = 64
NEG_INF = -1e30
EPS = 1e-6
LOG2E = math.log2(math.e)
LN2 = math.log(2.0)
MLA_HEADS = 8
MLA_Q_RANK = 512
MLA_KV_RANK = 256
MLA_NOPE = 64
MLA_ROPE = 32
MLA_V = 64
ROPE_THETA = 10000.0
DIL_PATTERNS = ((128, 1), (512, 4), (2048, 16))
DIL_GROUPS = 3
DIL_HPG = 4
DIL_HEADS = DIL_GROUPS * DIL_HPG
DIL_RADIUS = 64
WIN_HEADS = 8
WIN_KV_HEADS = 2
WIN_RADIUS = 128
N_EXPERT_GROUPS = 4
EXPERTS_PER_GROUP = 8
N_EXPERTS = N_EXPERT_GROUPS * EXPERTS_PER_GROUP
D_EXPERT = 384

LANES = 128
SLOT = 128
VMEM_LIMIT = 48 * 1024 * 1024

OFF_GATE = 0
OFF_CQ = 3072
OFF_WQ = 3584
OFF_CKV = 4096
OFF_WK = 4352
OFF_WV = 4480
OFF_KR = 4608
N_COLS = 4864
N_DIL_COLS = 3 * DIL_HEADS * HEAD_DIM

MOE_BM = 512
MOE_CHUNK = 16
MLA_TQ = 512
MLA_HPS = 8


def _cparams(sem, vmem=None, flags=None):
    return pltpu.CompilerParams(dimension_semantics=sem, vmem_limit_bytes=vmem, flags=flags)


def _tile(n, pref):
    t = min(n, pref)
    assert n % t == 0, (n, pref)
    return t


def _dot(a, b):
    return jnp.dot(a, b, preferred_element_type=F32)


def _dot_nt(a, b):
    return lax.dot_general(a, b, (((1,), (1,)), ((), ())), preferred_element_type=F32)


def _dot_tn(a, b):
    return lax.dot_general(a, b, (((0,), (0,)), ((), ())), preferred_element_type=F32)


def _seg_mean_sq(x, mseg):
    x2 = x * x
    hi = x2.astype(BF16)
    lo = (x2 - hi.astype(F32)).astype(BF16)
    return _dot(hi, mseg) + _dot(lo, mseg)


def _seg_norm(x, mseg, gain):
    return x * lax.rsqrt(_seg_mean_sq(x, mseg) + EPS) * gain


def _rms(x, gain):
    ms = jnp.mean(x * x, axis=-1, keepdims=True)
    return x * lax.rsqrt(ms + EPS) * gain


def _mod_kernel(c_ref, w_ref, b_ref, o_ref):
    c = c_ref[...]
    cond = (c * jax.nn.sigmoid(c)).astype(BF16)
    o_ref[0] = _dot(cond, w_ref[0].astype(BF16)) + b_ref[0]


def _modulation(c, w_ada, b_ada):
    depth, d, n = w_ada.shape
    b = c.shape[0]
    tn = _tile(n, 1536)
    return pl.pallas_call(
        _mod_kernel,
        out_shape=jax.ShapeDtypeStruct((depth, b, n), F32),
        grid=(depth, n // tn),
        in_specs=[
            pl.BlockSpec((b, d), lambda l, j: (0, 0)),
            pl.BlockSpec((1, d, tn), lambda l, j: (l, 0, j)),
            pl.BlockSpec((1, 1, tn), lambda l, j: (l, 0, j)),
        ],
        out_specs=pl.BlockSpec((1, b, tn), lambda l, j: (l, 0, j)),
        compiler_params=_cparams(("parallel", "parallel")),
        name="adaln_mod",
    )(c, w_ada, b_ada.reshape(depth, 1, n))


def _rope_table_kernel(pos_ref, invf_ref, cos_ref, sin_ref):
    ang = pos_ref[...].astype(F32) * invf_ref[...]
    cos_ref[...] = jnp.cos(ang)
    sin_ref[...] = jnp.sin(ang)


def _rope_tables(pos_col, invf_slot):
    t = pos_col.shape[0]
    tm = _tile(t, 1024)
    return pl.pallas_call(
        _rope_table_kernel,
        out_shape=(jax.ShapeDtypeStruct((t, SLOT), F32), jax.ShapeDtypeStruct((t, SLOT), F32)),
        grid=(t // tm,),
        in_specs=[pl.BlockSpec((tm, 1), lambda i: (i, 0)), pl.BlockSpec((1, SLOT), lambda i: (0, 0))],
        out_specs=(pl.BlockSpec((tm, SLOT), lambda i: (i, 0)), pl.BlockSpec((tm, SLOT), lambda i: (i, 0))),
        compiler_params=_cparams(("parallel",)),
        name="rope_tables",
    )(pos_col, invf_slot)


def _inproj_kernel(x_ref, mod_ref, g_ref, w_ref, o_ref, h_scr):
    @pl.when(pl.program_id(1) == 0)
    def _():
        m = mod_ref[0]
        h = _rms(x_ref[...], g_ref[...]) * (1.0 + m[1:2]) + m[0:1]
        h_scr[...] = h.astype(BF16)

    o_ref[...] = _dot(h_scr[...], w_ref[...]).astype(o_ref.dtype)


def _in_projection(x2d, mod_l, g_norm, w_in_l, seq, out_dtype, tm, tn, name):
    t, d = x2d.shape
    nc = w_in_l.shape[1]
    tm = _tile(seq, tm)
    assert nc % tn == 0
    per_b = seq // tm
    return pl.pallas_call(
        _inproj_kernel,
        out_shape=jax.ShapeDtypeStruct((t, nc), out_dtype),
        grid=(t // tm, nc // tn),
        in_specs=[
            pl.BlockSpec((tm, d), lambda i, j: (i, 0)),
            pl.BlockSpec((1, 6, d), lambda i, j: (i // per_b, 0, 0)),
            pl.BlockSpec((1, d), lambda i, j: (0, 0)),
            pl.BlockSpec((d, tn), lambda i, j: (0, j)),
        ],
        out_specs=pl.BlockSpec((tm, tn), lambda i, j: (i, j)),
        scratch_shapes=[pltpu.VMEM((tm, d), BF16)],
        compiler_params=_cparams(("parallel", "arbitrary"), VMEM_LIMIT),
        name=name,
    )(x2d, mod_l, g_norm, w_in_l)


def _mla_prep_kernel(cq_ref, ckv_ref, kr_ref, cos_ref, sin_ref, gcq_ref, gckv_ref, wuq_ref, wukv_ref,
                     gq_ref, gkn_ref, gkr_ref, mseg_ref, q_out, k_out, v_out):
    cos = cos_ref[...]
    sin = sin_ref[...]
    lane = lax.broadcasted_iota(I32, (1, SLOT), 1)
    s_neg = jnp.where((lane >= 64) & (lane < 80), -sin, 0.0)
    s_pos = jnp.where((lane >= 80) & (lane < 96), sin, 0.0)
    mseg = mseg_ref[...]

    def rope(xn):
        return xn * cos + pltpu.roll(xn, SLOT - 16, 1) * s_neg + pltpu.roll(xn, 16, 1) * s_pos

    cqn = _rms(cq_ref[...].astype(F32), gcq_ref[...]).astype(BF16)
    q = _dot(cqn, wuq_ref[...])
    for h in range(MLA_HEADS):
        sl = slice(h * SLOT, (h + 1) * SLOT)
        q_out[:, sl] = rope(_seg_norm(q[:, sl], mseg, gq_ref[...])).astype(q_out.dtype)

    ckvn = _rms(ckv_ref[...].astype(F32), gckv_ref[...]).astype(BF16)
    kv = _dot(ckvn, wukv_ref[...])
    kr = rope(_seg_norm(kr_ref[...].astype(F32), mseg, gkr_ref[...]))
    for h in range(MLA_HEADS):
        sl = slice(h * SLOT, (h + 1) * SLOT)
        k_out[:, sl] = (_seg_norm(kv[:, sl], mseg, gkn_ref[...]) + kr).astype(k_out.dtype)
    v_out[...] = kv[:, MLA_HEADS * SLOT:].astype(v_out.dtype)


def _mla_prep(cols, cos_t, sin_t, gcq, gckv, wuq, wukv, gq_slot, gkn_slot, gkr_slot, mseg):
    t = cols.shape[0]
    tm = _tile(t, 512)
    hs = MLA_HEADS * SLOT
    full = lambda shape: pl.BlockSpec(shape, lambda i: (0,) * len(shape))
    return pl.pallas_call(
        _mla_prep_kernel,
        out_shape=(jax.ShapeDtypeStruct((t, hs), BF16), jax.ShapeDtypeStruct((t, hs), BF16),
                   jax.ShapeDtypeStruct((t, MLA_HEADS * MLA_V), BF16)),
        grid=(t // tm,),
        in_specs=[
            pl.BlockSpec((tm, MLA_Q_RANK), lambda i: (i, OFF_CQ // MLA_Q_RANK)),
            pl.BlockSpec((tm, MLA_KV_RANK), lambda i: (i, OFF_CKV // MLA_KV_RANK)),
            pl.BlockSpec((tm, SLOT), lambda i: (i, OFF_KR // SLOT)),
            pl.BlockSpec((tm, SLOT), lambda i: (i, 0)),
            pl.BlockSpec((tm, SLOT), lambda i: (i, 0)),
            full((1, MLA_Q_RANK)), full((1, MLA_KV_RANK)),
            full((MLA_Q_RANK, hs)), full((MLA_KV_RANK, hs + MLA_HEADS * MLA_V)),
            full((1, SLOT)), full((1, SLOT)), full((1, SLOT)), full((SLOT, SLOT)),
        ],
        out_specs=(pl.BlockSpec((tm, hs), lambda i: (i, 0)), pl.BlockSpec((tm, hs), lambda i: (i, 0)),
                   pl.BlockSpec((tm, MLA_HEADS * MLA_V), lambda i: (i, 0))),
        compiler_params=_cparams(("parallel",), VMEM_LIMIT),
        name="mla_prep",
    )(cols, cols, cols, cos_t, sin_t, gcq, gckv, wuq, wukv, gq_slot, gkn_slot, gkr_slot, mseg)


def _mla_attn_kernel(q_ref, k_ref, v_ref, o_ref):
    lane = lax.broadcasted_iota(I32, (1, 2 * MLA_V), 1)
    for pair in range(MLA_HPS // 2):
        outs = []
        vp = v_ref[0, :, pair * 2 * MLA_V:(pair + 1) * 2 * MLA_V]
        for a in range(2):
            sl = slice((2 * pair + a) * SLOT, (2 * pair + a + 1) * SLOT)
            s = _dot_nt(q_ref[0, :, sl], k_ref[0, :, sl])
            m = jnp.max(s, axis=-1, keepdims=True)
            p = jnp.exp2(s - m)
            l = jnp.sum(p, axis=-1, keepdims=True)
            outs.append(_dot(p.astype(BF16), vp) / l)
        o_ref[0, :, pair * 2 * MLA_V:(pair + 1) * 2 * MLA_V] = jnp.where(lane < MLA_V, outs[0], outs[1]).astype(o_ref.dtype)


def _mla_attention(qm, km, vm, batch, seq):
    hs = MLA_HEADS * SLOT
    q3 = qm.reshape(batch, seq, hs)
    k3 = km.reshape(batch, seq, hs)
    v3 = vm.reshape(batch, seq, MLA_HEADS * MLA_V)
    tq = _tile(seq, MLA_TQ)
    out = pl.pallas_call(
        _mla_attn_kernel,
        out_shape=jax.ShapeDtypeStruct((batch, seq, MLA_HEADS * MLA_V), BF16),
        grid=(batch, MLA_HEADS // MLA_HPS, seq // tq),
        in_specs=[
            pl.BlockSpec((1, tq, MLA_HPS * SLOT), lambda b, p, i: (b, i, p)),
            pl.BlockSpec((1, seq, MLA_HPS * SLOT), lambda b, p, i: (b, 0, p)),
            pl.BlockSpec((1, seq, MLA_HPS * MLA_V), lambda b, p, i: (b, 0, p)),
        ],
        out_specs=pl.BlockSpec((1, tq, MLA_HPS * MLA_V), lambda b, p, i: (b, i, p)),
        compiler_params=_cparams(("parallel", "parallel", "arbitrary"), VMEM_LIMIT),
        name="mla_attention",
    )(q3, k3, v3)
    return out.reshape(batch * seq, MLA_HEADS * MLA_V)


WIN_BLOCKS_PER_STEP = 8
BAND_UNROLL = 16


def _window_start(i, bq, radius, n, kw):
    ws = jnp.clip(i * bq - radius, 0, n - kw)
    return pl.multiple_of(ws, 16)


MASKED_DIST = 1e30


def _masked_dist_t(i, bq, ws, kw, radius, pos_keys, pos_queries):
    keys = lax.broadcasted_iota(I32, (kw, bq), 0)
    queries = lax.broadcasted_iota(I32, (kw, bq), 1)
    rel = queries - keys + (i * bq - ws)
    d = pos_keys - pos_queries
    return jnp.where(jnp.maximum(rel, -rel) <= radius, jnp.maximum(d, -d).astype(F32), MASKED_DIST)


def _sub_positions(pos, batch, n, dil, bq):
    pos_sub = jnp.transpose(pos.reshape(batch, n, dil), (0, 2, 1))
    return pos_sub.reshape(batch, dil * n, 1), pos_sub.reshape(batch, dil * (n // bq), 1, bq)


def _dil_kernel(q0_ref, q1_ref, k0_ref, k1_ref, v0_ref, v1_ref, pcol_ref, prow_ref, gq_ref, gk_ref, mseg_ref,
                o_ref, lse_ref, kn_scr, vn_scr, *, n, bq, kw, dil, slopes):
    q_refs, k_refs, v_refs = (q0_ref, q1_ref), (k0_ref, k1_ref), (v0_ref, v1_ref)
    nblk = n // bq
    mseg = mseg_ref[...]
    gq = gq_ref[...]
    low_lane = lax.broadcasted_iota(I32, (1, 2 * HEAD_DIM), 1) < HEAD_DIM

    def rows(start, size):
        return pl.ds(start, size) if dil == 1 else pl.ds(start, size, stride=dil)

    def fill(r, carry):
        base = pl.multiple_of(r * n, 16)
        for hf in range(2):
            kn_scr[hf, pl.ds(base, n), :] = _seg_norm(k_refs[hf][0, rows(r, n), :], mseg, gk_ref[...]).astype(BF16)
            vn_scr[hf, pl.ds(base, n), :] = v_refs[hf][0, rows(r, n), :].astype(BF16)
        return carry

    lax.fori_loop(0, dil, fill, 0)

    def query_block(b, carry):
        r = b >> (nblk.bit_length() - 1)
        i = b & (nblk - 1)
        ws = _window_start(i, bq, DIL_RADIUS, n, kw)
        start = pl.multiple_of(r * n + ws, 16)
        q_rows = rows(r + i * (bq * dil), bq)
        dist = _masked_dist_t(i, bq, ws, kw, DIL_RADIUS, pcol_ref[0, pl.ds(start, kw), :], prow_ref[0, r * nblk + i])
        s_halves = []
        for hf in range(2):
            qn = _seg_norm(q_refs[hf][0, q_rows, :], mseg, gq)
            qs = jnp.concatenate([jnp.where(low_lane, qn, 0.0), jnp.where(low_lane, 0.0, qn)], axis=0).astype(BF16)
            bias = jnp.concatenate([slopes[2 * hf] * dist, slopes[2 * hf + 1] * dist], axis=1)
            s_halves.append(_dot_nt(kn_scr[hf, pl.ds(start, kw), :], qs) - bias)
        s = jnp.concatenate(s_halves, axis=1)
        m = jnp.max(s, axis=0, keepdims=True)
        p = jnp.exp2(s - m)
        l = jnp.sum(p, axis=0, keepdims=True)
        lse = (m + jnp.log2(l)) * LN2
        pb = p.astype(BF16)
        for hf in range(2):
            c0, c1, c2 = 2 * hf * bq, (2 * hf + 1) * bq, (2 * hf + 2) * bq
            ot = _dot_tn(vn_scr[hf, pl.ds(start, kw), :], pb[:, c0:c2]) / l[:, c0:c2]
            o_ref[0, hf, q_rows, :] = jnp.concatenate([ot[:HEAD_DIM, :bq], ot[HEAD_DIM:, bq:]], axis=0).T
            lse_ref[0, hf, q_rows, :] = jnp.concatenate([jnp.broadcast_to(lse[:, c0:c1], (HEAD_DIM, bq)),
                                                         jnp.broadcast_to(lse[:, c1:c2], (HEAD_DIM, bq))], axis=0).T
        return carry

    lax.fori_loop(0, dil * nblk, query_block, 0, unroll=min(dil * nblk, BAND_UNROLL))


def _dilated_group(dil_cols, pos, gq_row, gk_row, mseg, gi, batch, seq):
    _, dil = DIL_PATTERNS[gi]
    n = seq // dil
    bq = min(128, n)
    kw = min(bq + 2 * DIL_RADIUS, n)
    nblk = n // bq
    hw = 2 * HEAD_DIM
    cols3 = dil_cols.reshape(batch, seq, N_DIL_COLS)
    pcol, prow = _sub_positions(pos, batch, n, dil, bq)
    slopes = tuple(float(LOG2E * 2.0 ** (-8.0 * (gi * DIL_HPG + hh + 1) / DIL_HEADS)) for hh in range(DIL_HPG))
    kern = functools.partial(_dil_kernel, n=n, bq=bq, kw=kw, dil=dil, slopes=slopes)
    col = lambda which, hf: pl.BlockSpec((1, seq, hw), lambda b: (b, 0, (which * DIL_GROUPS + gi) * 2 + hf))
    small = lambda shape: pl.BlockSpec(shape, lambda b: (0,) * len(shape))
    halves = jax.ShapeDtypeStruct((batch, 2, seq, hw), F32)
    return pl.pallas_call(
        kern,
        out_shape=(halves, halves),
        grid=(batch,),
        in_specs=[
            col(0, 0), col(0, 1), col(1, 0), col(1, 1), col(2, 0), col(2, 1),
            pl.BlockSpec((1, seq, 1), lambda b: (b, 0, 0)),
            pl.BlockSpec((1, dil * nblk, 1, bq), lambda b: (b, 0, 0, 0)),
            small((1, hw)), small((1, hw)), small((hw, hw)),
        ],
        out_specs=(pl.BlockSpec((1, 2, seq, hw), lambda b: (b, 0, 0, 0)),
                   pl.BlockSpec((1, 2, seq, hw), lambda b: (b, 0, 0, 0))),
        scratch_shapes=[pltpu.VMEM((2, seq, hw), BF16), pltpu.VMEM((2, seq, hw), BF16)],
        compiler_params=_cparams(("parallel",), VMEM_LIMIT),
        name=f"dilated_group{gi}",
    )(cols3, cols3, cols3, cols3, cols3, cols3, pcol, prow, gq_row, gk_row, mseg)


def _win_kernel(q_ref, k_ref, v_ref, pcol_ref, prow_ref, gq_ref, gk_ref, sink_ref, mseg_ref, o_ref, kn_scr,
                *, n, bq, kw, slopes):
    step = pl.program_id(1)
    mseg = mseg_ref[...]

    @pl.when(step == 0)
    def _():
        kn_scr[...] = _seg_norm(k_ref[0].astype(F32), mseg, gk_ref[...]).astype(BF16)

    low_lane = lax.broadcasted_iota(I32, (1, 2 * HEAD_DIM), 1) < HEAD_DIM
    rep = WIN_HEADS // WIN_KV_HEADS
    sink = sink_ref[...]
    per_step = q_ref.shape[1] // bq
    for j in range(per_step):
        i = step * per_step + j
        rows = slice(j * bq, (j + 1) * bq)
        ws = _window_start(i, bq, WIN_RADIUS, n, kw)
        kwin = kn_scr[pl.ds(ws, kw), :]
        vwin = v_ref[0, pl.ds(ws, kw), :]
        dist = _masked_dist_t(i, bq, ws, kw, WIN_RADIUS, pcol_ref[0, pl.ds(ws, kw), :], prow_ref[0, j])

        qn_pairs = [_seg_norm(q_ref[0, rows, pair * 128:(pair + 1) * 128].astype(F32), mseg, gq_ref[...])
                    for pair in range(WIN_HEADS // 2)]
        q_parts = []
        for h in range(WIN_HEADS):
            pair, upper = divmod(h, 2)
            qm = jnp.where(low_lane, 0.0, qn_pairs[pair]) if upper else jnp.where(low_lane, qn_pairs[pair], 0.0)
            if upper != h // rep:
                qm = pltpu.roll(qm, HEAD_DIM, 1)
            q_parts.append(qm.astype(BF16))
        qs = jnp.concatenate(q_parts, axis=0)
        bias = jnp.concatenate([slopes[h] * dist for h in range(WIN_HEADS)], axis=1)
        s = _dot_nt(kwin, qs) - bias
        m = jnp.maximum(jnp.max(s, axis=0, keepdims=True), sink)
        e = jnp.exp2(s - m)
        den = jnp.sum(e, axis=0, keepdims=True) + jnp.exp2(sink - m)
        ot = _dot_tn(vwin, e.astype(BF16)) / den
        for pair in range(WIN_HEADS // 2):
            slabs = []
            for h in (2 * pair, 2 * pair + 1):
                g = h // rep
                slabs.append(ot[g * HEAD_DIM:(g + 1) * HEAD_DIM, h * bq:(h + 1) * bq])
            o_ref[0, rows, pair * 128:(pair + 1) * 128] = jnp.concatenate(slabs, axis=0).T.astype(o_ref.dtype)


def _window_mixer(cols, pos, gq_row, gk_row, sink_row, mseg128, batch, seq):
    n = seq
    bq = min(128, n)
    kw = min(bq + 2 * WIN_RADIUS, n)
    nblk = n // bq
    qw = WIN_HEADS * HEAD_DIM
    kvw = WIN_KV_HEADS * HEAD_DIM
    cols_v = cols.reshape(batch, seq, N_COLS)
    pcol = pos.reshape(batch, seq, 1)
    prow = pos.reshape(batch, nblk, 1, bq)
    slopes = tuple(float(LOG2E * 2.0 ** (-8.0 * (h + 1) / WIN_HEADS)) for h in range(WIN_HEADS))
    kern = functools.partial(_win_kernel, n=n, bq=bq, kw=kw, slopes=slopes)
    small = lambda shape: pl.BlockSpec(shape, lambda b, i: (0,) * len(shape))
    per_step = min(WIN_BLOCKS_PER_STEP, nblk)
    o = pl.pallas_call(
        kern,
        out_shape=jax.ShapeDtypeStruct((batch, seq, qw), BF16),
        grid=(batch, nblk // per_step),
        in_specs=[
            pl.BlockSpec((1, per_step * bq, qw), lambda b, i: (b, i, OFF_WQ // qw)),
            pl.BlockSpec((1, n, kvw), lambda b, i: (b, 0, OFF_WK // kvw)),
            pl.BlockSpec((1, n, kvw), lambda b, i: (b, 0, OFF_WV // kvw)),
            pl.BlockSpec((1, n, 1), lambda b, i: (b, 0, 0)),
            pl.BlockSpec((1, per_step, 1, bq), lambda b, i: (b, i, 0, 0)),
            small((1, kvw)), small((1, kvw)), small((1, WIN_HEADS * bq)), small((kvw, kvw)),
        ],
        out_specs=pl.BlockSpec((1, per_step * bq, qw), lambda b, i: (b, i, 0)),
        scratch_shapes=[pltpu.VMEM((n, kvw), BF16)],
        compiler_params=_cparams(("parallel", "arbitrary"), VMEM_LIMIT),
        name="window_mixer",
    )(cols_v, cols_v, cols_v, pcol, prow, gq_row, gk_row, sink_row, mseg128)
    return o.reshape(batch * seq, qw)


def _merge_kernel(gm_ref, gd_ref, gw_ref, om_ref, od0_ref, od1_ref, od2_ref, l0_ref, l1_ref, l2_ref, ow_ref,
                  x_ref, mod_ref, wm_ref, wd_ref, ww_ref, wo_ref, o_ref):
    def heads(ref):
        return jnp.concatenate([ref[0, 0], ref[0, 1]], axis=1)

    l0, l1, l2 = heads(l0_ref), heads(l1_ref), heads(l2_ref)
    m = jnp.maximum(jnp.maximum(l0, l1), l2)
    e0, e1, e2 = jnp.exp(l0 - m), jnp.exp(l1 - m), jnp.exp(l2 - m)
    od = (e0 * heads(od0_ref) + e1 * heads(od1_ref) + e2 * heads(od2_ref)) / (e0 + e1 + e2)
    y = jax.nn.sigmoid(gm_ref[...].astype(F32)) * _dot(om_ref[...], wm_ref[...])
    y += jax.nn.sigmoid(gd_ref[...].astype(F32)) * _dot(od.astype(BF16), wd_ref[...])
    y += jax.nn.sigmoid(gw_ref[...].astype(F32)) * _dot(ow_ref[...], ww_ref[...])
    z = _dot(y.astype(BF16), wo_ref[...])
    o_ref[...] = x_ref[...] + mod_ref[0][2:3] * z


def _merge(cols, o_mla, o_dil, lse_dil, o_win, x2d, mod_l, wm, wd, ww, wo, seq):
    t, d = x2d.shape
    tm = _tile(seq, 512)
    per_b = seq // tm
    row = lambda w, j=0: pl.BlockSpec((tm, w), lambda i: (i, j))
    full = lambda a: pl.BlockSpec(a.shape, lambda i: (0, 0))
    dil = pl.BlockSpec((1, 2, tm, 2 * HEAD_DIM), lambda i: (i // per_b, 0, i % per_b, 0))
    return pl.pallas_call(
        _merge_kernel,
        out_shape=jax.ShapeDtypeStruct((t, d), F32),
        grid=(t // tm,),
        in_specs=[row(d, 0), row(d, 1), row(d, 2), row(o_mla.shape[1]),
                  dil, dil, dil, dil, dil, dil, row(o_win.shape[1]),
                  row(d), pl.BlockSpec((1, 6, d), lambda i: (i // per_b, 0, 0)),
                  full(wm), full(wd), full(ww), full(wo)],
        out_specs=row(d),
        compiler_params=_cparams(("parallel",), VMEM_LIMIT),
        name="merge_out_proj",
    )(cols, cols, cols, o_mla, o_dil[0], o_dil[1], o_dil[2], lse_dil[0], lse_dil[1], lse_dil[2], o_win,
      x2d, mod_l, wm, wd, ww, wo)


def _router_kernel(x_ref, mod_ref, g_ref, whi_ref, wlo_ref, br_ref, upper_ref, hs_ref, pos_ref, gate_ref, tab_ref,
                   *, lr):
    m = mod_ref[0]
    h = _rms(x_ref[...], g_ref[...]) * (1.0 + m[4:5]) + m[3:4]
    tm = h.shape[0]
    h_hi = h.astype(BF16)
    h_lo = (h - h_hi.astype(F32)).astype(BF16)
    logits = _dot(h_hi, whi_ref[...]) + (_dot(h_hi, wlo_ref[...]) + _dot(h_lo, whi_ref[...])) + br_ref[...]
    lane = lax.broadcasted_iota(I32, (tm, LANES), 1)
    lane_f = lane.astype(F32)

    def first_argmax(vals, mx):
        return jnp.min(jnp.where(vals == mx, lane_f, float(LANES)), axis=-1, keepdims=True).astype(I32)

    lg = jnp.where(lane < N_EXPERT_GROUPS, logits, NEG_INF)
    mg = jnp.max(lg, axis=-1, keepdims=True)
    g_w = 1.0 / jnp.sum(jnp.exp(lg - mg), axis=-1, keepdims=True)
    g_idx = first_argmax(lg, mg)
    eid = lane - N_EXPERT_GROUPS
    in_grp = (eid >= 0) & (eid < N_EXPERTS) & ((eid >> 3) == g_idx)
    le = jnp.where(in_grp, logits, NEG_INF)
    m1 = jnp.max(le, axis=-1, keepdims=True)
    i1 = first_argmax(le, m1)
    le2 = jnp.where(lane == i1, NEG_INF, le)
    m2 = jnp.max(le2, axis=-1, keepdims=True)
    i2 = first_argmax(le2, m2)
    r = jnp.exp(m2 - m1)
    gate1 = g_w / (1.0 + r)
    gate2 = g_w * r / (1.0 + r)
    e1 = i1 - N_EXPERT_GROUPS
    e2 = i2 - N_EXPERT_GROUPS
    hit1 = lane == e1
    hit2 = lane == e2
    onehot = jnp.where(hit1 | hit2, 1.0, 0.0)
    rows = lax.broadcasted_iota(I32, (tm, tm), 0)
    cols = lax.broadcasted_iota(I32, (tm, tm), 1)
    before = jnp.where(rows > cols, 1.0, 0.0).astype(BF16)
    rank = _dot(before, onehot.astype(BF16))
    run_len = (jnp.sum(onehot, axis=0, keepdims=True).astype(I32) + (MOE_CHUNK - 1)) & jnp.int32(-MOE_CHUNK)
    run_off = _dot(jnp.broadcast_to(run_len.astype(F32), (8, LANES)).astype(BF16), upper_ref[...])[0:1]
    pos1 = jnp.sum(jnp.where(hit1, rank + run_off, 0.0), axis=-1, keepdims=True).astype(I32)
    pos2 = jnp.sum(jnp.where(hit2, rank + run_off, 0.0), axis=-1, keepdims=True).astype(I32)
    pos_lanes = jnp.where(lane == 0, pos1, jnp.where(lane == 1, pos2, 0))
    pos_rows = pos_lanes.astype(F32).T.astype(I32)
    local = lax.broadcasted_iota(I32, (lr, tm), 0)
    place = jnp.where((local == pos_rows[0:1]) | (local == pos_rows[1:2]), 1.0, 0.0).astype(BF16)
    hs_ref[0] = _dot(place, h_hi).astype(BF16)
    pos_ref[...] = pos_lanes
    gate_ref[...] = jnp.where(lane == 0, gate1, jnp.where(lane == 1, gate2, 0.0))
    sub = lax.broadcasted_iota(I32, (8, LANES), 0)
    tab_ref[0] = jnp.where(sub == 0, run_len, jnp.where(sub == 1, run_off.astype(I32), 0))


def _router(x2d, mod_l, g_norm, w_router, b_router, seq, tm, lr):
    t, d = x2d.shape
    per_b = seq // tm
    nt = t // tm
    row = lambda w: pl.BlockSpec((tm, w), lambda i: (i, 0))
    idx = jnp.arange(LANES)
    upper = jnp.where(idx[:, None] < idx[None, :], 1.0, 0.0).astype(BF16)
    w_hi = w_router.astype(BF16)
    w_lo = (w_router - w_hi.astype(F32)).astype(BF16)
    return pl.pallas_call(
        functools.partial(_router_kernel, lr=lr),
        out_shape=(jax.ShapeDtypeStruct((nt, lr, d), BF16), jax.ShapeDtypeStruct((t, LANES), I32),
                   jax.ShapeDtypeStruct((t, LANES), F32), jax.ShapeDtypeStruct((nt, 8, LANES), I32)),
        grid=(nt,),
        in_specs=[row(d), pl.BlockSpec((1, 6, d), lambda i: (i // per_b, 0, 0)),
                  pl.BlockSpec((1, d), lambda i: (0, 0)), pl.BlockSpec((d, LANES), lambda i: (0, 0)),
                  pl.BlockSpec((d, LANES), lambda i: (0, 0)),
                  pl.BlockSpec((1, LANES), lambda i: (0, 0)), pl.BlockSpec((LANES, LANES), lambda i: (0, 0))],
        out_specs=(pl.BlockSpec((1, lr, d), lambda i: (i, 0, 0)), row(LANES), row(LANES),
                   pl.BlockSpec((1, 8, LANES), lambda i: (i, 0, 0))),
        compiler_params=_cparams(("parallel",), VMEM_LIMIT),
        name="moe_router",
    )(x2d, mod_l, g_norm, w_hi, w_lo, b_router, upper)


def _for_each_chunk(off_ref, n_ref, base_ref, tile, fn):
    def per_expert(e, carry):
        k = tile * N_EXPERTS + e
        off, base = off_ref[k], base_ref[k]

        def per_chunk(c, carry2):
            fn(pl.multiple_of(off + c * MOE_CHUNK, MOE_CHUNK), pl.multiple_of(base + c * MOE_CHUNK, MOE_CHUNK))
            return carry2

        return lax.fori_loop(0, n_ref[k], per_chunk, carry)

    lax.fori_loop(0, N_EXPERTS, per_expert, 0)


def _scatter_kernel(off_ref, n_ref, base_ref, hs_ref, rows_in_ref, rows_ref, sem):
    del rows_in_ref
    tile = pl.program_id(0)

    def chunk_copy(local, glob):
        return pltpu.make_async_copy(hs_ref.at[0, pl.ds(local, MOE_CHUNK)], rows_ref.at[pl.ds(glob, MOE_CHUNK)], sem)

    _for_each_chunk(off_ref, n_ref, base_ref, tile, lambda a, b: chunk_copy(a, b).start())
    _for_each_chunk(off_ref, n_ref, base_ref, tile, lambda a, b: chunk_copy(a, b).wait())


def _scatter_rows(run_off, run_chunks, base, hs, n_rows):
    nt, lr, w = hs.shape
    zeros = jnp.zeros((n_rows, w), hs.dtype)
    grid_spec = pltpu.PrefetchScalarGridSpec(
        num_scalar_prefetch=3,
        grid=(nt,),
        in_specs=[pl.BlockSpec((1, lr, w), lambda i, *_: (i, 0, 0)), pl.BlockSpec(memory_space=pl.ANY)],
        out_specs=pl.BlockSpec(memory_space=pl.ANY),
        scratch_shapes=[pltpu.SemaphoreType.DMA],
    )
    return pl.pallas_call(
        _scatter_kernel,
        out_shape=jax.ShapeDtypeStruct((n_rows, w), hs.dtype),
        grid_spec=grid_spec,
        input_output_aliases={4: 0},
        compiler_params=_cparams(("arbitrary",)),
        name="moe_scatter",
    )(run_off, run_chunks, base, hs, zeros)


def _ffn_kernel(be_ref, nu_ref, rows_ref, w1_ref, w3_ref, w2_ref, y_ref, w13_scr, w2_scr):
    j = pl.program_id(0)
    used = j < nu_ref[0]
    new_expert = (j == 0) | (be_ref[j] != be_ref[jnp.maximum(j - 1, 0)])

    @pl.when(used & new_expert)
    def _():
        w13_scr[:, :D_EXPERT] = w1_ref[0].astype(BF16)
        w13_scr[:, D_EXPERT:] = w3_ref[0].astype(BF16)
        w2_scr[...] = w2_ref[0].astype(BF16)

    @pl.when(used)
    def _():
        h = _dot(rows_ref[...], w13_scr[...])
        a = h[:, :D_EXPERT]
        act = a * jax.nn.sigmoid(a) * h[:, D_EXPERT:]
        y_ref[...] = _dot(act.astype(BF16), w2_scr[...]).astype(y_ref.dtype)

    @pl.when(jnp.logical_not(used))
    def _():
        y_ref[...] = jnp.zeros_like(y_ref)


def _grouped_ffn(block_expert, n_used, rows, w1, w3, w2):
    n_rows, w = rows.shape
    d = w1.shape[1]
    nb = n_rows // MOE_BM
    grid_spec = pltpu.PrefetchScalarGridSpec(
        num_scalar_prefetch=2,
        grid=(nb,),
        in_specs=[
            pl.BlockSpec((MOE_BM, w), lambda j, be, nu: (j, 0)),
            pl.BlockSpec((1, d, D_EXPERT), lambda j, be, nu: (be[j], 0, 0)),
            pl.BlockSpec((1, d, D_EXPERT), lambda j, be, nu: (be[j], 0, 0)),
            pl.BlockSpec((1, D_EXPERT, d), lambda j, be, nu: (be[j], 0, 0)),
        ],
        out_specs=pl.BlockSpec((MOE_BM, w), lambda j, be, nu: (j, 0)),
        scratch_shapes=[pltpu.VMEM((d, 2 * D_EXPERT), BF16), pltpu.VMEM((D_EXPERT, d), BF16)],
    )
    return pl.pallas_call(
        _ffn_kernel,
        out_shape=jax.ShapeDtypeStruct((n_rows, w), BF16),
        grid_spec=grid_spec,
        compiler_params=_cparams(("arbitrary",), VMEM_LIMIT),
        name="moe_grouped_ffn",
    )(block_expert, n_used, rows, w1, w3, w2)


def _combine_kernel(off_ref, n_ref, base_ref, y_ref, pos_ref, gate_ref, x_ref, mod_ref, o_ref, ybuf, sem):
    tile = pl.program_id(0)
    slot = tile % 2

    def chunk_copy(s, local, glob):
        return pltpu.make_async_copy(y_ref.at[pl.ds(glob, MOE_CHUNK)], ybuf.at[s, pl.ds(local, MOE_CHUNK)], sem.at[s])

    def request(t, s):
        ybuf[s] = jnp.zeros(ybuf.shape[1:], ybuf.dtype)
        _for_each_chunk(off_ref, n_ref, base_ref, t, lambda a, b: chunk_copy(s, a, b).start())

    @pl.when(tile == 0)
    def _():
        request(tile, slot)

    @pl.when(tile + 1 < pl.num_programs(0))
    def _():
        request(tile + 1, 1 - slot)

    _for_each_chunk(off_ref, n_ref, base_ref, tile, lambda a, b: chunk_copy(slot, a, b).wait())
    tm = x_ref.shape[0]
    pos, g = pos_ref[...], gate_ref[...]
    local = lax.broadcasted_iota(I32, (tm, ybuf.shape[1]), 1)
    pick = (jnp.where(local == pos[:, 0:1], g[:, 0:1], 0.0)
            + jnp.where(local == pos[:, 1:2], g[:, 1:2], 0.0)).astype(BF16)
    moe = _dot(pick, ybuf[slot])
    o_ref[...] = x_ref[...] + mod_ref[0][5:6] * moe


def _combine(run_off, run_chunks, base, y, pos, gates, x2d, mod_l, seq, tm, lr):
    t, d = x2d.shape
    per_b = seq // tm
    grid_spec = pltpu.PrefetchScalarGridSpec(
        num_scalar_prefetch=3,
        grid=(t // tm,),
        in_specs=[pl.BlockSpec(memory_space=pl.ANY),
                  pl.BlockSpec((tm, LANES), lambda i, *_: (i, 0)), pl.BlockSpec((tm, LANES), lambda i, *_: (i, 0)),
                  pl.BlockSpec((tm, d), lambda i, *_: (i, 0)),
                  pl.BlockSpec((1, 6, d), lambda i, *_: (i // per_b, 0, 0))],
        out_specs=pl.BlockSpec((tm, d), lambda i, *_: (i, 0)),
        scratch_shapes=[pltpu.VMEM((2, lr, y.shape[1]), y.dtype), pltpu.SemaphoreType.DMA((2,))],
    )
    return pl.pallas_call(
        _combine_kernel,
        out_shape=jax.ShapeDtypeStruct((t, d), F32),
        grid_spec=grid_spec,
        compiler_params=_cparams(("arbitrary",), VMEM_LIMIT),
        name="moe_combine",
    )(run_off, run_chunks, base, y, pos, gates, x2d, mod_l)


def _moe(x2d, mod_l, g_norm2, w_gr, b_gr, w_er, b_er, w1, w3, w2, seq):
    t, d = x2d.shape
    pad = LANES - N_EXPERT_GROUPS - N_EXPERTS
    w_router = jnp.concatenate([w_gr, w_er, jnp.zeros((d, pad), F32)], axis=1)
    b_router = jnp.concatenate([b_gr, b_er, jnp.zeros((pad,), F32)]).reshape(1, LANES)
    tm = _tile(seq, 512)
    nt = t // tm
    lr = 2 * tm + N_EXPERTS * MOE_CHUNK
    hs, pos, gates, tab = _router(x2d, mod_l, g_norm2, w_router, b_router, seq, tm, lr)

    run_len = tab[:, 0, :N_EXPERTS]
    run_off = tab[:, 1, :N_EXPERTS]
    total = jnp.sum(run_len, axis=0)
    padded = (total + MOE_BM - 1) // MOE_BM * MOE_BM
    ends = jnp.cumsum(padded)
    base = (ends - padded)[None, :] + jnp.cumsum(run_len, axis=0) - run_len
    nb = (2 * t + nt * N_EXPERTS * MOE_CHUNK) // MOE_BM + N_EXPERTS
    block_start = jnp.arange(nb, dtype=I32) * MOE_BM
    block_expert = jnp.minimum(jnp.sum(block_start[:, None] >= ends[None, :], axis=1), N_EXPERTS - 1).astype(I32)
    n_used = (ends[-1] // MOE_BM).astype(I32).reshape(1)
    flat = lambda a: a.reshape(-1).astype(I32)
    run_off, run_chunks, base = flat(run_off), flat(run_len // MOE_CHUNK), flat(base)

    rows = _scatter_rows(run_off, run_chunks, base, hs, nb * MOE_BM)
    y = _grouped_ffn(block_expert, n_used, rows, w1, w3, w2)
    return _combine(run_off, run_chunks, base, y, pos, gates, x2d, mod_l, seq, tm, lr)


def _seg_matrix(width, segments):
    idx = jnp.arange(width)
    m = jnp.zeros((width, width), F32)
    for start, length in segments:
        inside = (idx >= start) & (idx < start + length)
        m = m + jnp.where(inside[:, None] & inside[None, :], 1.0 / length, 0.0)
    return m.astype(BF16)


def _layout_kernel(w_ref, main_ref, dil_ref):
    sizes = [MLA_Q_RANK, MLA_KV_RANK, MLA_ROPE, N_DIL_COLS,
             (WIN_HEADS + 2 * WIN_KV_HEADS) * HEAD_DIM, 3 * D_MODEL]
    starts = [sum(sizes[:k]) for k in range(len(sizes))]
    seg = lambda k, a=0, b=None: w_ref[0, :, starts[k] + a:starts[k] + (sizes[k] if b is None else b)].astype(BF16)
    qw = WIN_HEADS * HEAD_DIM
    rows = w_ref.shape[1]
    main_ref[0, :, OFF_GATE:OFF_GATE + sizes[5]] = seg(5)
    main_ref[0, :, OFF_CQ:OFF_CQ + sizes[0]] = seg(0)
    main_ref[0, :, OFF_WQ:OFF_WQ + qw] = seg(4, 0, qw)
    main_ref[0, :, OFF_CKV:OFF_CKV + sizes[1]] = seg(1)
    main_ref[0, :, OFF_WK:OFF_WK + sizes[4] - qw] = seg(4, qw)
    tail = jnp.concatenate([jnp.zeros((rows, MLA_NOPE), BF16), seg(2),
                            jnp.zeros((rows, N_COLS - OFF_KR - MLA_NOPE - MLA_ROPE), BF16)], axis=1)
    main_ref[0, :, OFF_KR:] = tail
    dil_ref[0] = seg(3)


def _layout_w_in(w_in):
    depth, d, n = w_in.shape
    tr = _tile(d, 256)
    return pl.pallas_call(
        _layout_kernel,
        out_shape=(jax.ShapeDtypeStruct((depth, d, N_COLS), BF16), jax.ShapeDtypeStruct((depth, d, N_DIL_COLS), BF16)),
        grid=(depth, d // tr),
        in_specs=[pl.BlockSpec((1, tr, n), lambda l, i: (l, i, 0))],
        out_specs=(pl.BlockSpec((1, tr, N_COLS), lambda l, i: (l, i, 0)),
                   pl.BlockSpec((1, tr, N_DIL_COLS), lambda l, i: (l, i, 0))),
        compiler_params=_cparams(("parallel", "parallel"), VMEM_LIMIT),
        name="w_in_layout",
    )(w_in)


def _layout_mla(w_uq, w_ukv, g_q, g_k):
    qd = MLA_NOPE + MLA_ROPE
    wq = w_uq.reshape(MLA_Q_RANK, MLA_HEADS, qd)
    wq = jnp.pad(wq, ((0, 0), (0, 0), (0, SLOT - qd))).reshape(MLA_Q_RANK, MLA_HEADS * SLOT)
    wkv = w_ukv.reshape(MLA_KV_RANK, MLA_HEADS, MLA_NOPE + MLA_V)
    wk = jnp.pad(wkv[:, :, :MLA_NOPE], ((0, 0), (0, 0), (0, SLOT - MLA_NOPE))).reshape(MLA_KV_RANK, MLA_HEADS * SLOT)
    wv = wkv[:, :, MLA_NOPE:].reshape(MLA_KV_RANK, MLA_HEADS * MLA_V)
    scale = LOG2E * float(qd) ** -0.5
    gq_slot = (jnp.pad(g_q, (0, SLOT - qd)) * scale).reshape(1, SLOT)
    gkn_slot = jnp.pad(g_k[:MLA_NOPE], (0, SLOT - MLA_NOPE)).reshape(1, SLOT)
    gkr_slot = jnp.pad(g_k[MLA_NOPE:], (MLA_NOPE, SLOT - qd)).reshape(1, SLOT)
    return wq.astype(BF16), jnp.concatenate([wk, wv], axis=1).astype(BF16), gq_slot, gkn_slot, gkr_slot


def kernel(x, c, pos, w_ada, b_ada, g_norm1, w_in, g_cq, w_uq, g_ckv, w_ukv, g_q_mla, g_k_mla, g_q_dil, g_k_dil,
           g_q_win, g_k_win, sink_win, w_br_mla, w_br_dil, w_br_win, w_out, g_norm2, w_gr, b_gr, w_er, b_er,
           w1, w3, w2):
    batch, seq, d = x.shape
    depth = w_ada.shape[0]
    t = batch * seq
    half = MLA_ROPE // 2
    inv_freq = ROPE_THETA ** (-jnp.arange(half, dtype=F32) / half)
    invf_slot = jnp.concatenate([jnp.zeros((MLA_NOPE,), F32), inv_freq, inv_freq,
                                 jnp.zeros((SLOT - MLA_NOPE - MLA_ROPE,), F32)]).reshape(1, SLOT)
    cos_t, sin_t = _rope_tables(pos.reshape(t, 1), invf_slot)
    mod = _modulation(c, w_ada, b_ada)
    w_in_k, w_dil_k = _layout_w_in(w_in)
    mseg_slot = _seg_matrix(SLOT, ((0, MLA_NOPE), (MLA_NOPE, MLA_ROPE)))
    mseg128 = _seg_matrix(128, tuple((k * HEAD_DIM, HEAD_DIM) for k in range(2)))
    head_scale = LOG2E * float(HEAD_DIM) ** -0.5
    win_bq = min(128, seq)

    x2d = x.reshape(t, d)
    for l in range(depth):
        mod_l = mod[l].reshape(batch, 6, d)
        g1 = g_norm1[l].reshape(1, d)
        cols = _in_projection(x2d, mod_l, g1, w_in_k[l], seq, BF16, 512, N_COLS, "in_projection")
        dil_cols = _in_projection(x2d, mod_l, g1, w_dil_k[l], seq, F32, 1024, N_DIL_COLS, "in_projection_dil")

        wuq, wukv, gq_slot, gkn_slot, gkr_slot = _layout_mla(w_uq[l], w_ukv[l], g_q_mla[l], g_k_mla[l])
        qm, km, vm = _mla_prep(cols, cos_t, sin_t, g_cq[l].reshape(1, -1), g_ckv[l].reshape(1, -1),
                               wuq, wukv, gq_slot, gkn_slot, gkr_slot, mseg_slot)
        o_mla = _mla_attention(qm, km, vm, batch, seq)

        gq_dil = (jnp.tile(g_q_dil[l], 2) * head_scale).reshape(1, -1)
        gk_dil = jnp.tile(g_k_dil[l], 2).reshape(1, -1)
        o_dil, lse_dil = [], []
        for gi in range(DIL_GROUPS):
            o_g, lse_g = _dilated_group(dil_cols, pos, gq_dil, gk_dil, mseg128, gi, batch, seq)
            o_dil.append(o_g)
            lse_dil.append(lse_g)

        gq_win = (jnp.tile(g_q_win[l], 2) * head_scale).reshape(1, -1)
        gk_win = jnp.tile(g_k_win[l], WIN_KV_HEADS).reshape(1, -1)
        sink_row = jnp.repeat(sink_win[l].astype(F32) * LOG2E, win_bq).reshape(1, WIN_HEADS * win_bq)
        o_win = _window_mixer(cols, pos, gq_win, gk_win, sink_row, mseg128, batch, seq)

        x2d = _merge(cols, o_mla, o_dil, lse_dil, o_win, x2d, mod_l, w_br_mla[l].astype(BF16),
                     w_br_dil[l].astype(BF16), w_br_win[l].astype(BF16), w_out[l].astype(BF16), seq)
        x2d = _moe(x2d, mod_l, g_norm2[l].reshape(1, d), w_gr[l], b_gr[l], w_er[l], b_er[l],
                   w1[l], w3[l], w2[l], seq)
    return x2d.reshape(batch, seq, d)
```

```python
import functools
import math

import jax
import jax.numpy as jnp
from jax import lax
from jax.experimental import pallas as pl
from jax.experimental.pallas import tpu as pltpu

F32 = jnp.float32
BF16 = jnp.bfloat16
I32 = jnp.int32

D_MODEL = 1024
HEAD_DIM = 64
NEG_INF = -1e30
EPS = 1e-6
LOG2E = math.log2(math.e)
LN2 = math.log(2.0)
MLA_HEADS = 8
MLA_Q_RANK = 512
MLA_KV_RANK = 256
MLA_NOPE = 64
MLA_ROPE = 32
MLA_V = 64
ROPE_THETA = 10000.0
DIL_PATTERNS = ((128, 1), (512, 4), (2048, 16))
DIL_GROUPS = 3
DIL_HPG = 4
DIL_HEADS = DIL_GROUPS * DIL_HPG
DIL_RADIUS = 64
WIN_HEADS = 8
WIN_KV_HEADS = 2
WIN_RADIUS = 128
N_EXPERT_GROUPS = 4
EXPERTS_PER_GROUP = 8
N_EXPERTS = N_EXPERT_GROUPS * EXPERTS_PER_GROUP
D_EXPERT = 384

LANES = 128
SLOT = 128
VMEM_LIMIT = 48 * 1024 * 1024

OFF_GATE = 0
OFF_CQ = 3072
OFF_WQ = 3584
OFF_CKV = 4096
OFF_WK = 4352
OFF_WV = 4480
OFF_KR = 4608
N_COLS = 4864
N_DIL_COLS = 3 * DIL_HEADS * HEAD_DIM

MOE_BM = 512
MOE_CHUNK = 16
MLA_TQ = 512
MLA_HPS = 8


def _cparams(sem, vmem=None, flags=None):
    return pltpu.CompilerParams(dimension_semantics=sem, vmem_limit_bytes=vmem, flags=flags)


def _tile(n, pref):
    t = min(n, pref)
    assert n % t == 0, (n, pref)
    return t


def _dot(a, b):
    return jnp.dot(a, b, preferred_element_type=F32)


def _dot_nt(a, b):
    return lax.dot_general(a, b, (((1,), (1,)), ((), ())), preferred_element_type=F32)


def _dot_tn(a, b):
    return lax.dot_general(a, b, (((0,), (0,)), ((), ())), preferred_element_type=F32)


def _seg_mean_sq(x, mseg):
    x2 = x * x
    hi = x2.astype(BF16)
    lo = (x2 - hi.astype(F32)).astype(BF16)
    return _dot(hi, mseg) + _dot(lo, mseg)


def _seg_norm(x, mseg, gain):
    return x * lax.rsqrt(_seg_mean_sq(x, mseg) + EPS) * gain


def _rms(x, gain):
    ms = jnp.mean(x * x, axis=-1, keepdims=True)
    return x * lax.rsqrt(ms + EPS) * gain


def _mod_kernel(c_ref, w_ref, b_ref, o_ref):
    c = c_ref[...]
    cond = (c * jax.nn.sigmoid(c)).astype(BF16)
    o_ref[0] = _dot(cond, w_ref[0].astype(BF16)) + b_ref[0]


def _modulation(c, w_ada, b_ada):
    depth, d, n = w_ada.shape
    b = c.shape[0]
    tn = _tile(n, 1536)
    return pl.pallas_call(
        _mod_kernel,
        out_shape=jax.ShapeDtypeStruct((depth, b, n), F32),
        grid=(depth, n // tn),
        in_specs=[
            pl.BlockSpec((b, d), lambda l, j: (0, 0)),
            pl.BlockSpec((1, d, tn), lambda l, j: (l, 0, j)),
            pl.BlockSpec((1, 1, tn), lambda l, j: (l, 0, j)),
        ],
        out_specs=pl.BlockSpec((1, b, tn), lambda l, j: (l, 0, j)),
        compiler_params=_cparams(("parallel", "parallel")),
        name="adaln_mod",
    )(c, w_ada, b_ada.reshape(depth, 1, n))


def _rope_table_kernel(pos_ref, invf_ref, cos_ref, sin_ref):
    ang = pos_ref[...].astype(F32) * invf_ref[...]
    cos_ref[...] = jnp.cos(ang)
    sin_ref[...] = jnp.sin(ang)


def _rope_tables(pos_col, invf_slot):
    t = pos_col.shape[0]
    tm = _tile(t, 1024)
    return pl.pallas_call(
        _rope_table_kernel,
        out_shape=(jax.ShapeDtypeStruct((t, SLOT), F32), jax.ShapeDtypeStruct((t, SLOT), F32)),
        grid=(t // tm,),
        in_specs=[pl.BlockSpec((tm, 1), lambda i: (i, 0)), pl.BlockSpec((1, SLOT), lambda i: (0, 0))],
        out_specs=(pl.BlockSpec((tm, SLOT), lambda i: (i, 0)), pl.BlockSpec((tm, SLOT), lambda i: (i, 0))),
        compiler_params=_cparams(("parallel",)),
        name="rope_tables",
    )(pos_col, invf_slot)


def _inproj_kernel(x_ref, mod_ref, g_ref, w_ref, o_ref, h_scr):
    @pl.when(pl.program_id(1) == 0)
    def _():
        m = mod_ref[0]
        h = _rms(x_ref[...], g_ref[...]) * (1.0 + m[1:2]) + m[0:1]
        h_scr[...] = h.astype(BF16)

    o_ref[...] = _dot(h_scr[...], w_ref[...]).astype(o_ref.dtype)


def _in_projection(x2d, mod_l, g_norm, w_in_l, seq, out_dtype, tm, tn, name):
    t, d = x2d.shape
    nc = w_in_l.shape[1]
    tm = _tile(seq, tm)
    assert nc % tn == 0
    per_b = seq // tm
    return pl.pallas_call(
        _inproj_kernel,
        out_shape=jax.ShapeDtypeStruct((t, nc), out_dtype),
        grid=(t // tm, nc // tn),
        in_specs=[
            pl.BlockSpec((tm, d), lambda i, j: (i, 0)),
            pl.BlockSpec((1, 6, d), lambda i, j: (i // per_b, 0, 0)),
            pl.BlockSpec((1, d), lambda i, j: (0, 0)),
            pl.BlockSpec((d, tn), lambda i, j: (0, j)),
        ],
        out_specs=pl.BlockSpec((tm, tn), lambda i, j: (i, j)),
        scratch_shapes=[pltpu.VMEM((tm, d), BF16)],
        compiler_params=_cparams(("parallel", "arbitrary"), VMEM_LIMIT),
        name=name,
    )(x2d, mod_l, g_norm, w_in_l)


def _mla_prep_kernel(cq_ref, ckv_ref, kr_ref, cos_ref, sin_ref, gcq_ref, gckv_ref, wuq_ref, wukv_ref,
                     gq_ref, gkn_ref, gkr_ref, mseg_ref, q_out, k_out, v_out):
    cos = cos_ref[...]
    sin = sin_ref[...]
    lane = lax.broadcasted_iota(I32, (1, SLOT), 1)
    s_neg = jnp.where((lane >= 64) & (lane < 80), -sin, 0.0)
    s_pos = jnp.where((lane >= 80) & (lane < 96), sin, 0.0)
    mseg = mseg_ref[...]

    def rope(xn):
        return xn * cos + pltpu.roll(xn, SLOT - 16, 1) * s_neg + pltpu.roll(xn, 16, 1) * s_pos

    cqn = _rms(cq_ref[...].astype(F32), gcq_ref[...]).astype(BF16)
    q = _dot(cqn, wuq_ref[...])
    for h in range(MLA_HEADS):
        sl = slice(h * SLOT, (h + 1) * SLOT)
        q_out[:, sl] = rope(_seg_norm(q[:, sl], mseg, gq_ref[...])).astype(q_out.dtype)

    ckvn = _rms(ckv_ref[...].astype(F32), gckv_ref[...]).astype(BF16)
    kv = _dot(ckvn, wukv_ref[...])
    kr = rope(_seg_norm(kr_ref[...].astype(F32), mseg, gkr_ref[...]))
    for h in range(MLA_HEADS):
        sl = slice(h * SLOT, (h + 1) * SLOT)
        k_out[:, sl] = (_seg_norm(kv[:, sl], mseg, gkn_ref[...]) + kr).astype(k_out.dtype)
    v_out[...] = kv[:, MLA_HEADS * SLOT:].astype(v_out.dtype)


def _mla_prep(cols, cos_t, sin_t, gcq, gckv, wuq, wukv, gq_slot, gkn_slot, gkr_slot, mseg):
    t = cols.shape[0]
    tm = _tile(t, 512)
    hs = MLA_HEADS * SLOT
    full = lambda shape: pl.BlockSpec(shape, lambda i: (0,) * len(shape))
    return pl.pallas_call(
        _mla_prep_kernel,
        out_shape=(jax.ShapeDtypeStruct((t, hs), BF16), jax.ShapeDtypeStruct((t, hs), BF16),
                   jax.ShapeDtypeStruct((t, MLA_HEADS * MLA_V), BF16)),
        grid=(t // tm,),
        in_specs=[
            pl.BlockSpec((tm, MLA_Q_RANK), lambda i: (i, OFF_CQ // MLA_Q_RANK)),
            pl.BlockSpec((tm, MLA_KV_RANK), lambda i: (i, OFF_CKV // MLA_KV_RANK)),
            pl.BlockSpec((tm, SLOT), lambda i: (i, OFF_KR // SLOT)),
            pl.BlockSpec((tm, SLOT), lambda i: (i, 0)),
            pl.BlockSpec((tm, SLOT), lambda i: (i, 0)),
            full((1, MLA_Q_RANK)), full((1, MLA_KV_RANK)),
            full((MLA_Q_RANK, hs)), full((MLA_KV_RANK, hs + MLA_HEADS * MLA_V)),
            full((1, SLOT)), full((1, SLOT)), full((1, SLOT)), full((SLOT, SLOT)),
        ],
        out_specs=(pl.BlockSpec((tm, hs), lambda i: (i, 0)), pl.BlockSpec((tm, hs), lambda i: (i, 0)),
                   pl.BlockSpec((tm, MLA_HEADS * MLA_V), lambda i: (i, 0))),
        compiler_params=_cparams(("parallel",), VMEM_LIMIT),
        name="mla_prep",
    )(cols, cols, cols, cos_t, sin_t, gcq, gckv, wuq, wukv, gq_slot, gkn_slot, gkr_slot, mseg)


def _mla_attn_kernel(q_ref, k_ref, v_ref, o_ref):
    lane = lax.broadcasted_iota(I32, (1, 2 * MLA_V), 1)
    for pair in range(MLA_HPS // 2):
        outs = []
        vp = v_ref[0, :, pair * 2 * MLA_V:(pair + 1) * 2 * MLA_V]
        for a in range(2):
            sl = slice((2 * pair + a) * SLOT, (2 * pair + a + 1) * SLOT)
            s = _dot_nt(q_ref[0, :, sl], k_ref[0, :, sl])
            m = jnp.max(s, axis=-1, keepdims=True)
            p = jnp.exp2(s - m)
            l = jnp.sum(p, axis=-1, keepdims=True)
            outs.append(_dot(p.astype(BF16), vp) / l)
        o_ref[0, :, pair * 2 * MLA_V:(pair + 1) * 2 * MLA_V] = jnp.where(lane < MLA_V, outs[0], outs[1]).astype(o_ref.dtype)


def _mla_attention(qm, km, vm, batch, seq):
    hs = MLA_HEADS * SLOT
    q3 = qm.reshape(batch, seq, hs)
    k3 = km.reshape(batch, seq, hs)
    v3 = vm.reshape(batch, seq, MLA_HEADS * MLA_V)
    tq = _tile(seq, MLA_TQ)
    out = pl.pallas_call(
        _mla_attn_kernel,
        out_shape=jax.ShapeDtypeStruct((batch, seq, MLA_HEADS * MLA_V), BF16),
        grid=(batch, MLA_HEADS // MLA_HPS, seq // tq),
        in_specs=[
            pl.BlockSpec((1, tq, MLA_HPS * SLOT), lambda b, p, i: (b, i, p)),
            pl.BlockSpec((1, seq, MLA_HPS * SLOT), lambda b, p, i: (b, 0, p)),
            pl.BlockSpec((1, seq, MLA_HPS * MLA_V), lambda b, p, i: (b, 0, p)),
        ],
        out_specs=pl.BlockSpec((1, tq, MLA_HPS * MLA_V), lambda b, p, i: (b, i, p)),
        compiler_params=_cparams(("parallel", "parallel", "arbitrary"), VMEM_LIMIT),
        name="mla_attention",
    )(q3, k3, v3)
    return out.reshape(batch * seq, MLA_HEADS * MLA_V)


WIN_BLOCKS_PER_STEP = 8
BAND_UNROLL = 16


def _window_start(i, bq, radius, n, kw):
    ws = jnp.clip(i * bq - radius, 0, n - kw)
    return pl.multiple_of(ws, 16)


MASKED_DIST = 1e30


def _masked_dist_t(i, bq, ws, kw, radius, pos_keys, pos_queries):
    keys = lax.broadcasted_iota(I32, (kw, bq), 0)
    queries = lax.broadcasted_iota(I32, (kw, bq), 1)
    rel = queries - keys + (i * bq - ws)
    d = pos_keys - pos_queries
    return jnp.where(jnp.maximum(rel, -rel) <= radius, jnp.maximum(d, -d).astype(F32), MASKED_DIST)


def _sub_positions(pos, batch, n, dil, bq):
    pos_sub = jnp.transpose(pos.reshape(batch, n, dil), (0, 2, 1))
    return pos_sub.reshape(batch, dil * n, 1), pos_sub.reshape(batch, dil * (n // bq), 1, bq)


def _dil_kernel(q0_ref, q1_ref, k0_ref, k1_ref, v0_ref, v1_ref, pcol_ref, prow_ref, gq_ref, gk_ref, mseg_ref,
                o_ref, lse_ref, kn_scr, vn_scr, *, n, bq, kw, dil, slopes):
    q_refs, k_refs, v_refs = (q0_ref, q1_ref), (k0_ref, k1_ref), (v0_ref, v1_ref)
    nblk = n // bq
    mseg = mseg_ref[...]
    gq = gq_ref[...]
    low_lane = lax.broadcasted_iota(I32, (1, 2 * HEAD_DIM), 1) < HEAD_DIM

    def rows(start, size):
        return pl.ds(start, size) if dil == 1 else pl.ds(start, size, stride=dil)

    def fill(r, carry):
        base = pl.multiple_of(r * n, 16)
        for hf in range(2):
            kn_scr[hf, pl.ds(base, n), :] = _seg_norm(k_refs[hf][0, rows(r, n), :], mseg, gk_ref[...]).astype(BF16)
            vn_scr[hf, pl.ds(base, n), :] = v_refs[hf][0, rows(r, n), :].astype(BF16)
        return carry

    lax.fori_loop(0, dil, fill, 0)

    def query_block(b, carry):
        r = b >> (nblk.bit_length() - 1)
        i = b & (nblk - 1)
        ws = _window_start(i, bq, DIL_RADIUS, n, kw)
        start = pl.multiple_of(r * n + ws, 16)
        q_rows = rows(r + i * (bq * dil), bq)
        dist = _masked_dist_t(i, bq, ws, kw, DIL_RADIUS, pcol_ref[0, pl.ds(start, kw), :], prow_ref[0, r * nblk + i])
        s_halves = []
        for hf in range(2):
            qn = _seg_norm(q_refs[hf][0, q_rows, :], mseg, gq)
            qs = jnp.concatenate([jnp.where(low_lane, qn, 0.0), jnp.where(low_lane, 0.0, qn)], axis=0).astype(BF16)
            bias = jnp.concatenate([slopes[2 * hf] * dist, slopes[2 * hf + 1] * dist], axis=1)
            s_halves.append(_dot_nt(kn_scr[hf, pl.ds(start, kw), :], qs) - bias)
        s = jnp.concatenate(s_halves, axis=1)
        m = jnp.max(s, axis=0, keepdims=True)
        p = jnp.exp2(s - m)
        l = jnp.sum(p, axis=0, keepdims=True)
        lse = (m + jnp.log2(l)) * LN2
        pb = p.astype(BF16)
        for hf in range(2):
            c0, c1, c2 = 2 * hf * bq, (2 * hf + 1) * bq, (2 * hf + 2) * bq
            ot = _dot_tn(vn_scr[hf, pl.ds(start, kw), :], pb[:, c0:c2]) / l[:, c0:c2]
            o_ref[0, hf, q_rows, :] = jnp.concatenate([ot[:HEAD_DIM, :bq], ot[HEAD_DIM:, bq:]], axis=0).T
            lse_ref[0, hf, q_rows, :] = jnp.concatenate([jnp.broadcast_to(lse[:, c0:c1], (HEAD_DIM, bq)),
                                                         jnp.broadcast_to(lse[:, c1:c2], (HEAD_DIM, bq))], axis=0).T
        return carry

    lax.fori_loop(0, dil * nblk, query_block, 0, unroll=min(dil * nblk, BAND_UNROLL))


def _dilated_group(dil_cols, pos, gq_row, gk_row, mseg, gi, batch, seq):
    _, dil = DIL_PATTERNS[gi]
    n = seq // dil
    bq = min(128, n)
    kw = min(bq + 2 * DIL_RADIUS, n)
    nblk = n // bq
    hw = 2 * HEAD_DIM
    cols3 = dil_cols.reshape(batch, seq, N_DIL_COLS)
    pcol, prow = _sub_positions(pos, batch, n, dil, bq)
    slopes = tuple(float(LOG2E * 2.0 ** (-8.0 * (gi * DIL_HPG + hh + 1) / DIL_HEADS)) for hh in range(DIL_HPG))
    kern = functools.partial(_dil_kernel, n=n, bq=bq, kw=kw, dil=dil, slopes=slopes)
    col = lambda which, hf: pl.BlockSpec((1, seq, hw), lambda b: (b, 0, (which * DIL_GROUPS + gi) * 2 + hf))
    small = lambda shape: pl.BlockSpec(shape, lambda b: (0,) * len(shape))
    halves = jax.ShapeDtypeStruct((batch, 2, seq, hw), F32)
    return pl.pallas_call(
        kern,
        out_shape=(halves, halves),
        grid=(batch,),
        in_specs=[
            col(0, 0), col(0, 1), col(1, 0), col(1, 1), col(2, 0), col(2, 1),
            pl.BlockSpec((1, seq, 1), lambda b: (b, 0, 0)),
            pl.BlockSpec((1, dil * nblk, 1, bq), lambda b: (b, 0, 0, 0)),
            small((1, hw)), small((1, hw)), small((hw, hw)),
        ],
        out_specs=(pl.BlockSpec((1, 2, seq, hw), lambda b: (b, 0, 0, 0)),
                   pl.BlockSpec((1, 2, seq, hw), lambda b: (b, 0, 0, 0))),
        scratch_shapes=[pltpu.VMEM((2, seq, hw), BF16), pltpu.VMEM((2, seq, hw), BF16)],
        compiler_params=_cparams(("parallel",), VMEM_LIMIT),
        name=f"dilated_group{gi}",
    )(cols3, cols3, cols3, cols3, cols3, cols3, pcol, prow, gq_row, gk_row, mseg)


def _win_kernel(q_ref, k_ref, v_ref, pcol_ref, prow_ref, gq_ref, gk_ref, sink_ref, mseg_ref, o_ref, kn_scr,
                *, n, bq, kw, slopes):
    step = pl.program_id(1)
    mseg = mseg_ref[...]

    @pl.when(step == 0)
    def _():
        kn_scr[...] = _seg_norm(k_ref[0].astype(F32), mseg, gk_ref[...]).astype(BF16)

    low_lane = lax.broadcasted_iota(I32, (1, 2 * HEAD_DIM), 1) < HEAD_DIM
    rep = WIN_HEADS // WIN_KV_HEADS
    sink = sink_ref[...]
    per_step = q_ref.shape[1] // bq
    for j in range(per_step):
        i = step * per_step + j
        rows = slice(j * bq, (j + 1) * bq)
        ws = _window_start(i, bq, WIN_RADIUS, n, kw)
        kwin = kn_scr[pl.ds(ws, kw), :]
        vwin = v_ref[0, pl.ds(ws, kw), :]
        dist = _masked_dist_t(i, bq, ws, kw, WIN_RADIUS, pcol_ref[0, pl.ds(ws, kw), :], prow_ref[0, j])

        qn_pairs = [_seg_norm(q_ref[0, rows, pair * 128:(pair + 1) * 128].astype(F32), mseg, gq_ref[...])
                    for pair in range(WIN_HEADS // 2)]
        q_parts = []
        for h in range(WIN_HEADS):
            pair, upper = divmod(h, 2)
            qm = jnp.where(low_lane, 0.0, qn_pairs[pair]) if upper else jnp.where(low_lane, qn_pairs[pair], 0.0)
            if upper != h // rep:
                qm = pltpu.roll(qm, HEAD_DIM, 1)
            q_parts.append(qm.astype(BF16))
        qs = jnp.concatenate(q_parts, axis=0)
        bias = jnp.concatenate([slopes[h] * dist for h in range(WIN_HEADS)], axis=1)
        s = _dot_nt(kwin, qs) - bias
        m = jnp.maximum(jnp.max(s, axis=0, keepdims=True), sink)
        e = jnp.exp2(s - m)
        den = jnp.sum(e, axis=0, keepdims=True) + jnp.exp2(sink - m)
        ot = _dot_tn(vwin, e.astype(BF16)) / den
        for pair in range(WIN_HEADS // 2):
            slabs = []
            for h in (2 * pair, 2 * pair + 1):
                g = h // rep
                slabs.append(ot[g * HEAD_DIM:(g + 1) * HEAD_DIM, h * bq:(h + 1) * bq])
            o_ref[0, rows, pair * 128:(pair + 1) * 128] = jnp.concatenate(slabs, axis=0).T.astype(o_ref.dtype)


def _window_mixer(cols, pos, gq_row, gk_row, sink_row, mseg128, batch, seq):
    n = seq
    bq = min(128, n)
    kw = min(bq + 2 * WIN_RADIUS, n)
    nblk = n // bq
    qw = WIN_HEADS * HEAD_DIM
    kvw = WIN_KV_HEADS * HEAD_DIM
    cols_v = cols.reshape(batch, seq, N_COLS)
    pcol = pos.reshape(batch, seq, 1)
    prow = pos.reshape(batch, nblk, 1, bq)
    slopes = tuple(float(LOG2E * 2.0 ** (-8.0 * (h + 1) / WIN_HEADS)) for h in range(WIN_HEADS))
    kern = functools.partial(_win_kernel, n=n, bq=bq, kw=kw, slopes=slopes)
    small = lambda shape: pl.BlockSpec(shape, lambda b, i: (0,) * len(shape))
    per_step = min(WIN_BLOCKS_PER_STEP, nblk)
    o = pl.pallas_call(
        kern,
        out_shape=jax.ShapeDtypeStruct((batch, seq, qw), BF16),
        grid=(batch, nblk // per_step),
        in_specs=[
            pl.BlockSpec((1, per_step * bq, qw), lambda b, i: (b, i, OFF_WQ // qw)),
            pl.BlockSpec((1, n, kvw), lambda b, i: (b, 0, OFF_WK // kvw)),
            pl.BlockSpec((1, n, kvw), lambda b, i: (b, 0, OFF_WV // kvw)),
            pl.BlockSpec((1, n, 1), lambda b, i: (b, 0, 0)),
            pl.BlockSpec((1, per_step, 1, bq), lambda b, i: (b, i, 0, 0)),
            small((1, kvw)), small((1, kvw)), small((1, WIN_HEADS * bq)), small((kvw, kvw)),
        ],
        out_specs=pl.BlockSpec((1, per_step * bq, qw), lambda b, i: (b, i, 0)),
        scratch_shapes=[pltpu.VMEM((n, kvw), BF16)],
        compiler_params=_cparams(("parallel", "arbitrary"), VMEM_LIMIT),
        name="window_mixer",
    )(cols_v, cols_v, cols_v, pcol, prow, gq_row, gk_row, sink_row, mseg128)
    return o.reshape(batch * seq, qw)


def _merge_kernel(gm_ref, gd_ref, gw_ref, om_ref, od0_ref, od1_ref, od2_ref, l0_ref, l1_ref, l2_ref, ow_ref,
                  x_ref, mod_ref, wm_ref, wd_ref, ww_ref, wo_ref, o_ref):
    def heads(ref):
        return jnp.concatenate([ref[0, 0], ref[0, 1]], axis=1)

    l0, l1, l2 = heads(l0_ref), heads(l1_ref), heads(l2_ref)
    m = jnp.maximum(jnp.maximum(l0, l1), l2)
    e0, e1, e2 = jnp.exp(l0 - m), jnp.exp(l1 - m), jnp.exp(l2 - m)
    od = (e0 * heads(od0_ref) + e1 * heads(od1_ref) + e2 * heads(od2_ref)) / (e0 + e1 + e2)
    y = jax.nn.sigmoid(gm_ref[...].astype(F32)) * _dot(om_ref[...], wm_ref[...])
    y += jax.nn.sigmoid(gd_ref[...].astype(F32)) * _dot(od.astype(BF16), wd_ref[...])
    y += jax.nn.sigmoid(gw_ref[...].astype(F32)) * _dot(ow_ref[...], ww_ref[...])
    z = _dot(y.astype(BF16), wo_ref[...])
    o_ref[...] = x_ref[...] + mod_ref[0][2:3] * z


def _merge(cols, o_mla, o_dil, lse_dil, o_win, x2d, mod_l, wm, wd, ww, wo, seq):
    t, d = x2d.shape
    tm = _tile(seq, 512)
    per_b = seq // tm
    row = lambda w, j=0: pl.BlockSpec((tm, w), lambda i: (i, j))
    full = lambda a: pl.BlockSpec(a.shape, lambda i: (0, 0))
    dil = pl.BlockSpec((1, 2, tm, 2 * HEAD_DIM), lambda i: (i // per_b, 0, i % per_b, 0))
    return pl.pallas_call(
        _merge_kernel,
        out_shape=jax.ShapeDtypeStruct((t, d), F32),
        grid=(t // tm,),
        in_specs=[row(d, 0), row(d, 1), row(d, 2), row(o_mla.shape[1]),
                  dil, dil, dil, dil, dil, dil, row(o_win.shape[1]),
                  row(d), pl.BlockSpec((1, 6, d), lambda i: (i // per_b, 0, 0)),
                  full(wm), full(wd), full(ww), full(wo)],
        out_specs=row(d),
        compiler_params=_cparams(("parallel",), VMEM_LIMIT),
        name="merge_out_proj",
    )(cols, cols, cols, o_mla, o_dil[0], o_dil[1], o_dil[2], lse_dil[0], lse_dil[1], lse_dil[2], o_win,
      x2d, mod_l, wm, wd, ww, wo)


def _router_kernel(x_ref, mod_ref, g_ref, whi_ref, wlo_ref, br_ref, upper_ref, hs_ref, pos_ref, gate_ref, tab_ref,
                   *, lr):
    m = mod_ref[0]
    h = _rms(x_ref[...], g_ref[...]) * (1.0 + m[4:5]) + m[3:4]
    tm = h.shape[0]
    h_hi = h.astype(BF16)
    h_lo = (h - h_hi.astype(F32)).astype(BF16)
    logits = _dot(h_hi, whi_ref[...]) + (_dot(h_hi, wlo_ref[...]) + _dot(h_lo, whi_ref[...])) + br_ref[...]
    lane = lax.broadcasted_iota(I32, (tm, LANES), 1)
    lane_f = lane.astype(F32)

    def first_argmax(vals, mx):
        return jnp.min(jnp.where(vals == mx, lane_f, float(LANES)), axis=-1, keepdims=True).astype(I32)

    lg = jnp.where(lane < N_EXPERT_GROUPS, logits, NEG_INF)
    mg = jnp.max(lg, axis=-1, keepdims=True)
    g_w = 1.0 / jnp.sum(jnp.exp(lg - mg), axis=-1, keepdims=True)
    g_idx = first_argmax(lg, mg)
    eid = lane - N_EXPERT_GROUPS
    in_grp = (eid >= 0) & (eid < N_EXPERTS) & ((eid >> 3) == g_idx)
    le = jnp.where(in_grp, logits, NEG_INF)
    m1 = jnp.max(le, axis=-1, keepdims=True)
    i1 = first_argmax(le, m1)
    le2 = jnp.where(lane == i1, NEG_INF, le)
    m2 = jnp.max(le2, axis=-1, keepdims=True)
    i2 = first_argmax(le2, m2)
    r = jnp.exp(m2 - m1)
    gate1 = g_w / (1.0 + r)
    gate2 = g_w * r / (1.0 + r)
    e1 = i1 - N_EXPERT_GROUPS
    e2 = i2 - N_EXPERT_GROUPS
    hit1 = lane == e1
    hit2 = lane == e2
    onehot = jnp.where(hit1 | hit2, 1.0, 0.0)
    rows = lax.broadcasted_iota(I32, (tm, tm), 0)
    cols = lax.broadcasted_iota(I32, (tm, tm), 1)
    before = jnp.where(rows > cols, 1.0, 0.0).astype(BF16)
    rank = _dot(before, onehot.astype(BF16))
    run_len = (jnp.sum(onehot, axis=0, keepdims=True).astype(I32) + (MOE_CHUNK - 1)) & jnp.int32(-MOE_CHUNK)
    run_off = _dot(jnp.broadcast_to(run_len.astype(F32), (8, LANES)).astype(BF16), upper_ref[...])[0:1]
    pos1 = jnp.sum(jnp.where(hit1, rank + run_off, 0.0), axis=-1, keepdims=True).astype(I32)
    pos2 = jnp.sum(jnp.where(hit2, rank + run_off, 0.0), axis=-1, keepdims=True).astype(I32)
    pos_lanes = jnp.where(lane == 0, pos1, jnp.where(lane == 1, pos2, 0))
    pos_rows = pos_lanes.astype(F32).T.astype(I32)
    local = lax.broadcasted_iota(I32, (lr, tm), 0)
    place = jnp.where((local == pos_rows[0:1]) | (local == pos_rows[1:2]), 1.0, 0.0).astype(BF16)
    hs_ref[0] = _dot(place, h_hi).astype(BF16)
    pos_ref[...] = pos_lanes
    gate_ref[...] = jnp.where(lane == 0, gate1, jnp.where(lane == 1, gate2, 0.0))
    sub = lax.broadcasted_iota(I32, (8, LANES), 0)
    tab_ref[0] = jnp.where(sub == 0, run_len, jnp.where(sub == 1, run_off.astype(I32), 0))


def _router(x2d, mod_l, g_norm, w_router, b_router, seq, tm, lr):
    t, d = x2d.shape
    per_b = seq // tm
    nt = t // tm
    row = lambda w: pl.BlockSpec((tm, w), lambda i: (i, 0))
    idx = jnp.arange(LANES)
    upper = jnp.where(idx[:, None] < idx[None, :], 1.0, 0.0).astype(BF16)
    w_hi = w_router.astype(BF16)
    w_lo = (w_router - w_hi.astype(F32)).astype(BF16)
    return pl.pallas_call(
        functools.partial(_router_kernel, lr=lr),
        out_shape=(jax.ShapeDtypeStruct((nt, lr, d), BF16), jax.ShapeDtypeStruct((t, LANES), I32),
                   jax.ShapeDtypeStruct((t, LANES), F32), jax.ShapeDtypeStruct((nt, 8, LANES), I32)),
        grid=(nt,),
        in_specs=[row(d), pl.BlockSpec((1, 6, d), lambda i: (i // per_b, 0, 0)),
                  pl.BlockSpec((1, d), lambda i: (0, 0)), pl.BlockSpec((d, LANES), lambda i: (0, 0)),
                  pl.BlockSpec((d, LANES), lambda i: (0, 0)),
                  pl.BlockSpec((1, LANES), lambda i: (0, 0)), pl.BlockSpec((LANES, LANES), lambda i: (0, 0))],
        out_specs=(pl.BlockSpec((1, lr, d), lambda i: (i, 0, 0)), row(LANES), row(LANES),
                   pl.BlockSpec((1, 8, LANES), lambda i: (i, 0, 0))),
        compiler_params=_cparams(("parallel",), VMEM_LIMIT),
        name="moe_router",
    )(x2d, mod_l, g_norm, w_hi, w_lo, b_router, upper)


def _for_each_chunk(off_ref, n_ref, base_ref, tile, fn):
    def per_expert(e, carry):
        k = tile * N_EXPERTS + e
        off, base = off_ref[k], base_ref[k]

        def per_chunk(c, carry2):
            fn(pl.multiple_of(off + c * MOE_CHUNK, MOE_CHUNK), pl.multiple_of(base + c * MOE_CHUNK, MOE_CHUNK))
            return carry2

        return lax.fori_loop(0, n_ref[k], per_chunk, carry)

    lax.fori_loop(0, N_EXPERTS, per_expert, 0)


def _scatter_kernel(off_ref, n_ref, base_ref, hs_ref, rows_in_ref, rows_ref, sem):
    del rows_in_ref
    tile = pl.program_id(0)

    def chunk_copy(local, glob):
        return pltpu.make_async_copy(hs_ref.at[0, pl.ds(local, MOE_CHUNK)], rows_ref.at[pl.ds(glob, MOE_CHUNK)], sem)

    _for_each_chunk(off_ref, n_ref, base_ref, tile, lambda a, b: chunk_copy(a, b).start())
    _for_each_chunk(off_ref, n_ref, base_ref, tile, lambda a, b: chunk_copy(a, b).wait())


def _scatter_rows(run_off, run_chunks, base, hs, n_rows):
    nt, lr, w = hs.shape
    zeros = jnp.zeros((n_rows, w), hs.dtype)
    grid_spec = pltpu.PrefetchScalarGridSpec(
        num_scalar_prefetch=3,
        grid=(nt,),
        in_specs=[pl.BlockSpec((1, lr, w), lambda i, *_: (i, 0, 0)), pl.BlockSpec(memory_space=pl.ANY)],
        out_specs=pl.BlockSpec(memory_space=pl.ANY),
        scratch_shapes=[pltpu.SemaphoreType.DMA],
    )
    return pl.pallas_call(
        _scatter_kernel,
        out_shape=jax.ShapeDtypeStruct((n_rows, w), hs.dtype),
        grid_spec=grid_spec,
        input_output_aliases={4: 0},
        compiler_params=_cparams(("arbitrary",)),
        name="moe_scatter",
    )(run_off, run_chunks, base, hs, zeros)


def _ffn_kernel(be_ref, nu_ref, rows_ref, w1_ref, w3_ref, w2_ref, y_ref, w13_scr, w2_scr):
    j = pl.program_id(0)
    used = j < nu_ref[0]
    new_expert = (j == 0) | (be_ref[j] != be_ref[jnp.maximum(j - 1, 0)])

    @pl.when(used & new_expert)
    def _():
        w13_scr[:, :D_EXPERT] = w1_ref[0].astype(BF16)
        w13_scr[:, D_EXPERT:] = w3_ref[0].astype(BF16)
        w2_scr[...] = w2_ref[0].astype(BF16)

    @pl.when(used)
    def _():
        h = _dot(rows_ref[...], w13_scr[...])
        a = h[:, :D_EXPERT]
        act = a * jax.nn.sigmoid(a) * h[:, D_EXPERT:]
        y_ref[...] = _dot(act.astype(BF16), w2_scr[...]).astype(y_ref.dtype)

    @pl.when(jnp.logical_not(used))
    def _():
        y_ref[...] = jnp.zeros_like(y_ref)


def _grouped_ffn(block_expert, n_used, rows, w1, w3, w2):
    n_rows, w = rows.shape
    d = w1.shape[1]
    nb = n_rows // MOE_BM
    grid_spec = pltpu.PrefetchScalarGridSpec(
        num_scalar_prefetch=2,
        grid=(nb,),
        in_specs=[
            pl.BlockSpec((MOE_BM, w), lambda j, be, nu: (j, 0)),
            pl.BlockSpec((1, d, D_EXPERT), lambda j, be, nu: (be[j], 0, 0)),
            pl.BlockSpec((1, d, D_EXPERT), lambda j, be, nu: (be[j], 0, 0)),
            pl.BlockSpec((1, D_EXPERT, d), lambda j, be, nu: (be[j], 0, 0)),
        ],
        out_specs=pl.BlockSpec((MOE_BM, w), lambda j, be, nu: (j, 0)),
        scratch_shapes=[pltpu.VMEM((d, 2 * D_EXPERT), BF16), pltpu.VMEM((D_EXPERT, d), BF16)],
    )
    return pl.pallas_call(
        _ffn_kernel,
        out_shape=jax.ShapeDtypeStruct((n_rows, w), BF16),
        grid_spec=grid_spec,
        compiler_params=_cparams(("arbitrary",), VMEM_LIMIT),
        name="moe_grouped_ffn",
    )(block_expert, n_used, rows, w1, w3, w2)


def _combine_kernel(off_ref, n_ref, base_ref, y_ref, pos_ref, gate_ref, x_ref, mod_ref, o_ref, ybuf, sem):
    tile = pl.program_id(0)
    slot = tile % 2

    def chunk_copy(s, local, glob):
        return pltpu.make_async_copy(y_ref.at[pl.ds(glob, MOE_CHUNK)], ybuf.at[s, pl.ds(local, MOE_CHUNK)], sem.at[s])

    def request(t, s):
        ybuf[s] = jnp.zeros(ybuf.shape[1:], ybuf.dtype)
        _for_each_chunk(off_ref, n_ref, base_ref, t, lambda a, b: chunk_copy(s, a, b).start())

    @pl.when(tile == 0)
    def _():
        request(tile, slot)

    @pl.when(tile + 1 < pl.num_programs(0))
    def _():
        request(tile + 1, 1 - slot)

    _for_each_chunk(off_ref, n_ref, base_ref, tile, lambda a, b: chunk_copy(slot, a, b).wait())
    tm = x_ref.shape[0]
    pos, g = pos_ref[...], gate_ref[...]
    local = lax.broadcasted_iota(I32, (tm, ybuf.shape[1]), 1)
    pick = (jnp.where(local == pos[:, 0:1], g[:, 0:1], 0.0)
            + jnp.where(local == pos[:, 1:2], g[:, 1:2], 0.0)).astype(BF16)
    moe = _dot(pick, ybuf[slot])
    o_ref[...] = x_ref[...] + mod_ref[0][5:6] * moe


def _combine(run_off, run_chunks, base, y, pos, gates, x2d, mod_l, seq, tm, lr):
    t, d = x2d.shape
    per_b = seq // tm
    grid_spec = pltpu.PrefetchScalarGridSpec(
        num_scalar_prefetch=3,
        grid=(t // tm,),
        in_specs=[pl.BlockSpec(memory_space=pl.ANY),
                  pl.BlockSpec((tm, LANES), lambda i, *_: (i, 0)), pl.BlockSpec((tm, LANES), lambda i, *_: (i, 0)),
                  pl.BlockSpec((tm, d), lambda i, *_: (i, 0)),
                  pl.BlockSpec((1, 6, d), lambda i, *_: (i // per_b, 0, 0))],
        out_specs=pl.BlockSpec((tm, d), lambda i, *_: (i, 0)),
        scratch_shapes=[pltpu.VMEM((2, lr, y.shape[1]), y.dtype), pltpu.SemaphoreType.DMA((2,))],
    )
    return pl.pallas_call(
        _combine_kernel,
        out_shape=jax.ShapeDtypeStruct((t, d), F32),
        grid_spec=grid_spec,
        compiler_params=_cparams(("arbitrary",), VMEM_LIMIT),
        name="moe_combine",
    )(run_off, run_chunks, base, y, pos, gates, x2d, mod_l)


def _moe(x2d, mod_l, g_norm2, w_gr, b_gr, w_er, b_er, w1, w3, w2, layer, seq):
    t, d = x2d.shape
    pad = LANES - N_EXPERT_GROUPS - N_EXPERTS
    w_router = jnp.concatenate([w_gr, w_er, jnp.zeros((d, pad), F32)], axis=1)
    b_router = jnp.concatenate([b_gr, b_er, jnp.zeros((pad,), F32)]).reshape(1, LANES)
    tm = _tile(seq, 512)
    nt = t // tm
    lr = 2 * tm + N_EXPERTS * MOE_CHUNK
    hs, pos, gates, tab = _router(x2d, mod_l, g_norm2, w_router, b_router, seq, tm, lr)

    run_len = tab[:, 0, :N_EXPERTS]
    run_off = tab[:, 1, :N_EXPERTS]
    total = jnp.sum(run_len, axis=0)
    padded = (total + MOE_BM - 1) // MOE_BM * MOE_BM
    ends = jnp.cumsum(padded)
    base = (ends - padded)[None, :] + jnp.cumsum(run_len, axis=0) - run_len
    nb = (2 * t + nt * N_EXPERTS * MOE_CHUNK) // MOE_BM + N_EXPERTS
    block_start = jnp.arange(nb, dtype=I32) * MOE_BM
    block_expert = jnp.minimum(jnp.sum(block_start[:, None] >= ends[None, :], axis=1), N_EXPERTS - 1).astype(I32)
    n_used = (ends[-1] // MOE_BM).astype(I32).reshape(1)
    flat = lambda a: a.reshape(-1).astype(I32)
    run_off, run_chunks, base = flat(run_off), flat(run_len // MOE_CHUNK), flat(base)

    rows = _scatter_rows(run_off, run_chunks, base, hs, nb * MOE_BM)
    stacked = lambda w: w.reshape((-1,) + w.shape[2:])
    y = _grouped_ffn(block_expert + layer * N_EXPERTS, n_used, rows, stacked(w1), stacked(w3), stacked(w2))
    return _combine(run_off, run_chunks, base, y, pos, gates, x2d, mod_l, seq, tm, lr)


def _seg_matrix(width, segments):
    idx = jnp.arange(width)
    m = jnp.zeros((width, width), F32)
    for start, length in segments:
        inside = (idx >= start) & (idx < start + length)
        m = m + jnp.where(inside[:, None] & inside[None, :], 1.0 / length, 0.0)
    return m.astype(BF16)


def _layout_kernel(w_ref, main_ref, dil_ref):
    sizes = [MLA_Q_RANK, MLA_KV_RANK, MLA_ROPE, N_DIL_COLS,
             (WIN_HEADS + 2 * WIN_KV_HEADS) * HEAD_DIM, 3 * D_MODEL]
    starts = [sum(sizes[:k]) for k in range(len(sizes))]
    seg = lambda k, a=0, b=None: w_ref[0, :, starts[k] + a:starts[k] + (sizes[k] if b is None else b)].astype(BF16)
    qw = WIN_HEADS * HEAD_DIM
    rows = w_ref.shape[1]
    main_ref[0, :, OFF_GATE:OFF_GATE + sizes[5]] = seg(5)
    main_ref[0, :, OFF_CQ:OFF_CQ + sizes[0]] = seg(0)
    main_ref[0, :, OFF_WQ:OFF_WQ + qw] = seg(4, 0, qw)
    main_ref[0, :, OFF_CKV:OFF_CKV + sizes[1]] = seg(1)
    main_ref[0, :, OFF_WK:OFF_WK + sizes[4] - qw] = seg(4, qw)
    tail = jnp.concatenate([jnp.zeros((rows, MLA_NOPE), BF16), seg(2),
                            jnp.zeros((rows, N_COLS - OFF_KR - MLA_NOPE - MLA_ROPE), BF16)], axis=1)
    main_ref[0, :, OFF_KR:] = tail
    dil_ref[0] = seg(3)


def _layout_w_in(w_in):
    depth, d, n = w_in.shape
    tr = _tile(d, 256)
    return pl.pallas_call(
        _layout_kernel,
        out_shape=(jax.ShapeDtypeStruct((depth, d, N_COLS), BF16), jax.ShapeDtypeStruct((depth, d, N_DIL_COLS), BF16)),
        grid=(depth, d // tr),
        in_specs=[pl.BlockSpec((1, tr, n), lambda l, i: (l, i, 0))],
        out_specs=(pl.BlockSpec((1, tr, N_COLS), lambda l, i: (l, i, 0)),
                   pl.BlockSpec((1, tr, N_DIL_COLS), lambda l, i: (l, i, 0))),
        compiler_params=_cparams(("parallel", "parallel"), VMEM_LIMIT),
        name="w_in_layout",
    )(w_in)


def _layout_mla(w_uq, w_ukv, g_q, g_k):
    qd = MLA_NOPE + MLA_ROPE
    wq = w_uq.reshape(MLA_Q_RANK, MLA_HEADS, qd)
    wq = jnp.pad(wq, ((0, 0), (0, 0), (0, SLOT - qd))).reshape(MLA_Q_RANK, MLA_HEADS * SLOT)
    wkv = w_ukv.reshape(MLA_KV_RANK, MLA_HEADS, MLA_NOPE + MLA_V)
    wk = jnp.pad(wkv[:, :, :MLA_NOPE], ((0, 0), (0, 0), (0, SLOT - MLA_NOPE))).reshape(MLA_KV_RANK, MLA_HEADS * SLOT)
    wv = wkv[:, :, MLA_NOPE:].reshape(MLA_KV_RANK, MLA_HEADS * MLA_V)
    scale = LOG2E * float(qd) ** -0.5
    gq_slot = (jnp.pad(g_q, (0, SLOT - qd)) * scale).reshape(1, SLOT)
    gkn_slot = jnp.pad(g_k[:MLA_NOPE], (0, SLOT - MLA_NOPE)).reshape(1, SLOT)
    gkr_slot = jnp.pad(g_k[MLA_NOPE:], (MLA_NOPE, SLOT - qd)).reshape(1, SLOT)
    return wq.astype(BF16), jnp.concatenate([wk, wv], axis=1).astype(BF16), gq_slot, gkn_slot, gkr_slot


def kernel(x, c, pos, w_ada, b_ada, g_norm1, w_in, g_cq, w_uq, g_ckv, w_ukv, g_q_mla, g_k_mla, g_q_dil, g_k_dil,
           g_q_win, g_k_win, sink_win, w_br_mla, w_br_dil, w_br_win, w_out, g_norm2, w_gr, b_gr, w_er, b_er,
           w1, w3, w2):
    batch, seq, d = x.shape
    depth = w_ada.shape[0]
    t = batch * seq
    half = MLA_ROPE // 2
    inv_freq = ROPE_THETA ** (-jnp.arange(half, dtype=F32) / half)
    invf_slot = jnp.concatenate([jnp.zeros((MLA_NOPE,), F32), inv_freq, inv_freq,
                                 jnp.zeros((SLOT - MLA_NOPE - MLA_ROPE,), F32)]).reshape(1, SLOT)
    cos_t, sin_t = _rope_tables(pos.reshape(t, 1), invf_slot)
    mod = _modulation(c, w_ada, b_ada)
    w_in_k, w_dil_k = _layout_w_in(w_in)
    mseg_slot = _seg_matrix(SLOT, ((0, MLA_NOPE), (MLA_NOPE, MLA_ROPE)))
    mseg128 = _seg_matrix(128, tuple((k * HEAD_DIM, HEAD_DIM) for k in range(2)))
    head_scale = LOG2E * float(HEAD_DIM) ** -0.5
    win_bq = min(128, seq)

    x2d = x.reshape(t, d)
    for l in range(depth):
        mod_l = mod[l].reshape(batch, 6, d)
        g1 = g_norm1[l].reshape(1, d)
        cols = _in_projection(x2d, mod_l, g1, w_in_k[l], seq, BF16, 512, N_COLS, "in_projection")
        dil_cols = _in_projection(x2d, mod_l, g1, w_dil_k[l], seq, F32, 1024, N_DIL_COLS, "in_projection_dil")

        wuq, wukv, gq_slot, gkn_slot, gkr_slot = _layout_mla(w_uq[l], w_ukv[l], g_q_mla[l], g_k_mla[l])
        qm, km, vm = _mla_prep(cols, cos_t, sin_t, g_cq[l].reshape(1, -1), g_ckv[l].reshape(1, -1),
                               wuq, wukv, gq_slot, gkn_slot, gkr_slot, mseg_slot)
        o_mla = _mla_attention(qm, km, vm, batch, seq)

        gq_dil = (jnp.tile(g_q_dil[l], 2) * head_scale).reshape(1, -1)
        gk_dil = jnp.tile(g_k_dil[l], 2).reshape(1, -1)
        o_dil, lse_dil = [], []
        for gi in range(DIL_GROUPS):
            o_g, lse_g = _dilated_group(dil_cols, pos, gq_dil, gk_dil, mseg128, gi, batch, seq)
            o_dil.append(o_g)
            lse_dil.append(lse_g)

        gq_win = (jnp.tile(g_q_win[l], 2) * head_scale).reshape(1, -1)
        gk_win = jnp.tile(g_k_win[l], WIN_KV_HEADS).reshape(1, -1)
        sink_row = jnp.repeat(sink_win[l].astype(F32) * LOG2E, win_bq).reshape(1, WIN_HEADS * win_bq)
        o_win = _window_mixer(cols, pos, gq_win, gk_win, sink_row, mseg128, batch, seq)

        x2d = _merge(cols, o_mla, o_dil, lse_dil, o_win, x2d, mod_l, w_br_mla[l].astype(BF16),
                     w_br_dil[l].astype(BF16), w_br_win[l].astype(BF16), w_out[l].astype(BF16), seq)
        x2d = _moe(x2d, mod_l, g_norm2[l].reshape(1, d), w_gr[l], b_gr[l], w_er[l], b_er[l],
                   w1, w3, w2, l, seq)
    return x2d.reshape(batch, seq, d)
```
